```python
import math
import jax
import jax.numpy as jnp
from jax import lax
import numpy as np

D_MODEL = 1024
BATCH = 32
SEQ = 256
DEPTH = 2
DEC_BATCH = 8
DEC_SEQ = 1024
PAST_LEN = 512

GRID_W = 64
N_EVEN = (DEPTH + 1) // 2
N_ODD = DEPTH // 2
N_MOD = 6
NORM_EPS = 1e-6

HY_W = D_MODEL // 2
HY_ORDER = 2
HY_SHORT = 3
HY_EMB = 33
HY_BANDS = (HY_EMB - 1) // 2
HY_FFN = 64
HY_TARGET = 1e-2
HY_FAST = 0.3
HY_SLOW = 1.5

RW_W = D_MODEL // 2
RW_N = 64
RW_H = RW_W // RW_N
RW_WLORA = 32
RW_ALORA = 32
RW_GLORA = 96
RW_LN_EPS = 64e-5
RW_PROJ = 3 * RW_W + 2 * RW_WLORA + 2 * RW_ALORA + RW_GLORA
P_EVEN = 3 * HY_W + RW_PROJ

GD_H = 8
GD_DK = 128
GD_DV = 128
GD_QK = GD_H * GD_DK
GD_W = GD_H * GD_DV
GD_CONV = 3
GD_CHUNK = 64
P_ODD = 2 * GD_QK + 2 * GD_W + 4 * GD_H

D_FF = 2816
N_EXP = 8
TOP_K = 2
E_FF = 3584

kernel_name = 'hybrid_hyena_rwkv7_gdn_diffusion_step'


def _split(x, sizes):
    return jnp.split(x, np.cumsum(sizes)[:-1].tolist(), axis=-1)


def rmsnorm(x, g):
    xf = x.astype(jnp.float32)
    y = xf * lax.rsqrt(jnp.mean(xf * xf, -1, keepdims=True) + NORM_EPS)
    return (y * g.astype(jnp.float32)).astype(x.dtype)


def dwconv_centred(x, w):
    K = w.shape[0]
    p = K // 2
    L = x.shape[1]
    xp = jnp.pad(x, ((0, 0), (p, p), (0, 0)))
    y = xp[:, 0:L] * w[0]
    for j in range(1, K):
        y = y + xp[:, j:j + L] * w[j]
    return y


def grid_pos_embed(n_tokens):
    rows = n_tokens // GRID_W
    t = jnp.arange(rows * GRID_W)
    row = (t // GRID_W).astype(jnp.float32)
    col = (t % GRID_W).astype(jnp.float32)
    q = D_MODEL // 4
    omega = jnp.exp(-math.log(10000.0) * jnp.arange(q, dtype=jnp.float32) / q)
    enc = lambda pos: jnp.concatenate([jnp.sin(pos[:, None] * omega), jnp.cos(pos[:, None] * omega)], -1)
    return jnp.concatenate([enc(row), enc(col)], -1)


def hyena_filters(L, w1, b1, f1, w2, b2, f2, w3):
    k = jnp.arange(L, dtype=jnp.float32)
    t = k / (L - 1)
    bands = jnp.linspace(1e-4, HY_BANDS - 1, HY_BANDS, dtype=jnp.float32)
    ang = (2.0 * math.pi * k / L)[:, None] * bands[None, :]
    feats = jnp.concatenate([t[:, None], jnp.cos(ang), -jnp.sin(ang)], -1)
    h = jnp.sin(f1 * (feats @ w1 + b1))
    h = jnp.sin(f2 * (h @ w2 + b2))
    h = (h @ w3).reshape(L, 2, HY_ORDER, HY_W)
    deltas = jnp.abs(jnp.linspace(math.log(HY_TARGET) / HY_FAST, math.log(HY_TARGET) / HY_SLOW, HY_W, dtype=jnp.float32))
    window = jnp.exp(-t[:, None] * deltas[None, :])
    h = h * window[:, None, None, :]
    fwd = h[:, 0]
    bwd = h[1:, 1][::-1]
    full = jnp.concatenate([fwd, jnp.zeros((1, HY_ORDER, HY_W), jnp.float32), bwd], 0)
    return full / jnp.sum(jnp.abs(full), 0, keepdims=True)


def hyena_mix(u, filt, bias):
    L = u.shape[1]
    v, x1, x2 = jnp.split(u, 3, -1)
    hf = jnp.fft.rfft(filt, axis=0)
    z = v
    for o, gate in enumerate((x1, x2)):
        zf = jnp.fft.rfft(z, n=2 * L, axis=1)
        conv = jnp.fft.irfft(zf * hf[None, :, o], n=2 * L, axis=1)[:, :L]
        z = gate * (conv + z * bias[o])
    return z


def token_shift_centred(p, mu):
    pp = jnp.pad(p, ((0, 0), (1, 1), (0, 0)))
    nb = 0.5 * (pp[:, :-2] + pp[:, 2:])
    return p + (nb - p) * mu


def rwkv7_scan(r, decay, k, v, a, b, s0, reverse):
    def step(S, inp):
        r_t, d_t, k_t, v_t, a_t, b_t = inp
        sa = jnp.einsum('bhvk,bhk->bhv', S, a_t)
        S = S * d_t[:, :, None, :] + sa[..., None] * b_t[:, :, None, :] + v_t[..., None] * k_t[:, :, None, :]
        return S, jnp.einsum('bhvk,bhk->bhv', S, r_t)
    xs = tuple(jnp.swapaxes(t, 0, 1) for t in (r, decay, k, v, a, b))
    s_fin, ys = lax.scan(step, s0, xs, reverse=reverse)
    return jnp.swapaxes(ys, 0, 1), s_fin


def rwkv7_mix(p, s0, mu, w0, w2, a0, a2, g2, k_k, k_a, r_k, ln_w, ln_b):
    B, L, _ = p.shape
    p = token_shift_centred(p, mu)
    r, k, v, wd, ad, gd = _split(p, (RW_W, RW_W, RW_W, 2 * RW_WLORA, 2 * RW_ALORA, RW_GLORA))
    wd = wd.reshape(B, L, 2, RW_WLORA)
    ad = ad.reshape(B, L, 2, RW_ALORA)
    w = -jax.nn.softplus(-(w0 + jnp.einsum('bldr,drc->bldc', jnp.tanh(wd), w2))) - 0.5
    decay = jnp.exp(-jnp.exp(w))
    a = jax.nn.sigmoid(a0 + jnp.einsum('bldr,drc->bldc', ad, a2))
    g = jax.nn.sigmoid(gd) @ g2
    heads = lambda t: t.reshape(t.shape[:-1] + (RW_H, RW_N))
    kk = heads(k * k_k)
    kk = kk / jnp.maximum(jnp.linalg.norm(kk, axis=-1, keepdims=True), 1e-12)
    k_dir = k[:, :, None, :] * (1.0 + (a - 1.0) * k_a)
    rh, vh = heads(r), heads(v)
    ys, bonus, states = [], [], []
    for d in range(2):
        a_d = heads(a[:, :, d])
        k_d = heads(k_dir[:, :, d])
        y_d, s_d = rwkv7_scan(rh, heads(decay[:, :, d]), k_d, vh, -kk, kk * a_d, s0[:, d], d == 1)
        ys.append(y_d)
        bonus.append(jnp.sum(rh * k_d * r_k, -1, keepdims=True) * vh)
        states.append(s_d)
    y = ys[0] + ys[1]
    mean = jnp.mean(y, -1, keepdims=True)
    var = jnp.mean(jnp.square(y - mean), -1, keepdims=True)
    y = ((y - mean) * lax.rsqrt(var + RW_LN_EPS)).reshape(B, L, RW_W) * ln_w + ln_b
    y = (y + (bonus[0] + bonus[1]).reshape(B, L, RW_W)) * g
    return y, jnp.stack(states, 1)


def even_mixer(h, s0, P, j):
    f32 = jnp.float32
    B, L, _ = h.shape
    p = (h @ P['e_w_in'][j]).astype(f32)
    p_hy, p_rw = p[..., :3 * HY_W], p[..., 3 * HY_W:]
    u = dwconv_centred(p_hy, P['e_hy_conv_w'][j].astype(f32)) + P['e_hy_conv_b'][j].astype(f32)
    filt = hyena_filters(L, P['e_hf_w1'][j].astype(f32), P['e_hf_b1'][j].astype(f32), P['e_hf_freq1'][j].astype(f32),
                         P['e_hf_w2'][j].astype(f32), P['e_hf_b2'][j].astype(f32), P['e_hf_freq2'][j].astype(f32),
                         P['e_hf_w3'][j].astype(f32))
    y_hy = hyena_mix(u, filt, P['e_hy_bias'][j].astype(f32))
    y_rw, s_new = rwkv7_mix(p_rw, s0.astype(f32), P['e_rw_mu'][j].astype(f32), P['e_rw_w0'][j].astype(f32),
                            P['e_rw_w2'][j].astype(f32), P['e_rw_a0'][j].astype(f32), P['e_rw_a2'][j].astype(f32),
                            P['e_rw_g2'][j].astype(f32), P['e_rw_kk'][j].astype(f32), P['e_rw_ka'][j].astype(f32),
                            P['e_rw_rk'][j].astype(f32), P['e_rw_ln_w'][j].astype(f32), P['e_rw_ln_b'][j].astype(f32))
    y = jnp.concatenate([y_hy, y_rw], -1).astype(h.dtype) @ P['e_w_out'][j]
    return y, s_new


def gdn_chunked(q, k, v, g, beta, s0):
    B, L, H, DK = q.shape
    DV = v.shape[-1]
    C = GD_CHUNK
    NC = L // C
    to_chunks = lambda t: jnp.swapaxes(t.reshape((B, NC, C, H) + t.shape[3:]), 2, 3)
    q = to_chunks(q * DK ** -0.5)
    k = to_chunks(k)
    v = to_chunks(v)
    g = jnp.cumsum(to_chunks(g), axis=-1)
    beta = to_chunks(beta)
    idx = jnp.arange(C)
    causal = idx[:, None] >= idx[None, :]
    strict = idx[:, None] > idx[None, :]
    gamma = jnp.exp(jnp.where(causal, g[..., :, None] - g[..., None, :], -jnp.inf))
    kb = k * beta[..., None]
    A = jnp.where(strict, jnp.einsum('bnhid,bnhjd->bnhij', kb, k) * gamma, 0.0)
    eye = jnp.eye(C, dtype=jnp.float32)
    T = lax.linalg.triangular_solve(A + eye, jnp.broadcast_to(eye, A.shape), left_side=True, lower=True, unit_diagonal=True)
    u = T @ (v * beta[..., None])
    w = T @ (kb * jnp.exp(g)[..., None])
    qk = jnp.where(causal, jnp.einsum('bnhid,bnhjd->bnhij', q, k) * gamma, 0.0)
    q_dec = q * jnp.exp(g)[..., None]
    k_dec = k * jnp.exp(g[..., -1:] - g)[..., None]
    g_tot = jnp.exp(g[..., -1])

    def step(S, inp):
        u_n, w_n, qk_n, qd_n, kd_n, gt_n = inp
        v_new = u_n - jnp.einsum('bhck,bhkv->bhcv', w_n, S)
        o = jnp.einsum('bhck,bhkv->bhcv', qd_n, S) + jnp.einsum('bhij,bhjv->bhiv', qk_n, v_new)
        S = S * gt_n[..., None, None] + jnp.einsum('bhck,bhcv->bhkv', kd_n, v_new)
        return S, o

    xs = tuple(jnp.swapaxes(t, 0, 1) for t in (u, w, qk, q_dec, k_dec, g_tot))
    s_fin, o = lax.scan(step, s0, xs)
    o = jnp.transpose(o, (1, 0, 3, 2, 4)).reshape(B, L, H, DV)
    return o, s_fin


def gdn_mix(p, s0, conv_w, A_log, dt_bias, norm_g):
    B, L, _ = p.shape
    qkv, z, a, b = _split(p, (2 * GD_QK + GD_W, GD_W, 2 * GD_H, 2 * GD_H))
    qkv = jax.nn.silu(dwconv_centred(qkv, conv_w))
    q, k, v = _split(qkv, (GD_QK, GD_QK, GD_W))
    l2n = lambda t: t * lax.rsqrt(jnp.sum(t * t, -1, keepdims=True) + 1e-6)
    q = l2n(q.reshape(B, L, GD_H, GD_DK))
    k = l2n(k.reshape(B, L, GD_H, GD_DK))
    v = v.reshape(B, L, GD_H, GD_DV)
    g = -jnp.exp(A_log) * jax.nn.softplus(a.reshape(B, L, 2, GD_H) + dt_bias)
    beta = jax.nn.sigmoid(b.reshape(B, L, 2, GD_H))
    o_f, s_f = gdn_chunked(q, k, v, g[:, :, 0], beta[:, :, 0], s0[:, 0])
    flip = lambda t: t[:, ::-1]
    o_b, s_b = gdn_chunked(flip(q), flip(k), flip(v), flip(g[:, :, 1]), flip(beta[:, :, 1]), s0[:, 1])
    o = o_f + flip(o_b)
    o = o * lax.rsqrt(jnp.mean(o * o, -1, keepdims=True) + NORM_EPS) * norm_g
    o = o * jax.nn.silu(z.reshape(B, L, GD_H, GD_DV))
    return o.reshape(B, L, GD_W), jnp.stack([s_f, s_b], 1)


def odd_mixer(h, s0, P, j):
    f32 = jnp.float32
    p = (h @ P['o_w_in'][j]).astype(f32)
    y, s_new = gdn_mix(p, s0.astype(f32), P['o_conv_w'][j].astype(f32), P['o_A_log'][j].astype(f32),
                       P['o_dt_bias'][j].astype(f32), P['o_norm_g'][j].astype(f32))
    return y.astype(h.dtype) @ P['o_w_out'][j], s_new


def dense_swiglu(h, w_in, w_out):
    gate, up = jnp.split(h @ w_in, 2, -1)
    return (jax.nn.silu(gate) * up) @ w_out


def moe_swiglu(h, router, w_in, w_out):
    logits = (h @ router).astype(jnp.float32)
    top_val, top_idx = lax.top_k(logits, TOP_K)
    gates = jax.nn.softmax(top_val, -1)
    combine = jnp.einsum('blk,blke->ble', gates, jax.nn.one_hot(top_idx, N_EXP, dtype=jnp.float32)).astype(h.dtype)
    out = dense_swiglu(h, w_in[0], w_out[0]) * combine[..., 0:1]
    for e in range(1, N_EXP):
        out = out + dense_swiglu(h, w_in[e], w_out[e]) * combine[..., e:e + 1]
    return out


def trunk(x, cond, rw_s0, gd_s0, P):
    rw_states, gd_states = [], []
    for i in range(DEPTH):
        j = i // 2
        mod = (jax.nn.silu(cond) @ P['ada_w'][i] + P['ada_b'][i])[:, None, :]
        sh1, sc1, g1, sh2, sc2, g2 = jnp.split(mod, N_MOD, -1)
        h = rmsnorm(x, P['norm_mix_g'][i]) * (1 + sc1) + sh1
        if i % 2 == 0:
            out, s = even_mixer(h, rw_s0[:, j], P, j)
            rw_states.append(s)
        else:
            out, s = odd_mixer(h, gd_s0[:, j], P, j)
            gd_states.append(s)
        x = x + g1 * out
        h = rmsnorm(x, P['norm_ffn_g'][i]) * (1 + sc2) + sh2
        if i % 2 == 0:
            ff = dense_swiglu(h, P['e_ffn_w_in'][j], P['e_ffn_w_out'][j])
        else:
            ff = moe_swiglu(h, P['o_router'][j], P['o_moe_w_in'][j], P['o_moe_w_out'][j])
        x = x + g2 * ff
    return rmsnorm(x, P['final_norm_g']), jnp.stack(rw_states, 1), jnp.stack(gd_states, 1)


def setup_inputs(seed: int = 0) -> dict:
    key = jax.random.key(seed)
    ks = iter(jax.random.split(key, 64))
    nrm = lambda shape, s: jax.random.normal(next(ks), shape, jnp.float32) * s
    uni = lambda shape, lo, hi: jax.random.uniform(next(ks), shape, jnp.float32, lo, hi)
    D = D_MODEL
    NE = N_EVEN
    NO = N_ODD
    inv = lambda n: n ** -0.5
    dt = jnp.exp(uni((NO, 2, GD_H), math.log(1e-3), math.log(1e-1)))
    return {
        'x_prompt': nrm((BATCH, SEQ, D), 1.0),
        'x_sample': nrm((DEC_BATCH, DEC_SEQ, D), 1.0),
        'state_rwkv': nrm((DEC_BATCH, NE, 2, RW_H, RW_N, RW_N), 0.1),
        'state_gdn': nrm((DEC_BATCH, NO, 2, GD_H, GD_DK, GD_DV), 0.1),
        'c': nrm((DEC_BATCH, D), 1.0),
        'c_ctx': nrm((D,), 1.0),
        'ada_w': nrm((DEPTH, D, N_MOD * D), 0.5 * inv(D)),
        'ada_b': nrm((DEPTH, N_MOD * D), 0.02),
        'norm_mix_g': 1.0 + nrm((DEPTH, D), 0.05),
        'norm_ffn_g': 1.0 + nrm((DEPTH, D), 0.05),
        'final_norm_g': 1.0 + nrm((D,), 0.05),
        'e_w_in': nrm((NE, D, P_EVEN), inv(D)),
        'e_hy_conv_w': nrm((NE, HY_SHORT, 3 * HY_W), 0.5),
        'e_hy_conv_b': nrm((NE, 3 * HY_W), 0.02),
        'e_hf_w1': nrm((NE, HY_EMB, HY_FFN), inv(HY_EMB)),
        'e_hf_b1': nrm((NE, HY_FFN), 0.1),
        'e_hf_freq1': 1.0 + nrm((NE, HY_FFN), 0.05),
        'e_hf_w2': nrm((NE, HY_FFN, HY_FFN), inv(HY_FFN)),
        'e_hf_b2': nrm((NE, HY_FFN), 0.1),
        'e_hf_freq2': 1.0 + nrm((NE, HY_FFN), 0.05),
        'e_hf_w3': nrm((NE, HY_FFN, 2 * HY_ORDER * HY_W), inv(HY_FFN)),
        'e_hy_bias': nrm((NE, HY_ORDER, HY_W), 0.5),
        'e_rw_mu': uni((NE, RW_PROJ), 0.0, 1.0),
        'e_rw_w0': uni((NE, 2, RW_W), -6.0, -1.0),
        'e_rw_w2': nrm((NE, 2, RW_WLORA, RW_W), 0.5 * inv(RW_WLORA)),
        'e_rw_a0': nrm((NE, 2, RW_W), 0.1),
        'e_rw_a2': nrm((NE, 2, RW_ALORA, RW_W), 0.5 * inv(RW_ALORA)),
        'e_rw_g2': nrm((NE, RW_GLORA, RW_W), inv(RW_GLORA)),
        'e_rw_kk': 0.85 + nrm((NE, RW_W), 0.05),
        'e_rw_ka': 1.0 + nrm((NE, RW_W), 0.05),
        'e_rw_rk': nrm((NE, RW_H, RW_N), 0.1),
        'e_rw_ln_w': 1.0 + nrm((NE, RW_W), 0.05),
        'e_rw_ln_b': nrm((NE, RW_W), 0.02),
        'e_w_out': nrm((NE, HY_W + RW_W, D), inv(HY_W + RW_W)),
        'e_ffn_w_in': nrm((NE, D, 2 * D_FF), inv(D)),
        'e_ffn_w_out': nrm((NE, D_FF, D), inv(D_FF)),
        'o_w_in': jnp.concatenate([nrm((NO, D, 2 * GD_QK + 2 * GD_W), inv(D)),
                                   nrm((NO, D, 2 * GD_H), 0.1 * inv(D)),
                                   nrm((NO, D, 2 * GD_H), inv(D))], -1),
        'o_conv_w': nrm((NO, GD_CONV, 2 * GD_QK + GD_W), 0.5),
        'o_A_log': jnp.log(uni((NO, 2, GD_H), 1.0, 16.0)),
        'o_dt_bias': dt + jnp.log(-jnp.expm1(-dt)),
        'o_norm_g': 1.0 + nrm((NO, GD_DV), 0.05),
        'o_w_out': nrm((NO, GD_W, D), inv(GD_W)),
        'o_router': nrm((NO, D, N_EXP), inv(D)),
        'o_moe_w_in': nrm((NO, N_EXP, D, 2 * E_FF), inv(D)),
        'o_moe_w_out': nrm((NO, N_EXP, E_FF, D), inv(E_FF)),
    }


def reference(x_prompt, x_sample, state_rwkv, state_gdn, c, c_ctx,
              ada_w, ada_b, norm_mix_g, norm_ffn_g, final_norm_g,
              e_w_in, e_hy_conv_w, e_hy_conv_b, e_hf_w1, e_hf_b1, e_hf_freq1, e_hf_w2, e_hf_b2, e_hf_freq2,
              e_hf_w3, e_hy_bias, e_rw_mu, e_rw_w0, e_rw_w2, e_rw_a0, e_rw_a2, e_rw_g2, e_rw_kk, e_rw_ka,
              e_rw_rk, e_rw_ln_w, e_rw_ln_b, e_w_out, e_ffn_w_in, e_ffn_w_out,
              o_w_in, o_conv_w, o_A_log, o_dt_bias, o_norm_g, o_w_out, o_router, o_moe_w_in, o_moe_w_out):
    P = {
        'ada_w': ada_w, 'ada_b': ada_b, 'norm_mix_g': norm_mix_g, 'norm_ffn_g': norm_ffn_g,
        'final_norm_g': final_norm_g,
        'e_w_in': e_w_in, 'e_hy_conv_w': e_hy_conv_w, 'e_hy_conv_b': e_hy_conv_b,
        'e_hf_w1': e_hf_w1, 'e_hf_b1': e_hf_b1, 'e_hf_freq1': e_hf_freq1,
        'e_hf_w2': e_hf_w2, 'e_hf_b2': e_hf_b2, 'e_hf_freq2': e_hf_freq2,
        'e_hf_w3': e_hf_w3, 'e_hy_bias': e_hy_bias,
        'e_rw_mu': e_rw_mu, 'e_rw_w0': e_rw_w0, 'e_rw_w2': e_rw_w2, 'e_rw_a0': e_rw_a0, 'e_rw_a2': e_rw_a2,
        'e_rw_g2': e_rw_g2, 'e_rw_kk': e_rw_kk, 'e_rw_ka': e_rw_ka, 'e_rw_rk': e_rw_rk,
        'e_rw_ln_w': e_rw_ln_w, 'e_rw_ln_b': e_rw_ln_b, 'e_w_out': e_w_out,
        'e_ffn_w_in': e_ffn_w_in, 'e_ffn_w_out': e_ffn_w_out,
        'o_w_in': o_w_in, 'o_conv_w': o_conv_w, 'o_A_log': o_A_log, 'o_dt_bias': o_dt_bias,
        'o_norm_g': o_norm_g, 'o_w_out': o_w_out, 'o_router': o_router,
        'o_moe_w_in': o_moe_w_in, 'o_moe_w_out': o_moe_w_out,
    }
    B = x_prompt.shape[0]
    rw0 = jnp.zeros((B, N_EVEN, 2, RW_H, RW_N, RW_N), jnp.float32)
    gd0 = jnp.zeros((B, N_ODD, 2, GD_H, GD_DK, GD_DV), jnp.float32)
    y_prompt, new_state_rwkv, new_state_gdn = trunk(x_prompt, c_ctx[None, :], rw0, gd0, P)
    x_lat = x_sample + grid_pos_embed(x_sample.shape[1]).astype(x_sample.dtype)[None]
    y_sample, _, _ = trunk(x_lat, c, state_rwkv, state_gdn, P)
    return (y_prompt, y_sample, new_state_rwkv, new_state_gdn)
```

```python
import functools
import math

import numpy as np
import jax
import jax.numpy as jnp
from jax import lax
from jax.experimental import pallas as pl
from jax.experimental.pallas import tpu as pltpu

F32 = jnp.float32
BF16 = jnp.bfloat16

D = 1024
B_CTX, L_CTX = 32, 256
B_DEN, L_DEN = 8, 1024
T_CTX = B_CTX * L_CTX
T_DEN = B_DEN * L_DEN
T_ALL = T_CTX + T_DEN
GRID_W = 64
NORM_EPS = 1e-6

HY_W = 512
HY_EMB = 33
HY_BANDS = 16
HY_FFN = 64
HY_TARGET, HY_FAST, HY_SLOW = 1e-2, 0.3, 1.5

RW_W = 512
RW_N = 64
RW_H = 8
RW_LORA = 224
RW_LN_EPS = 64e-5
P_EVEN = 3 * HY_W + 3 * RW_W + RW_LORA
P_EVEN_PAD = 3328

GD_H = 8
GD_DK = 128
GD_QKV = 3072
P_ODD = 4128
P_ODD_PAD = 4224

D_FF = 2816
N_EXP = 8
E_FF = 3584

LANES = 128
TM = 1024
RW_CHUNK = 32
GD_CHUNK = 64
SUPER = 128
VMEM_LIMIT = 56 * 1024 * 1024

_NN = (((1,), (0,)), ((), ()))
_NT = (((1,), (1,)), ((), ()))
_TN = (((0,), (0,)), ((), ()))


def _cparams(*sem):
    return pltpu.CompilerParams(dimension_semantics=sem, vmem_limit_bytes=VMEM_LIMIT)


def _dg(a, b, dims=_NN):
    return lax.dot_general(a, b, dims, preferred_element_type=F32)


def _split2(x):
    hi = x.astype(BF16)
    lo = (x - hi.astype(F32)).astype(BF16)
    return hi, lo


def _split3(x):
    x0 = x.astype(BF16)
    r1 = x - x0.astype(F32)
    x1 = r1.astype(BF16)
    x2 = (r1 - x1.astype(F32)).astype(BF16)
    return x0, x1, x2


def _dot(a, b, passes=1, dims=_NN):
    if passes == 1:
        return _dg(a.astype(BF16), b.astype(BF16), dims)
    if passes == 3:
        ah, al = _split2(a)
        bh, bl = _split2(b)
        return _dg(ah, bh, dims) + (_dg(ah, bl, dims) + _dg(al, bh, dims))
    a0, a1, a2 = _split3(a)
    b0, b1, b2 = _split3(b)
    small = _dg(a0, b2, dims) + _dg(a1, b1, dims) + _dg(a2, b0, dims)
    mid = _dg(a0, b1, dims) + _dg(a1, b0, dims)
    return _dg(a0, b0, dims) + (mid + small)


def _dot_exact_l(m, x, dims=_NN):
    x0, x1, x2 = _split3(x)
    return _dg(m, x0, dims) + (_dg(m, x1, dims) + _dg(m, x2, dims))


def _dot_exact_r(x, m, dims=_NN):
    x0, x1, x2 = _split3(x)
    return _dg(x0, m, dims) + (_dg(x1, m, dims) + _dg(x2, m, dims))


def _sigmoid(x):
    return 1.0 / (1.0 + jnp.exp(-x))


def _silu(x):
    return x * _sigmoid(x)


def _softplus(x):
    return jnp.maximum(x, 0.0) + jnp.log(1.0 + jnp.exp(-jnp.abs(x)))


def _shift_rows(x):
    n = x.shape[0]
    row = lax.broadcasted_iota(jnp.int32, x.shape, 0)
    prev = jnp.where(row == 0, 0.0, pltpu.roll(x, 1, 0))
    nxt = jnp.where(row == n - 1, 0.0, pltpu.roll(x, n - 1, 0))
    return prev, nxt


def _conv3(x, w, b=None):
    prev, nxt = _shift_rows(x)
    y = prev * w[0:1] + x * w[1:2] + nxt * w[2:3]
    return y if b is None else y + b


def _chunk_masks(n, chunk, fwd):
    row = lax.broadcasted_iota(jnp.int32, (n, n), 0)
    col = lax.broadcasted_iota(jnp.int32, (n, n), 1)
    sh = int(math.log2(chunk))
    same = (row >> sh) == (col >> sh)
    before = (col < row) if fwd else (col > row)
    after = (col > row) if fwd else (col < row)
    strict = same & before
    incl = same & (before | (row == col))
    later = same & after
    return same, strict, incl, later


def _mask_bf16(m):
    return jnp.where(m, 1.0, 0.0).astype(BF16)


def _tri_inv(x, chunk, passes):
    n = x.shape[0]
    row = lax.broadcasted_iota(jnp.int32, (n, n), 0)
    col = lax.broadcasted_iota(jnp.int32, (n, n), 1)
    t = jnp.where(row == col, 1.0, 0.0) + jnp.where((row >> 1) == (col >> 1), x, 0.0)
    for lvl in range(1, int(math.log2(chunk))):
        join = ((row >> (lvl + 1)) == (col >> (lvl + 1))) & ((row >> lvl) != (col >> lvl))
        t = t + _dot(t, _dot(jnp.where(join, x, 0.0), t, passes), passes)
    return t


@functools.lru_cache(maxsize=None)
def _pos_table():
    t = np.arange(L_DEN)
    row = (t // GRID_W).astype(np.float32)
    col = (t % GRID_W).astype(np.float32)
    q = D // 4
    omega = np.exp(-math.log(10000.0) * np.arange(q, dtype=np.float32) / q).astype(np.float32)
    enc = lambda pos: np.concatenate([np.sin(pos[:, None] * omega), np.cos(pos[:, None] * omega)], -1)
    return np.concatenate([enc(row), enc(col)], -1).astype(np.float32)


@functools.lru_cache(maxsize=None)
def _dft_tables(L):
    f = np.arange(L, dtype=np.int64)
    m = (f[:, None] * f[None, :]) % (2 * L)
    ang = np.pi * m.astype(np.float64) / L
    return np.cos(ang).astype(np.float32), np.sin(ang).astype(np.float32)


@functools.lru_cache(maxsize=None)
def _hyena_static(L):
    k = np.arange(L, dtype=np.float32)
    t = k / np.float32(L - 1)
    bands = np.linspace(1e-4, HY_BANDS - 1, HY_BANDS, dtype=np.float32)
    ang = (np.float32(2.0 * math.pi) * k / np.float32(L))[:, None] * bands[None, :]
    feats = np.concatenate([t[:, None], np.cos(ang), -np.sin(ang)], -1).astype(np.float32)
    feats_p = np.zeros((L, LANES), np.float32)
    feats_p[:, :HY_EMB] = feats
    deltas = np.abs(np.linspace(math.log(HY_TARGET) / HY_FAST, math.log(HY_TARGET) / HY_SLOW, HY_W,
                                dtype=np.float32))
    window = np.exp(-t[:, None] * deltas[None, :]).astype(np.float32)
    return feats_p, window


def _block_ones(n, blk):
    i = np.arange(n) // blk
    return (i[:, None] == i[None, :]).astype(np.float32)


def _mod_kernel(c_ref, w_ref, b_ref, o_ref):
    o_ref[...] = _dot(_silu(c_ref[...]), w_ref[...], 6) + b_ref[...]


def modulation(cond16, ada_w, ada_b):
    depth = ada_w.shape[0]
    tn = 1024
    return pl.pallas_call(
        _mod_kernel,
        grid=(depth, 6 * D // tn),
        in_specs=[pl.BlockSpec((16, D), lambda i, j: (0, 0)),
                  pl.BlockSpec((None, D, tn), lambda i, j: (i, 0, j)),
                  pl.BlockSpec((None, 1, tn), lambda i, j: (i, 0, j))],
        out_specs=pl.BlockSpec((None, 16, tn), lambda i, j: (i, 0, j)),
        out_shape=jax.ShapeDtypeStruct((depth, 16, 6 * D), F32),
        compiler_params=_cparams("arbitrary", "arbitrary"),
    )(cond16, ada_w, ada_b.reshape(depth, 1, 6 * D))


def _assemble_kernel(xp_ref, xs_ref, pos_ref, o_ref):
    i = pl.program_id(0)

    @pl.when(i < T_CTX // 256)
    def _():
        o_ref[...] = xp_ref[...]

    @pl.when(i >= T_CTX // 256)
    def _():
        o_ref[...] = xs_ref[...] + pos_ref[...]


def assemble_tokens(x_prompt, x_sample):
    nc = T_CTX // 256
    pos = jnp.asarray(_pos_table())
    return pl.pallas_call(
        _assemble_kernel,
        grid=(T_ALL // 256,),
        in_specs=[pl.BlockSpec((256, D), lambda i: (jnp.minimum(i, nc - 1), 0)),
                  pl.BlockSpec((256, D), lambda i: (jnp.maximum(i - nc, 0), 0)),
                  pl.BlockSpec((256, D), lambda i: (jnp.maximum(i - nc, 0) % (L_DEN // 256), 0))],
        out_specs=pl.BlockSpec((256, D), lambda i: (i, 0)),
        out_shape=jax.ShapeDtypeStruct((T_ALL, D), F32),
        compiler_params=_cparams("arbitrary"),
    )(x_prompt.reshape(T_CTX, D), x_sample.reshape(T_DEN, D), pos)


def _norm_mod(x, g, sc, sh):
    y = x * lax.rsqrt(jnp.mean(x * x, -1, keepdims=True) + NORM_EPS)
    return (y * g) * (1.0 + sc) + sh


def _norm_mm_kernel(x_ref, g_ref, sc_ref, sh_ref, w_ref, o_ref, h_scr):
    @pl.when(pl.program_id(1) == 0)
    def _():
        h_scr[...] = _norm_mod(x_ref[...], g_ref[...], sc_ref[...], sh_ref[...]).astype(BF16)

    o_ref[...] = _dg(h_scr[...], w_ref[...]).astype(o_ref.dtype)


def norm_matmul(x, g, sc_t, sh_t, w_bf16, tn, out_dtype=F32):
    n = w_bf16.shape[1]
    return pl.pallas_call(
        _norm_mm_kernel,
        grid=(T_ALL // TM, n // tn),
        in_specs=[pl.BlockSpec((TM, D), lambda i, j: (i, 0)),
                  pl.BlockSpec((1, D), lambda i, j: (0, 0)),
                  pl.BlockSpec((None, 1, D), lambda i, j: (i, 0, 0)),
                  pl.BlockSpec((None, 1, D), lambda i, j: (i, 0, 0)),
                  pl.BlockSpec((D, tn), lambda i, j: (0, j))],
        out_specs=pl.BlockSpec((TM, tn), lambda i, j: (i, j)),
        out_shape=jax.ShapeDtypeStruct((T_ALL, n), out_dtype),
        scratch_shapes=[pltpu.VMEM((TM, D), BF16)],
        compiler_params=_cparams("arbitrary", "arbitrary"),
    )(x, g.reshape(1, D), sc_t, sh_t, w_bf16)


def _norm_swiglu_kernel(x_ref, g_ref, sc_ref, sh_ref, wg_ref, wu_ref, o_ref, h_scr):
    @pl.when(pl.program_id(1) == 0)
    def _():
        h_scr[...] = _norm_mod(x_ref[...], g_ref[...], sc_ref[...], sh_ref[...]).astype(BF16)

    h = h_scr[...]
    o_ref[...] = (_silu(_dg(h, wg_ref[...])) * _dg(h, wu_ref[...])).astype(o_ref.dtype)


def norm_swiglu(x, g, sc_t, sh_t, w_in_bf16, dff, tn):
    nj = dff // tn
    return pl.pallas_call(
        _norm_swiglu_kernel,
        grid=(T_ALL // TM, nj),
        in_specs=[pl.BlockSpec((TM, D), lambda i, j: (i, 0)),
                  pl.BlockSpec((1, D), lambda i, j: (0, 0)),
                  pl.BlockSpec((None, 1, D), lambda i, j: (i, 0, 0)),
                  pl.BlockSpec((None, 1, D), lambda i, j: (i, 0, 0)),
                  pl.BlockSpec((D, tn), lambda i, j: (0, j)),
                  pl.BlockSpec((D, tn), lambda i, j: (0, j + nj))],
        out_specs=pl.BlockSpec((TM, tn), lambda i, j: (i, j)),
        out_shape=jax.ShapeDtypeStruct((T_ALL, dff), BF16),
        scratch_shapes=[pltpu.VMEM((TM, D), BF16)],
        compiler_params=_cparams("arbitrary", "arbitrary"),
    )(x, g.reshape(1, D), sc_t, sh_t, w_in_bf16, w_in_bf16)


def _mm_res_kernel(y_ref, w_ref, x_ref, gate_ref, o_ref):
    o_ref[...] = x_ref[...] + gate_ref[...] * _dg(y_ref[...], w_ref[...])


def matmul_residual(y_bf16, w_bf16, x, gate_t, tn=256):
    k = y_bf16.shape[1]
    return pl.pallas_call(
        _mm_res_kernel,
        grid=(T_ALL // TM, D // tn),
        in_specs=[pl.BlockSpec((TM, k), lambda i, j: (i, 0)),
                  pl.BlockSpec((k, tn), lambda i, j: (0, j)),
                  pl.BlockSpec((TM, tn), lambda i, j: (i, j)),
                  pl.BlockSpec((None, 1, tn), lambda i, j: (i, 0, j))],
        out_specs=pl.BlockSpec((TM, tn), lambda i, j: (i, j)),
        out_shape=jax.ShapeDtypeStruct((T_ALL, D), F32),
        compiler_params=_cparams("arbitrary", "arbitrary"),
    )(y_bf16, w_bf16, x, gate_t)


def _hy_filter_kernel(feat_ref, w1_ref, b1_ref, f1_ref, w2_ref, b2_ref, f2_ref, w3f_ref, w3b_ref, win_ref,
                      c_ref, s_ref, hr_ref, hi_ref, hn_ref):
    L = feat_ref.shape[0]
    h = jnp.sin(f1_ref[...] * (_dot(feat_ref[...], w1_ref[...], 6) + b1_ref[...]))
    h = jnp.sin(f2_ref[...] * (_dot(h, w2_ref[...], 6) + b2_ref[...]))
    win = win_ref[...]
    fw = _dot(h, w3f_ref[...], 6) * win
    bw = _dot(h, w3b_ref[...], 6) * win
    row = lax.broadcasted_iota(jnp.int32, fw.shape, 0)
    bw = jnp.where(row == 0, 0.0, bw)
    nrm = jnp.sum(jnp.abs(fw), 0, keepdims=True) + jnp.sum(jnp.abs(bw), 0, keepdims=True)
    ev = (fw + bw) / nrm
    od = (bw - fw) / nrm
    alt = (1 - 2 * (row & 1)).astype(F32)
    hr_ref[...] = _dot(c_ref[...], ev, 6)
    hi_ref[...] = _dot(s_ref[...], od, 6)
    hn_ref[...] = jnp.broadcast_to(jnp.sum(ev * alt, 0, keepdims=True), (8, ev.shape[1]))
    del L


def hyena_filter_spectrum(L, w1, b1, f1, w2, b2, f2, w3):
    feats, window = _hyena_static(L)
    cos_t, sin_t = _dft_tables(L)
    tc = 128
    ncb = HY_W // tc
    w1p = jnp.zeros((LANES, HY_FFN), F32).at[:HY_EMB].set(w1)
    const = lambda shape: pl.BlockSpec(shape, lambda o, c: (0,) * len(shape))
    return pl.pallas_call(
        _hy_filter_kernel,
        grid=(2, ncb),
        in_specs=[const((L, LANES)), const((LANES, HY_FFN)), const((1, HY_FFN)), const((1, HY_FFN)),
                  const((HY_FFN, HY_FFN)), const((1, HY_FFN)), const((1, HY_FFN)),
                  pl.BlockSpec((HY_FFN, tc), lambda o, c: (0, o * ncb + c)),
                  pl.BlockSpec((HY_FFN, tc), lambda o, c: (0, 2 * ncb + o * ncb + c)),
                  pl.BlockSpec((L, tc), lambda o, c: (0, c)),
                  const((L, L)), const((L, L))],
        out_specs=[pl.BlockSpec((None, L, tc), lambda o, c: (o, 0, c)),
                   pl.BlockSpec((None, L, tc), lambda o, c: (o, 0, c)),
                   pl.BlockSpec((None, 8, tc), lambda o, c: (o, 0, c))],
        out_shape=[jax.ShapeDtypeStruct((2, L, HY_W), F32), jax.ShapeDtypeStruct((2, L, HY_W), F32),
                   jax.ShapeDtypeStruct((2, 8, HY_W), F32)],
        compiler_params=_cparams("arbitrary", "arbitrary"),
    )(jnp.asarray(feats), w1p, b1.reshape(1, -1), f1.reshape(1, -1), w2, b2.reshape(1, -1), f2.reshape(1, -1),
      w3, w3, jnp.asarray(window), jnp.asarray(cos_t), jnp.asarray(sin_t))


def _hy_mix_kernel(pv_ref, p1_ref, p2_ref, wv_ref, w1_ref, w2_ref, bv_ref, b1_ref, b2_ref,
                   hr_ref, hi_ref, hn_ref, bias_ref, ch_ref, cl_ref, sh_ref, sl_ref, o_ref, *, passes):
    L = pv_ref.shape[0]
    z = _conv3(pv_ref[...], wv_ref[...], bv_ref[...])
    gates = (_conv3(p1_ref[...], w1_ref[...], b1_ref[...]), _conv3(p2_ref[...], w2_ref[...], b2_ref[...]))
    row = lax.broadcasted_iota(jnp.int32, z.shape, 0)
    alt = (1 - 2 * (row & 1)).astype(F32)
    ch, cl, sh, sl = ch_ref[...], cl_ref[...], sh_ref[...], sl_ref[...]

    def tdot(th, tl, x):
        if passes == 1:
            return _dg(th, x.astype(BF16))
        xh, xl = _split2(x)
        return _dg(th, xh) + (_dg(th, xl) + _dg(tl, xh))

    inv_l = 1.0 / L
    for o in range(2):
        hr, hi, hn = hr_ref[o], hi_ref[o], hn_ref[o][0:1]
        zc = tdot(ch, cl, z)
        zs = tdot(sh, sl, z)
        zn = jnp.sum(z * alt, 0, keepdims=True)
        yr = zc * hr + zs * hi
        yi = zc * hi - zs * hr
        wr = jnp.where(row == 0, 0.5 * inv_l, inv_l)
        conv = tdot(ch, cl, yr * wr) - tdot(sh, sl, yi * inv_l) + alt * (zn * hn * (0.5 * inv_l))
        z = gates[o] * (conv + z * bias_ref[o:o + 1])
    o_ref[...] = z.astype(o_ref.dtype)


def hyena_mix(p, row0, nseq, L, conv_w, conv_b, spec, bias, passes):
    hr, hi, hn = spec
    cos_t, sin_t = _dft_tables(L)
    ch, cl = _np_split2(cos_t)
    sh, sl = _np_split2(sin_t)
    tc = 128
    ncb = HY_W // tc
    b0 = row0 // L
    pspec = lambda part: pl.BlockSpec((L, tc), lambda b, c: (b0 + b, part * ncb + c))
    wspec = lambda part: pl.BlockSpec((3, tc), lambda b, c: (0, part * ncb + c))
    bspec = lambda part: pl.BlockSpec((1, tc), lambda b, c: (0, part * ncb + c))
    hspec = lambda rows: pl.BlockSpec((2, rows, tc), lambda b, c: (0, 0, c))
    tab = pl.BlockSpec((L, L), lambda b, c: (0, 0))
    return pl.pallas_call(
        functools.partial(_hy_mix_kernel, passes=passes),
        grid=(nseq, ncb),
        in_specs=[pspec(0), pspec(1), pspec(2), wspec(0), wspec(1), wspec(2), bspec(0), bspec(1), bspec(2),
                  hspec(L), hspec(L), hspec(8), pl.BlockSpec((2, tc), lambda b, c: (0, c)), tab, tab, tab, tab],
        out_specs=pl.BlockSpec((L, tc), lambda b, c: (b, c)),
        out_shape=jax.ShapeDtypeStruct((nseq * L, HY_W), BF16),
        compiler_params=_cparams("arbitrary", "arbitrary"),
    )(p, p, p, conv_w, conv_w, conv_w, conv_b, conv_b, conv_b, hr, hi, hn, bias,
      jnp.asarray(ch), jnp.asarray(cl), jnp.asarray(sh), jnp.asarray(sl))


def _np_split2(x):
    hi = x.astype(jnp.bfloat16)
    lo = (x - hi.astype(np.float32)).astype(jnp.bfloat16)
    return hi, lo


def _rw_prep_kernel(pr_ref, pk_ref, pv_ref, pl_ref, mur_ref, muk_ref, muv_ref, mul_ref, w0_ref, a0_ref,
                    w2_ref, a2_ref, g2_ref, kkw_ref, kaw_ref, rkw_ref, ones_ref,
                    r_ref, v_ref, kk_ref, lw_ref, kd_ref, bd_ref, bon_ref, g_ref):
    def shift(p, mu):
        prev, nxt = _shift_rows(p)
        return p + (0.5 * (prev + nxt) - p) * mu

    r = shift(pr_ref[...], mur_ref[...])
    k = shift(pk_ref[...], muk_ref[...])
    v = shift(pv_ref[...], muv_ref[...])
    lo = shift(pl_ref[...], mul_ref[...])
    ones = ones_ref[...]
    g_ref[...] = _dot(_sigmoid(lo), g2_ref[...], 1)
    kkr = k * kkw_ref[...]
    kk = kkr / jnp.maximum(jnp.sqrt(_dot_exact_r(kkr * kkr, ones)), 1e-12)
    th = jnp.tanh(lo)
    bon = jnp.zeros_like(r)
    for d in range(2):
        w = -_softplus(-(w0_ref[d:d + 1] + _dot(th, w2_ref[d], 3))) - 0.5
        lw_ref[d] = -jnp.exp(w)
        a = _sigmoid(a0_ref[d:d + 1] + _dot(lo, a2_ref[d], 1))
        kd = k * (1.0 + (a - 1.0) * kaw_ref[...])
        kd_ref[d] = kd
        bd_ref[d] = kk * a
        bon = bon + _dot_exact_r(r * kd * rkw_ref[...], ones) * v
    r_ref[...] = r
    v_ref[...] = v
    kk_ref[...] = kk
    bon_ref[...] = bon


def rwkv_prep(p, row0, nseq, L, mu, w0, w2, a0, a2, g2, k_k, k_a, r_k):
    b0 = row0 // L
    cb = LANES
    ncb = RW_W // cb
    c0 = 3 * HY_W // cb
    wide = lambda part: pl.BlockSpec((L, cb), lambda b, c: (b0 + b, c0 + part * ncb + c))
    lora = pl.BlockSpec((L, 256), lambda b, c: (b0 + b, (3 * HY_W + 3 * RW_W) // 256))
    muw = lambda part: pl.BlockSpec((1, cb), lambda b, c: (0, part * ncb + c))
    vec = lambda rows: pl.BlockSpec((rows, cb), lambda b, c: (0, c))
    mu_p = jnp.zeros((1, 3 * RW_W + 256), F32).at[0, :3 * RW_W + RW_LORA].set(mu)
    w2f = jnp.zeros((2, 256, RW_W), F32).at[0, 0:32].set(w2[0]).at[1, 32:64].set(w2[1])
    a2f = jnp.zeros((2, 256, RW_W), F32).at[0, 64:96].set(a2[0]).at[1, 96:128].set(a2[1])
    g2f = jnp.zeros((256, RW_W), F32).at[128:224].set(g2)
    n = nseq * L
    one = jax.ShapeDtypeStruct((n, RW_W), F32)
    two = jax.ShapeDtypeStruct((2, n, RW_W), F32)
    ospec1 = pl.BlockSpec((L, cb), lambda b, c: (b, c))
    ospec2 = pl.BlockSpec((2, L, cb), lambda b, c: (0, b, c))
    return pl.pallas_call(
        _rw_prep_kernel,
        grid=(nseq, ncb),
        in_specs=[wide(0), wide(1), wide(2), lora, muw(0), muw(1), muw(2),
                  pl.BlockSpec((1, 256), lambda b, c: (0, 3 * RW_W // 256)),
                  vec(2), vec(2),
                  pl.BlockSpec((2, 256, cb), lambda b, c: (0, 0, c)),
                  pl.BlockSpec((2, 256, cb), lambda b, c: (0, 0, c)),
                  pl.BlockSpec((256, cb), lambda b, c: (0, c)),
                  vec(1), vec(1), vec(1),
                  pl.BlockSpec((cb, cb), lambda b, c: (0, 0))],
        out_specs=[ospec1, ospec1, ospec1, ospec2, ospec2, ospec2, ospec1, ospec1],
        out_shape=[one, one, one, two, two, two, one, one],
        compiler_params=_cparams("arbitrary", "arbitrary"),
    )(p, p, p, p, mu_p, mu_p, mu_p, mu_p, w0, a0, w2f, a2f, g2f, k_k.reshape(1, RW_W), k_a.reshape(1, RW_W),
      r_k.reshape(1, RW_W), jnp.asarray(_block_ones(cb, RW_N)).astype(BF16))


def _rw_block(r, lw, k, v, a, b, s, fwd, passes):
    n = r.shape[0]
    c = RW_CHUNK
    same, strict, incl, later = _chunk_masks(n, c, fwd)
    cum = _dot_exact_l(_mask_bf16(incl), lw)
    suf = _dot_exact_l(_mask_bf16(later), lw)
    tot = _dot_exact_l(_mask_bf16(same), lw)
    e_neg = jnp.exp(-cum)
    at = a * jnp.exp(cum - lw)
    rt = r * jnp.exp(cum)
    bt = b * e_neg
    kt = k * e_neg
    e_suf = jnp.exp(suf)
    bp = b * e_suf
    kp = k * e_suf
    lane = lax.broadcasted_iota(jnp.int32, (1, LANES), 1)
    ahat = jnp.zeros_like(r)
    w1 = jnp.zeros_like(r)
    rhat = rt
    y0 = jnp.zeros_like(r)
    for g in range(LANES // RW_N):
        mg = (lane >> 6) == g
        at_g = jnp.where(mg, at, 0.0)
        rt_g = jnp.where(mg, rt, 0.0)
        v_g = jnp.where(mg, v, 0.0)
        ab = jnp.where(strict, _dot(at_g, bt, passes, _NT), 0.0)
        ak = jnp.where(strict, _dot(at_g, kt, passes, _NT), 0.0)
        rb = jnp.where(incl, _dot(rt_g, bt, passes, _NT), 0.0)
        rk = jnp.where(incl, _dot(rt_g, kt, passes, _NT), 0.0)
        tinv = _tri_inv(ab, c, passes)
        ahat_g = _dot(tinv, at_g, passes)
        w1_g = _dot(tinv, _dot(ak, v_g, passes), passes)
        rhat = rhat + _dot(rb, ahat_g, passes)
        y0 = y0 + _dot(rb, w1_g, passes) + _dot(rk, v_g, passes)
        ahat = ahat + ahat_g
        w1 = w1 + w1_g
    rowl = lax.broadcasted_iota(jnp.int32, (LANES, LANES), 0)
    coll = lax.broadcasted_iota(jnp.int32, (LANES, LANES), 1)
    diag_blocks = (rowl >> 6) == (coll >> 6)
    ys = [None] * (n // c)
    order = range(n // c) if fwd else range(n // c - 1, -1, -1)
    for ci in order:
        sl = slice(ci * c, (ci + 1) * c)
        x = _dot(jnp.concatenate([ahat[sl], rhat[sl]], 0), s, passes, _NT)
        u = w1[sl] + x[:c]
        ys[ci] = y0[sl] + x[c:]
        upd = _dot(jnp.concatenate([u, v[sl]], 0), jnp.concatenate([bp[sl], kp[sl]], 0), passes, _TN)
        s = s * jnp.exp(tot[ci * c:ci * c + 1]) + jnp.where(diag_blocks, upd, 0.0)
    return jnp.concatenate(ys, 0), s


def _rw_scan_kernel(r_ref, v_ref, kk_ref, lw_ref, kd_ref, bd_ref, s0_ref, y_ref, s_ref, *, passes):
    L = r_ref.shape[0]
    nblk = L // SUPER

    def body(i, carry):
        out = []
        for d in range(2):
            j = i if d == 0 else nblk - 1 - i
            rows = pl.ds(pl.multiple_of(j * SUPER, SUPER), SUPER)
            kk = kk_ref[rows, :]
            y, s = _rw_block(r_ref[rows, :], lw_ref[d, rows, :], kd_ref[d, rows, :], v_ref[rows, :],
                             -kk, bd_ref[d, rows, :], carry[d], d == 0, passes)
            y_ref[d, rows, :] = y
            out.append(s)
        return tuple(out)

    s_fin = lax.fori_loop(0, nblk, body, (s0_ref[0], s0_ref[1]))
    s_ref[0] = s_fin[0]
    s_ref[1] = s_fin[1]


def rwkv_scan(r, v, kk, lw, kd, bd, s0_bd, nseq, L, passes):
    ngrp = RW_W // LANES
    one = pl.BlockSpec((L, LANES), lambda b, g: (b, g))
    two = pl.BlockSpec((2, L, LANES), lambda b, g: (0, b, g))
    st = pl.BlockSpec((None, 2, None, LANES, LANES), lambda b, g: (b, 0, g, 0, 0))
    return pl.pallas_call(
        functools.partial(_rw_scan_kernel, passes=passes),
        grid=(nseq, ngrp),
        in_specs=[one, one, one, two, two, two, st],
        out_specs=[two, st],
        out_shape=[jax.ShapeDtypeStruct((2, nseq * L, RW_W), F32),
                   jax.ShapeDtypeStruct((nseq, 2, ngrp, LANES, LANES), F32)],
        compiler_params=_cparams("arbitrary", "arbitrary"),
    )(r, v, kk, lw, kd, bd, s0_bd)


def _rw_post_kernel(y_ref, bon_ref, g_ref, lnw_ref, lnb_ref, ones_ref, o_ref):
    y = y_ref[0] + y_ref[1]
    ones = ones_ref[...]
    mean = _dot_exact_r(y, ones) * (1.0 / RW_N)
    yc = y - mean
    var = _dot_exact_r(yc * yc, ones) * (1.0 / RW_N)
    yn = yc * lax.rsqrt(var + RW_LN_EPS) * lnw_ref[...] + lnb_ref[...]
    o_ref[...] = ((yn + bon_ref[...]) * g_ref[...]).astype(o_ref.dtype)


def rwkv_post(y2, bonus, g, ln_w, ln_b):
    n = bonus.shape[0]
    tm = 512
    row = pl.BlockSpec((tm, RW_W), lambda i: (i, 0))
    vec = pl.BlockSpec((1, RW_W), lambda i: (0, 0))
    return pl.pallas_call(
        _rw_post_kernel,
        grid=(n // tm,),
        in_specs=[pl.BlockSpec((2, tm, RW_W), lambda i: (0, i, 0)), row, row, vec, vec,
                  pl.BlockSpec((RW_W, RW_W), lambda i: (0, 0))],
        out_specs=row,
        out_shape=jax.ShapeDtypeStruct((n, RW_W), BF16),
        compiler_params=_cparams("arbitrary"),
    )(y2, bonus, g, ln_w.reshape(1, RW_W), ln_b.reshape(1, RW_W),
      jnp.asarray(_block_ones(RW_W, RW_N)).astype(BF16))


def _rw_states_to_blockdiag(s):
    b = s.shape[0]
    s = s.reshape(b, 2, 4, 2, RW_N, RW_N)
    z = jnp.zeros_like(s[:, :, :, 0])
    top = jnp.concatenate([s[:, :, :, 0], z], -1)
    bot = jnp.concatenate([z, s[:, :, :, 1]], -1)
    return jnp.concatenate([top, bot], -2)


def _rw_states_from_blockdiag(s):
    b = s.shape[0]
    return jnp.stack([s[:, :, :, :RW_N, :RW_N], s[:, :, :, RW_N:, RW_N:]], 3).reshape(b, 2, RW_H, RW_N, RW_N)


def _gd_block(q, k, v, lw, beta, s, fwd, passes):
    n = q.shape[0]
    c = GD_CHUNK
    same, strict, incl, later = _chunk_masks(n, c, fwd)
    cum = _dot_exact_l(_mask_bf16(incl), lw)
    suf = _dot_exact_l(_mask_bf16(later), lw)
    tot = _dot_exact_l(_mask_bf16(same), lw)
    lane = lax.broadcasted_iota(jnp.int32, (n, LANES), 1)
    pick0 = _mask_bf16(lane == 0)
    cum_row = _dot_exact_l(pick0, cum, _NT)
    gam = jnp.exp(jnp.where(incl, cum[:, 0:1] - cum_row, -jnp.inf))
    kb = k * beta
    a = jnp.where(strict, _dot(kb, k, passes, _NT) * gam, 0.0)
    qk = jnp.where(incl, _dot(q, k, passes, _NT) * gam, 0.0)
    tinv = _tri_inv(-a, c, passes)
    e_cum = jnp.exp(cum)
    u = _dot(tinv, v * beta, passes)
    w = _dot(tinv, kb * e_cum, passes)
    qhat = q * e_cum - _dot(qk, w, passes)
    o0 = _dot(qk, u, passes)
    kd = k * jnp.exp(suf)
    os_ = [None] * (n // c)
    order = range(n // c) if fwd else range(n // c - 1, -1, -1)
    for ci in order:
        sl = slice(ci * c, (ci + 1) * c)
        x = _dot(jnp.concatenate([w[sl], qhat[sl]], 0), s, passes)
        vn = u[sl] - x[:c]
        os_[ci] = o0[sl] + x[c:]
        s = s * jnp.exp(tot[ci * c:ci * c + 1]) + _dot(kd[sl], vn, passes, _TN)
    return jnp.concatenate(os_, 0), s


def _gd_kernel(pq_ref, pk_ref, pv_ref, pz_ref, pab_ref, wq_ref, wk_ref, wv_ref, alog_ref, dtb_ref, ng_ref,
               s0_ref, o_ref, s_ref, q_scr, k_scr, v_scr, gb_scr, o_scr, *, passes):
    L = pq_ref.shape[0]
    h = pl.program_id(1)
    l2n = lambda t: t * lax.rsqrt(jnp.sum(t * t, -1, keepdims=True) + 1e-6)
    q_scr[...] = l2n(_silu(_conv3(pq_ref[...], wq_ref[...]))) * (GD_DK ** -0.5)
    k_scr[...] = l2n(_silu(_conv3(pk_ref[...], wk_ref[...])))
    v_scr[...] = _silu(_conv3(pv_ref[...], wv_ref[...]))
    pab = pab_ref[...]
    lane = lax.broadcasted_iota(jnp.int32, pab.shape, 1)
    g_all = -jnp.exp(alog_ref[...]) * _softplus(pab + dtb_ref[...])
    b_all = _sigmoid(pab)
    for d in range(2):
        g = jnp.sum(jnp.where(lane == d * GD_H + h, g_all, 0.0), -1, keepdims=True)
        bt = jnp.sum(jnp.where(lane == 2 * GD_H + d * GD_H + h, b_all, 0.0), -1, keepdims=True)
        gb_scr[d] = jnp.broadcast_to(g, (L, LANES))
        gb_scr[2 + d] = jnp.broadcast_to(bt, (L, LANES))
    nblk = L // SUPER

    def body(i, carry):
        out = []
        for d in range(2):
            j = i if d == 0 else nblk - 1 - i
            rows = pl.ds(pl.multiple_of(j * SUPER, SUPER), SUPER)
            o, s = _gd_block(q_scr[rows, :], k_scr[rows, :], v_scr[rows, :], gb_scr[d, rows, :],
                             gb_scr[2 + d, rows, :], carry[d], d == 0, passes)
            o_scr[d, rows, :] = o
            out.append(s)
        return tuple(out)

    s_fin = lax.fori_loop(0, nblk, body, (s0_ref[0], s0_ref[1]))
    s_ref[0] = s_fin[0]
    s_ref[1] = s_fin[1]
    o = o_scr[0] + o_scr[1]
    o = o * lax.rsqrt(jnp.mean(o * o, -1, keepdims=True) + NORM_EPS) * ng_ref[...]
    o_ref[...] = (o * _silu(pz_ref[...])).astype(o_ref.dtype)


def gdn_mix(p, row0, nseq, L, conv_w, a_log, dt_bias, norm_g, s0, passes):
    b0 = row0 // L
    col = lambda part: pl.BlockSpec((L, LANES), lambda b, h: (b0 + b, part * GD_H + h))
    wcol = lambda part: pl.BlockSpec((3, LANES), lambda b, h: (0, part * GD_H + h))
    vec = pl.BlockSpec((1, LANES), lambda b, h: (0, 0))
    st = pl.BlockSpec((None, 2, None, LANES, LANES), lambda b, h: (b, 0, h, 0, 0))
    alog_row = jnp.zeros((1, LANES), F32).at[0, :2 * GD_H].set(a_log.reshape(-1))
    dtb_row = jnp.zeros((1, LANES), F32).at[0, :2 * GD_H].set(dt_bias.reshape(-1))
    return pl.pallas_call(
        functools.partial(_gd_kernel, passes=passes),
        grid=(nseq, GD_H),
        in_specs=[col(0), col(1), col(2), col(3),
                  pl.BlockSpec((L, LANES), lambda b, h: (b0 + b, 4 * GD_H)),
                  wcol(0), wcol(1), wcol(2), vec, vec, vec, st],
        out_specs=[pl.BlockSpec((L, LANES), lambda b, h: (b, h)), st],
        out_shape=[jax.ShapeDtypeStruct((nseq * L, GD_H * LANES), BF16),
                   jax.ShapeDtypeStruct((nseq, 2, GD_H, LANES, LANES), F32)],
        scratch_shapes=[pltpu.VMEM((L, LANES), F32), pltpu.VMEM((L, LANES), F32), pltpu.VMEM((L, LANES), F32),
                        pltpu.VMEM((4, L, LANES), F32), pltpu.VMEM((2, L, LANES), F32)],
        compiler_params=_cparams("arbitrary", "arbitrary"),
    )(p, p, p, p, p, conv_w, conv_w, conv_w, alog_row, dtb_row, norm_g.reshape(1, LANES), s0)


def _router_kernel(x_ref, g_ref, sc_ref, sh_ref, wr_ref, h_ref, comb_ref):
    h = _norm_mod(x_ref[...], g_ref[...], sc_ref[...], sh_ref[...])
    h_ref[...] = h.astype(BF16)
    logits = _dot(h, wr_ref[...], 6)
    lane = lax.broadcasted_iota(jnp.int32, logits.shape, 1)
    logits = jnp.where(lane < N_EXP, logits, -jnp.inf)
    m1 = jnp.max(logits, -1, keepdims=True)
    i1 = jnp.min(jnp.where(logits == m1, lane, LANES), -1, keepdims=True)
    rest = jnp.where(lane == i1, -jnp.inf, logits)
    m2 = jnp.max(rest, -1, keepdims=True)
    i2 = jnp.min(jnp.where(rest == m2, lane, LANES), -1, keepdims=True)
    e2 = jnp.exp(m2 - m1)
    g1 = 1.0 / (1.0 + e2)
    g2 = e2 / (1.0 + e2)
    comb_ref[...] = jnp.where(lane == i1, g1, 0.0) + jnp.where(lane == i2, g2, 0.0)


def moe_router(x, g, sc_t, sh_t, router):
    tm = 512
    per = TM // tm
    wr = jnp.zeros((D, LANES), F32).at[:, :N_EXP].set(router)
    return pl.pallas_call(
        _router_kernel,
        grid=(T_ALL // tm,),
        in_specs=[pl.BlockSpec((tm, D), lambda i: (i, 0)),
                  pl.BlockSpec((1, D), lambda i: (0, 0)),
                  pl.BlockSpec((None, 1, D), lambda i: (i // per, 0, 0)),
                  pl.BlockSpec((None, 1, D), lambda i: (i // per, 0, 0)),
                  pl.BlockSpec((D, LANES), lambda i: (0, 0))],
        out_specs=[pl.BlockSpec((tm, D), lambda i: (i, 0)), pl.BlockSpec((tm, LANES), lambda i: (i, 0))],
        out_shape=[jax.ShapeDtypeStruct((T_ALL, D), BF16), jax.ShapeDtypeStruct((T_ALL, LANES), F32)],
        compiler_params=_cparams("arbitrary"),
    )(x, g.reshape(1, D), sc_t, sh_t, wr)


def _moe_kernel(h_ref, comb_ref, wg_ref, wu_ref, wo_ref, x_ref, gate_ref, fg_ref, o_ref, acc):
    e = pl.program_id(1)
    f = pl.program_id(2)

    @pl.when((e == 0) & (f == 0))
    def _():
        acc[...] = jnp.zeros_like(acc)

    h = h_ref[...]
    comb = comb_ref[...]
    lane = lax.broadcasted_iota(jnp.int32, comb.shape, 1)
    ce = jnp.sum(jnp.where(lane == e, comb, 0.0), -1, keepdims=True)
    act = _silu(_dg(h, wg_ref[...])) * _dg(h, wu_ref[...])
    acc[...] += _dg((act * ce).astype(BF16), wo_ref[...])

    @pl.when((e == pl.num_programs(1) - 1) & (f == pl.num_programs(2) - 1))
    def _():
        xn = x_ref[...] + gate_ref[...] * acc[...]
        y = xn * lax.rsqrt(jnp.mean(xn * xn, -1, keepdims=True) + NORM_EPS)
        o_ref[...] = y * fg_ref[...]


def moe_ffn_final(h_bf16, comb, w_in_bf16, w_out_bf16, x, gate_t, final_g):
    tn = 512
    nf = E_FF // tn
    return pl.pallas_call(
        _moe_kernel,
        grid=(T_ALL // TM, N_EXP, nf),
        in_specs=[pl.BlockSpec((TM, D), lambda i, e, f: (i, 0)),
                  pl.BlockSpec((TM, LANES), lambda i, e, f: (i, 0)),
                  pl.BlockSpec((None, D, tn), lambda i, e, f: (e, 0, f)),
                  pl.BlockSpec((None, D, tn), lambda i, e, f: (e, 0, f + nf)),
                  pl.BlockSpec((None, tn, D), lambda i, e, f: (e, f, 0)),
                  pl.BlockSpec((TM, D), lambda i, e, f: (i, 0)),
                  pl.BlockSpec((None, 1, D), lambda i, e, f: (i, 0, 0)),
                  pl.BlockSpec((1, D), lambda i, e, f: (0, 0))],
        out_specs=pl.BlockSpec((TM, D), lambda i, e, f: (i, 0)),
        out_shape=jax.ShapeDtypeStruct((T_ALL, D), F32),
        scratch_shapes=[pltpu.VMEM((TM, D), F32)],
        compiler_params=_cparams("arbitrary", "arbitrary", "arbitrary"),
    )(h_bf16, comb, w_in_bf16, w_in_bf16, w_out_bf16, x, gate_t, final_g.reshape(1, D))


def _tile_rows(mod_l, k):
    idx = np.concatenate([np.zeros(T_CTX // TM, np.int32),
                          1 + np.arange(T_DEN // TM, dtype=np.int32) // (L_DEN // TM)])
    return mod_l[idx, k * D:(k + 1) * D][:, None, :]


def _pad_cols(w, n):
    return jnp.pad(w, ((0, 0), (0, n - w.shape[1])))


def kernel(x_prompt, x_sample, state_rwkv, state_gdn, c, c_ctx, ada_w, ada_b, norm_mix_g, norm_ffn_g, final_norm_g, e_w_in, e_hy_conv_w, e_hy_conv_b, e_hf_w1, e_hf_b1, e_hf_freq1, e_hf_w2, e_hf_b2, e_hf_freq2, e_hf_w3, e_hy_bias, e_rw_mu, e_rw_w0, e_rw_w2, e_rw_a0, e_rw_a2, e_rw_g2, e_rw_kk, e_rw_ka, e_rw_rk, e_rw_ln_w, e_rw_ln_b, e_w_out, e_ffn_w_in, e_ffn_w_out, o_w_in, o_conv_w, o_A_log, o_dt_bias, o_norm_g, o_w_out, o_router, o_moe_w_in, o_moe_w_out):
    passes = 1
    cond16 = jnp.zeros((16, D), F32).at[0].set(c_ctx).at[1:1 + B_DEN].set(c)
    mod = modulation(cond16, ada_w, ada_b)
    x = assemble_tokens(x_prompt, x_sample)

    m0 = [_tile_rows(mod[0], k) for k in range(6)]
    p = norm_matmul(x, norm_mix_g[0], m0[1], m0[0], _pad_cols(e_w_in[0], P_EVEN_PAD).astype(BF16), 256)
    ys_hy, ys_rw, st_rw = [], [], None
    for row0, nseq, L in ((0, B_CTX, L_CTX), (T_CTX, B_DEN, L_DEN)):
        spec = hyena_filter_spectrum(L, e_hf_w1[0], e_hf_b1[0], e_hf_freq1[0], e_hf_w2[0], e_hf_b2[0],
                                     e_hf_freq2[0], e_hf_w3[0])
        ys_hy.append(hyena_mix(p, row0, nseq, L, e_hy_conv_w[0], e_hy_conv_b[0].reshape(1, -1), spec,
                               e_hy_bias[0], 3))
        r, v, kk, lw, kd, bd, bonus, g = rwkv_prep(p, row0, nseq, L, e_rw_mu[0], e_rw_w0[0], e_rw_w2[0],
                                                   e_rw_a0[0], e_rw_a2[0], e_rw_g2[0], e_rw_kk[0], e_rw_ka[0],
                                                   e_rw_rk[0])
        if row0 == 0:
            s0 = jnp.zeros((nseq, 2, 4, LANES, LANES), F32)
        else:
            s0 = _rw_states_to_blockdiag(state_rwkv[:, 0])
        y2, s_new = rwkv_scan(r, v, kk, lw, kd, bd, s0, nseq, L, passes)
        if row0 == 0:
            st_rw = _rw_states_from_blockdiag(s_new)
        ys_rw.append(rwkv_post(y2, bonus, g, e_rw_ln_w[0], e_rw_ln_b[0]))
    y_mix = jnp.concatenate([jnp.concatenate(ys_hy, 0), jnp.concatenate(ys_rw, 0)], 1)
    x = matmul_residual(y_mix, e_w_out[0].astype(BF16), x, m0[2])
    act = norm_swiglu(x, norm_ffn_g[0], m0[4], m0[3], e_ffn_w_in[0].astype(BF16), D_FF, 256)
    x = matmul_residual(act, e_ffn_w_out[0].astype(BF16), x, m0[5])

    m1 = [_tile_rows(mod[1], k) for k in range(6)]
    p = norm_matmul(x, norm_mix_g[1], m1[1], m1[0], _pad_cols(o_w_in[0], P_ODD_PAD).astype(BF16), 384)
    os_, st_gd = [], None
    for row0, nseq, L in ((0, B_CTX, L_CTX), (T_CTX, B_DEN, L_DEN)):
        if row0 == 0:
            s0 = jnp.zeros((nseq, 2, GD_H, LANES, LANES), F32)
        else:
            s0 = state_gdn[:, 0]
        o, s_new = gdn_mix(p, row0, nseq, L, o_conv_w[0], o_A_log[0], o_dt_bias[0], o_norm_g[0], s0, passes)
        if row0 == 0:
            st_gd = s_new
        os_.append(o)
    x = matmul_residual(jnp.concatenate(os_, 0), o_w_out[0].astype(BF16), x, m1[2])
    h, comb = moe_router(x, norm_ffn_g[1], m1[4], m1[3], o_router[0])
    y = moe_ffn_final(h, comb, o_moe_w_in[0].astype(BF16), o_moe_w_out[0].astype(BF16), x, m1[5], final_norm_g)

    y_prompt = y[:T_CTX].reshape(B_CTX, L_CTX, D)
    y_sample = y[T_CTX:].reshape(B_DEN, L_DEN, D)
    return (y_prompt, y_sample, st_rw[:, None], st_gd[:, None])
```

```python
import functools
import math

import numpy as np
import jax
import jax.numpy as jnp
from jax import lax
from jax.experimental import pallas as pl
from jax.experimental.pallas import tpu as pltpu

F32 = jnp.float32
BF16 = jnp.bfloat16

D = 1024
B_CTX, L_CTX = 32, 256
B_DEN, L_DEN = 8, 1024
T_CTX = B_CTX * L_CTX
T_DEN = B_DEN * L_DEN
T_ALL = T_CTX + T_DEN
GRID_W = 64
NORM_EPS = 1e-6

HY_W = 512
HY_EMB = 33
HY_BANDS = 16
HY_FFN = 64
HY_TARGET, HY_FAST, HY_SLOW = 1e-2, 0.3, 1.5

RW_W = 512
RW_N = 64
RW_H = 8
RW_LORA = 224
RW_LN_EPS = 64e-5
P_EVEN = 3 * HY_W + 3 * RW_W + RW_LORA
P_EVEN_PAD = 3328

GD_H = 8
GD_DK = 128
GD_QKV = 3072
P_ODD = 4128
P_ODD_PAD = 4224

D_FF = 2816
N_EXP = 8
E_FF = 3584

LANES = 128
TM = 1024
RW_CHUNK = 32
GD_CHUNK = 64
SUPER = 128
VMEM_LIMIT = 56 * 1024 * 1024

_NN = (((1,), (0,)), ((), ()))
_NT = (((1,), (1,)), ((), ()))
_TN = (((0,), (0,)), ((), ()))


def _cparams(*sem):
    return pltpu.CompilerParams(dimension_semantics=sem, vmem_limit_bytes=VMEM_LIMIT)


def _dg(a, b, dims=_NN):
    return lax.dot_general(a, b, dims, preferred_element_type=F32)


def _split2(x):
    hi = x.astype(BF16)
    lo = (x - hi.astype(F32)).astype(BF16)
    return hi, lo


def _split3(x):
    x0 = x.astype(BF16)
    r1 = x - x0.astype(F32)
    x1 = r1.astype(BF16)
    x2 = (r1 - x1.astype(F32)).astype(BF16)
    return x0, x1, x2


def _dot(a, b, passes=1, dims=_NN):
    if passes == 1:
        return _dg(a.astype(BF16), b.astype(BF16), dims)
    if passes == 3:
        ah, al = _split2(a)
        bh, bl = _split2(b)
        return _dg(ah, bh, dims) + (_dg(ah, bl, dims) + _dg(al, bh, dims))
    a0, a1, a2 = _split3(a)
    b0, b1, b2 = _split3(b)
    small = _dg(a0, b2, dims) + _dg(a1, b1, dims) + _dg(a2, b0, dims)
    mid = _dg(a0, b1, dims) + _dg(a1, b0, dims)
    return _dg(a0, b0, dims) + (mid + small)


def _dot_exact_l(m, x, dims=_NN):
    x0, x1, x2 = _split3(x)
    return _dg(m, x0, dims) + (_dg(m, x1, dims) + _dg(m, x2, dims))


def _dot_exact_r(x, m, dims=_NN):
    x0, x1, x2 = _split3(x)
    return _dg(x0, m, dims) + (_dg(x1, m, dims) + _dg(x2, m, dims))


def _sigmoid(x):
    return 1.0 / (1.0 + jnp.exp(-x))


def _silu(x):
    return x * _sigmoid(x)


def _softplus(x):
    return jnp.maximum(x, 0.0) + jnp.log(1.0 + jnp.exp(-jnp.abs(x)))


def _shift_rows(x):
    n = x.shape[0]
    row = lax.broadcasted_iota(jnp.int32, x.shape, 0)
    prev = jnp.where(row == 0, 0.0, pltpu.roll(x, 1, 0))
    nxt = jnp.where(row == n - 1, 0.0, pltpu.roll(x, n - 1, 0))
    return prev, nxt


def _conv3(x, w, b=None):
    prev, nxt = _shift_rows(x)
    y = prev * w[0:1] + x * w[1:2] + nxt * w[2:3]
    return y if b is None else y + b


def _chunk_masks(n, chunk, fwd):
    row = lax.broadcasted_iota(jnp.int32, (n, n), 0)
    col = lax.broadcasted_iota(jnp.int32, (n, n), 1)
    sh = int(math.log2(chunk))
    same = (row >> sh) == (col >> sh)
    before = (col < row) if fwd else (col > row)
    after = (col > row) if fwd else (col < row)
    pair = (row >> 1) == (col >> 1)
    joins = [((row >> (lvl + 1)) == (col >> (lvl + 1))) & ((row >> lvl) != (col >> lvl))
             for lvl in range(1, sh)]
    return dict(same=same, strict=same & before, incl=same & (before | (row == col)), later=same & after,
                eye=jnp.where(row == col, 1.0, 0.0), pair=pair, joins=joins,
                incl_bf=_mask_bf16(same & (before | (row == col))), later_bf=_mask_bf16(same & after),
                same_bf=_mask_bf16(same))


def _mask_bf16(m):
    return jnp.where(m, 1.0, 0.0).astype(BF16)


def _tri_inv(xs, masks, passes):
    ts = [m["eye"] + jnp.where(m["pair"], x, 0.0) for x, m in zip(xs, masks)]
    for lvl in range(len(masks[0]["joins"])):
        ps = [_dot(jnp.where(m["joins"][lvl], x, 0.0), t, passes) for x, m, t in zip(xs, masks, ts)]
        ts = [t + _dot(t, p, passes) for t, p in zip(ts, ps)]
    return ts


@functools.lru_cache(maxsize=None)
def _pos_table():
    t = np.arange(L_DEN)
    row = (t // GRID_W).astype(np.float32)
    col = (t % GRID_W).astype(np.float32)
    q = D // 4
    omega = np.exp(-math.log(10000.0) * np.arange(q, dtype=np.float32) / q).astype(np.float32)
    enc = lambda pos: np.concatenate([np.sin(pos[:, None] * omega), np.cos(pos[:, None] * omega)], -1)
    return np.concatenate([enc(row), enc(col)], -1).astype(np.float32)


@functools.lru_cache(maxsize=None)
def _dft_tables(L):
    f = np.arange(L, dtype=np.int64)
    m = (f[:, None] * f[None, :]) % (2 * L)
    ang = np.pi * m.astype(np.float64) / L
    return np.cos(ang).astype(np.float32), np.sin(ang).astype(np.float32)


@functools.lru_cache(maxsize=None)
def _hyena_static(L):
    k = np.arange(L, dtype=np.float32)
    t = k / np.float32(L - 1)
    bands = np.linspace(1e-4, HY_BANDS - 1, HY_BANDS, dtype=np.float32)
    ang = (np.float32(2.0 * math.pi) * k / np.float32(L))[:, None] * bands[None, :]
    feats = np.concatenate([t[:, None], np.cos(ang), -np.sin(ang)], -1).astype(np.float32)
    feats_p = np.zeros((L, LANES), np.float32)
    feats_p[:, :HY_EMB] = feats
    deltas = np.abs(np.linspace(math.log(HY_TARGET) / HY_FAST, math.log(HY_TARGET) / HY_SLOW, HY_W,
                                dtype=np.float32))
    window = np.exp(-t[:, None] * deltas[None, :]).astype(np.float32)
    return feats_p, window


def _block_ones(n, blk):
    i = np.arange(n) // blk
    return (i[:, None] == i[None, :]).astype(np.float32)


def _mod_kernel(c_ref, w_ref, b_ref, o_ref):
    o_ref[...] = _dot(_silu(c_ref[...]), w_ref[...], 6) + b_ref[...]


def modulation(cond16, ada_w, ada_b):
    depth = ada_w.shape[0]
    tn = 1024
    return pl.pallas_call(
        _mod_kernel,
        grid=(depth, 6 * D // tn),
        in_specs=[pl.BlockSpec((16, D), lambda i, j: (0, 0)),
                  pl.BlockSpec((None, D, tn), lambda i, j: (i, 0, j)),
                  pl.BlockSpec((None, 1, tn), lambda i, j: (i, 0, j))],
        out_specs=pl.BlockSpec((None, 16, tn), lambda i, j: (i, 0, j)),
        out_shape=jax.ShapeDtypeStruct((depth, 16, 6 * D), F32),
        compiler_params=_cparams("arbitrary", "arbitrary"),
    )(cond16, ada_w, ada_b.reshape(depth, 1, 6 * D))


def _assemble_kernel(xp_ref, xs_ref, pos_ref, o_ref):
    i = pl.program_id(0)

    @pl.when(i < T_CTX // 256)
    def _():
        o_ref[...] = xp_ref[...]

    @pl.when(i >= T_CTX // 256)
    def _():
        o_ref[...] = xs_ref[...] + pos_ref[...]


def assemble_tokens(x_prompt, x_sample):
    nc = T_CTX // 256
    pos = jnp.asarray(_pos_table())
    return pl.pallas_call(
        _assemble_kernel,
        grid=(T_ALL // 256,),
        in_specs=[pl.BlockSpec((256, D), lambda i: (jnp.minimum(i, nc - 1), 0)),
                  pl.BlockSpec((256, D), lambda i: (jnp.maximum(i - nc, 0), 0)),
                  pl.BlockSpec((256, D), lambda i: (jnp.maximum(i - nc, 0) % (L_DEN // 256), 0))],
        out_specs=pl.BlockSpec((256, D), lambda i: (i, 0)),
        out_shape=jax.ShapeDtypeStruct((T_ALL, D), F32),
        compiler_params=_cparams("arbitrary"),
    )(x_prompt.reshape(T_CTX, D), x_sample.reshape(T_DEN, D), pos)


def _norm_mod(x, g, sc, sh):
    y = x * lax.rsqrt(jnp.mean(x * x, -1, keepdims=True) + NORM_EPS)
    return (y * g) * (1.0 + sc) + sh


def _norm_mm_kernel(x_ref, g_ref, sc_ref, sh_ref, w_ref, o_ref, h_scr):
    @pl.when(pl.program_id(1) == 0)
    def _():
        h_scr[...] = _norm_mod(x_ref[...], g_ref[...], sc_ref[...], sh_ref[...]).astype(BF16)

    o_ref[...] = _dg(h_scr[...], w_ref[...]).astype(o_ref.dtype)


def norm_matmul(x, g, sc_t, sh_t, w_bf16, tn, out_dtype=F32):
    n = w_bf16.shape[1]
    return pl.pallas_call(
        _norm_mm_kernel,
        grid=(T_ALL // TM, n // tn),
        in_specs=[pl.BlockSpec((TM, D), lambda i, j: (i, 0)),
                  pl.BlockSpec((1, D), lambda i, j: (0, 0)),
                  pl.BlockSpec((None, 1, D), lambda i, j: (i, 0, 0)),
                  pl.BlockSpec((None, 1, D), lambda i, j: (i, 0, 0)),
                  pl.BlockSpec((D, tn), lambda i, j: (0, j))],
        out_specs=pl.BlockSpec((TM, tn), lambda i, j: (i, j)),
        out_shape=jax.ShapeDtypeStruct((T_ALL, n), out_dtype),
        scratch_shapes=[pltpu.VMEM((TM, D), BF16)],
        compiler_params=_cparams("arbitrary", "arbitrary"),
    )(x, g.reshape(1, D), sc_t, sh_t, w_bf16)


def _norm_swiglu_kernel(x_ref, g_ref, sc_ref, sh_ref, wg_ref, wu_ref, o_ref, h_scr):
    @pl.when(pl.program_id(1) == 0)
    def _():
        h_scr[...] = _norm_mod(x_ref[...], g_ref[...], sc_ref[...], sh_ref[...]).astype(BF16)

    h = h_scr[...]
    o_ref[...] = (_silu(_dg(h, wg_ref[...])) * _dg(h, wu_ref[...])).astype(o_ref.dtype)


def norm_swiglu(x, g, sc_t, sh_t, w_in_bf16, dff, tn):
    nj = dff // tn
    return pl.pallas_call(
        _norm_swiglu_kernel,
        grid=(T_ALL // TM, nj),
        in_specs=[pl.BlockSpec((TM, D), lambda i, j: (i, 0)),
                  pl.BlockSpec((1, D), lambda i, j: (0, 0)),
                  pl.BlockSpec((None, 1, D), lambda i, j: (i, 0, 0)),
                  pl.BlockSpec((None, 1, D), lambda i, j: (i, 0, 0)),
                  pl.BlockSpec((D, tn), lambda i, j: (0, j)),
                  pl.BlockSpec((D, tn), lambda i, j: (0, j + nj))],
        out_specs=pl.BlockSpec((TM, tn), lambda i, j: (i, j)),
        out_shape=jax.ShapeDtypeStruct((T_ALL, dff), BF16),
        scratch_shapes=[pltpu.VMEM((TM, D), BF16)],
        compiler_params=_cparams("arbitrary", "arbitrary"),
    )(x, g.reshape(1, D), sc_t, sh_t, w_in_bf16, w_in_bf16)


def _mm_res_kernel(y_ref, w_ref, x_ref, gate_ref, o_ref):
    o_ref[...] = x_ref[...] + gate_ref[...] * _dg(y_ref[...], w_ref[...])


def matmul_residual(y_bf16, w_bf16, x, gate_t, tn=256):
    k = y_bf16.shape[1]
    return pl.pallas_call(
        _mm_res_kernel,
        grid=(T_ALL // TM, D // tn),
        in_specs=[pl.BlockSpec((TM, k), lambda i, j: (i, 0)),
                  pl.BlockSpec((k, tn), lambda i, j: (0, j)),
                  pl.BlockSpec((TM, tn), lambda i, j: (i, j)),
                  pl.BlockSpec((None, 1, tn), lambda i, j: (i, 0, j))],
        out_specs=pl.BlockSpec((TM, tn), lambda i, j: (i, j)),
        out_shape=jax.ShapeDtypeStruct((T_ALL, D), F32),
        compiler_params=_cparams("arbitrary", "arbitrary"),
    )(y_bf16, w_bf16, x, gate_t)


def _hy_filter_kernel(feat_ref, w1_ref, b1_ref, f1_ref, w2_ref, b2_ref, f2_ref, w3f_ref, w3b_ref, win_ref,
                      c_ref, s_ref, hr_ref, hi_ref, hn_ref):
    L = feat_ref.shape[0]
    h = jnp.sin(f1_ref[...] * (_dot(feat_ref[...], w1_ref[...], 6) + b1_ref[...]))
    h = jnp.sin(f2_ref[...] * (_dot(h, w2_ref[...], 6) + b2_ref[...]))
    win = win_ref[...]
    fw = _dot(h, w3f_ref[...], 6) * win
    bw = _dot(h, w3b_ref[...], 6) * win
    row = lax.broadcasted_iota(jnp.int32, fw.shape, 0)
    bw = jnp.where(row == 0, 0.0, bw)
    nrm = jnp.sum(jnp.abs(fw), 0, keepdims=True) + jnp.sum(jnp.abs(bw), 0, keepdims=True)
    ev = (fw + bw) / nrm
    od = (bw - fw) / nrm
    alt = (1 - 2 * (row & 1)).astype(F32)
    hr_ref[...] = _dot(c_ref[...], ev, 6)
    hi_ref[...] = _dot(s_ref[...], od, 6)
    hn_ref[...] = jnp.broadcast_to(jnp.sum(ev * alt, 0, keepdims=True), (8, ev.shape[1]))
    del L


def hyena_filter_spectrum(L, w1, b1, f1, w2, b2, f2, w3):
    feats, window = _hyena_static(L)
    cos_t, sin_t = _dft_tables(L)
    tc = 128
    ncb = HY_W // tc
    w1p = jnp.zeros((LANES, HY_FFN), F32).at[:HY_EMB].set(w1)
    const = lambda shape: pl.BlockSpec(shape, lambda o, c: (0,) * len(shape))
    return pl.pallas_call(
        _hy_filter_kernel,
        grid=(2, ncb),
        in_specs=[const((L, LANES)), const((LANES, HY_FFN)), const((1, HY_FFN)), const((1, HY_FFN)),
                  const((HY_FFN, HY_FFN)), const((1, HY_FFN)), const((1, HY_FFN)),
                  pl.BlockSpec((HY_FFN, tc), lambda o, c: (0, o * ncb + c)),
                  pl.BlockSpec((HY_FFN, tc), lambda o, c: (0, 2 * ncb + o * ncb + c)),
                  pl.BlockSpec((L, tc), lambda o, c: (0, c)),
                  const((L, L)), const((L, L))],
        out_specs=[pl.BlockSpec((None, L, tc), lambda o, c: (o, 0, c)),
                   pl.BlockSpec((None, L, tc), lambda o, c: (o, 0, c)),
                   pl.BlockSpec((None, 8, tc), lambda o, c: (o, 0, c))],
        out_shape=[jax.ShapeDtypeStruct((2, L, HY_W), F32), jax.ShapeDtypeStruct((2, L, HY_W), F32),
                   jax.ShapeDtypeStruct((2, 8, HY_W), F32)],
        compiler_params=_cparams("arbitrary", "arbitrary"),
    )(jnp.asarray(feats), w1p, b1.reshape(1, -1), f1.reshape(1, -1), w2, b2.reshape(1, -1), f2.reshape(1, -1),
      w3, w3, jnp.asarray(window), jnp.asarray(cos_t), jnp.asarray(sin_t))


def _hy_mix_kernel(pv_ref, p1_ref, p2_ref, wv_ref, w1_ref, w2_ref, bv_ref, b1_ref, b2_ref,
                   hr_ref, hi_ref, hn_ref, bias_ref, ch_ref, cl_ref, sh_ref, sl_ref, o_ref, *, passes):
    L = pv_ref.shape[0]
    z = _conv3(pv_ref[...], wv_ref[...], bv_ref[...])
    gates = (_conv3(p1_ref[...], w1_ref[...], b1_ref[...]), _conv3(p2_ref[...], w2_ref[...], b2_ref[...]))
    row = lax.broadcasted_iota(jnp.int32, z.shape, 0)
    alt = (1 - 2 * (row & 1)).astype(F32)
    ch, cl, sh, sl = ch_ref[...], cl_ref[...], sh_ref[...], sl_ref[...]

    def tdot(th, tl, x):
        if passes == 1:
            return _dg(th, x.astype(BF16))
        xh, xl = _split2(x)
        return _dg(th, xh) + (_dg(th, xl) + _dg(tl, xh))

    inv_l = 1.0 / L
    for o in range(2):
        hr, hi, hn = hr_ref[o], hi_ref[o], hn_ref[o][0:1]
        zc = tdot(ch, cl, z)
        zs = tdot(sh, sl, z)
        zn = jnp.sum(z * alt, 0, keepdims=True)
        yr = zc * hr + zs * hi
        yi = zc * hi - zs * hr
        wr = jnp.where(row == 0, 0.5 * inv_l, inv_l)
        conv = tdot(ch, cl, yr * wr) - tdot(sh, sl, yi * inv_l) + alt * (zn * hn * (0.5 * inv_l))
        z = gates[o] * (conv + z * bias_ref[o:o + 1])
    o_ref[...] = z.astype(o_ref.dtype)


def hyena_mix(p, row0, nseq, L, conv_w, conv_b, spec, bias, passes):
    hr, hi, hn = spec
    cos_t, sin_t = _dft_tables(L)
    ch, cl = _np_split2(cos_t)
    sh, sl = _np_split2(sin_t)
    tc = 128
    ncb = HY_W // tc
    b0 = row0 // L
    pspec = lambda part: pl.BlockSpec((L, tc), lambda b, c: (b0 + b, part * ncb + c))
    wspec = lambda part: pl.BlockSpec((3, tc), lambda b, c: (0, part * ncb + c))
    bspec = lambda part: pl.BlockSpec((1, tc), lambda b, c: (0, part * ncb + c))
    hspec = lambda rows: pl.BlockSpec((2, rows, tc), lambda b, c: (0, 0, c))
    tab = pl.BlockSpec((L, L), lambda b, c: (0, 0))
    return pl.pallas_call(
        functools.partial(_hy_mix_kernel, passes=passes),
        grid=(nseq, ncb),
        in_specs=[pspec(0), pspec(1), pspec(2), wspec(0), wspec(1), wspec(2), bspec(0), bspec(1), bspec(2),
                  hspec(L), hspec(L), hspec(8), pl.BlockSpec((2, tc), lambda b, c: (0, c)), tab, tab, tab, tab],
        out_specs=pl.BlockSpec((L, tc), lambda b, c: (b, c)),
        out_shape=jax.ShapeDtypeStruct((nseq * L, HY_W), BF16),
        compiler_params=_cparams("arbitrary", "arbitrary"),
    )(p, p, p, conv_w, conv_w, conv_w, conv_b, conv_b, conv_b, hr, hi, hn, bias,
      jnp.asarray(ch), jnp.asarray(cl), jnp.asarray(sh), jnp.asarray(sl))


def _np_split2(x):
    hi = x.astype(jnp.bfloat16)
    lo = (x - hi.astype(np.float32)).astype(jnp.bfloat16)
    return hi, lo


def _rw_prep_kernel(pr_ref, pk_ref, pv_ref, pl_ref, mur_ref, muk_ref, muv_ref, mul_ref, w0_ref, a0_ref,
                    w2_ref, a2_ref, g2_ref, kkw_ref, kaw_ref, rkw_ref, ones_ref,
                    r_ref, v_ref, kk_ref, lw_ref, kd_ref, bd_ref, bon_ref, g_ref):
    def shift(p, mu):
        prev, nxt = _shift_rows(p)
        return p + (0.5 * (prev + nxt) - p) * mu

    r = shift(pr_ref[...], mur_ref[...])
    k = shift(pk_ref[...], muk_ref[...])
    v = shift(pv_ref[...], muv_ref[...])
    lo = shift(pl_ref[...], mul_ref[...])
    ones = ones_ref[...]
    g_ref[...] = _dot(_sigmoid(lo), g2_ref[...], 1)
    kkr = k * kkw_ref[...]
    kk = kkr / jnp.maximum(jnp.sqrt(_dot_exact_r(kkr * kkr, ones)), 1e-12)
    th = jnp.tanh(lo)
    bon = jnp.zeros_like(r)
    for d in range(2):
        w = -_softplus(-(w0_ref[d:d + 1] + _dot(th, w2_ref[d], 3))) - 0.5
        lw_ref[d] = -jnp.exp(w)
        a = _sigmoid(a0_ref[d:d + 1] + _dot(lo, a2_ref[d], 1))
        kd = k * (1.0 + (a - 1.0) * kaw_ref[...])
        kd_ref[d] = kd
        bd_ref[d] = kk * a
        bon = bon + _dot_exact_r(r * kd * rkw_ref[...], ones) * v
    r_ref[...] = r
    v_ref[...] = v
    kk_ref[...] = kk
    bon_ref[...] = bon


def rwkv_prep(p, row0, nseq, L, mu, w0, w2, a0, a2, g2, k_k, k_a, r_k):
    b0 = row0 // L
    cb = LANES
    ncb = RW_W // cb
    c0 = 3 * HY_W // cb
    wide = lambda part: pl.BlockSpec((L, cb), lambda b, c: (b0 + b, c0 + part * ncb + c))
    lora = pl.BlockSpec((L, 256), lambda b, c: (b0 + b, (3 * HY_W + 3 * RW_W) // 256))
    muw = lambda part: pl.BlockSpec((1, cb), lambda b, c: (0, part * ncb + c))
    vec = lambda rows: pl.BlockSpec((rows, cb), lambda b, c: (0, c))
    mu_p = jnp.zeros((1, 3 * RW_W + 256), F32).at[0, :3 * RW_W + RW_LORA].set(mu)
    w2f = jnp.zeros((2, 256, RW_W), F32).at[0, 0:32].set(w2[0]).at[1, 32:64].set(w2[1])
    a2f = jnp.zeros((2, 256, RW_W), F32).at[0, 64:96].set(a2[0]).at[1, 96:128].set(a2[1])
    g2f = jnp.zeros((256, RW_W), F32).at[128:224].set(g2)
    n = nseq * L
    one = jax.ShapeDtypeStruct((n, RW_W), F32)
    two = jax.ShapeDtypeStruct((2, n, RW_W), F32)
    ospec1 = pl.BlockSpec((L, cb), lambda b, c: (b, c))
    ospec2 = pl.BlockSpec((2, L, cb), lambda b, c: (0, b, c))
    return pl.pallas_call(
        _rw_prep_kernel,
        grid=(nseq, ncb),
        in_specs=[wide(0), wide(1), wide(2), lora, muw(0), muw(1), muw(2),
                  pl.BlockSpec((1, 256), lambda b, c: (0, 3 * RW_W // 256)),
                  vec(2), vec(2),
                  pl.BlockSpec((2, 256, cb), lambda b, c: (0, 0, c)),
                  pl.BlockSpec((2, 256, cb), lambda b, c: (0, 0, c)),
                  pl.BlockSpec((256, cb), lambda b, c: (0, c)),
                  vec(1), vec(1), vec(1),
                  pl.BlockSpec((cb, cb), lambda b, c: (0, 0))],
        out_specs=[ospec1, ospec1, ospec1, ospec2, ospec2, ospec2, ospec1, ospec1],
        out_shape=[one, one, one, two, two, two, one, one],
        compiler_params=_cparams("arbitrary", "arbitrary"),
    )(p, p, p, p, mu_p, mu_p, mu_p, mu_p, w0, a0, w2f, a2f, g2f, k_k.reshape(1, RW_W), k_a.reshape(1, RW_W),
      r_k.reshape(1, RW_W), jnp.asarray(_block_ones(cb, RW_N)).astype(BF16))


def _rw_blocks(ch, passes):
    n = SUPER
    c = RW_CHUNK
    nc = n // c
    idx = range(len(ch))
    cum = [_dot_exact_l(x["masks"]["incl_bf"], x["lw"]) for x in ch]
    suf = [_dot_exact_l(x["masks"]["later_bf"], x["lw"]) for x in ch]
    tot = [_dot_exact_l(x["masks"]["same_bf"], x["lw"]) for x in ch]
    e_neg = [jnp.exp(-cum[i]) for i in idx]
    at = [ch[i]["a"] * jnp.exp(cum[i] - ch[i]["lw"]) for i in idx]
    rt = [ch[i]["r"] * jnp.exp(cum[i]) for i in idx]
    bk = [jnp.concatenate([ch[i]["b"] * e_neg[i], ch[i]["k"] * e_neg[i]], 0) for i in idx]
    e_suf = [jnp.exp(suf[i]) for i in idx]
    bp = [ch[i]["b"] * e_suf[i] for i in idx]
    kp = [ch[i]["k"] * e_suf[i] for i in idx]
    lane = lax.broadcasted_iota(jnp.int32, (1, LANES), 1)
    heads = range(LANES // RW_N)
    sub = [(i, g) for i in idx for g in heads]
    mg = [(lane >> 6) == g for g in heads]
    at_g = [jnp.where(mg[g], at[i], 0.0) for i, g in sub]
    v_g = [jnp.where(mg[g], ch[i]["v"], 0.0) for i, g in sub]
    m = [_dot(jnp.concatenate([at_g[j], jnp.where(mg[g], rt[i], 0.0)], 0), bk[i], passes, _NT)
         for j, (i, g) in enumerate(sub)]
    smask = [ch[i]["masks"] for i, g in sub]
    ab = [jnp.where(smask[j]["strict"], m[j][:n, :n], 0.0) for j in range(len(sub))]
    ak = [jnp.where(smask[j]["strict"], m[j][:n, n:], 0.0) for j in range(len(sub))]
    rb = [jnp.where(smask[j]["incl"], m[j][n:, :n], 0.0) for j in range(len(sub))]
    rk = [jnp.where(smask[j]["incl"], m[j][n:, n:], 0.0) for j in range(len(sub))]
    tinv = _tri_inv(ab, smask, passes)
    akv = [_dot(ak[j], v_g[j], passes) for j in range(len(sub))]
    aw = [_dot(tinv[j], jnp.concatenate([at_g[j], akv[j]], 1), passes) for j in range(len(sub))]
    ry = [_dot(rb[j], aw[j], passes) for j in range(len(sub))]
    rkv = [_dot(rk[j], v_g[j], passes) for j in range(len(sub))]
    nh = len(heads)
    ahat = [sum(aw[i * nh + g][:, :LANES] for g in heads) for i in idx]
    w1 = [sum(aw[i * nh + g][:, LANES:] for g in heads) for i in idx]
    rhat = [rt[i] + sum(ry[i * nh + g][:, :LANES] for g in heads) for i in idx]
    y0 = [sum(ry[i * nh + g][:, LANES:] + rkv[i * nh + g] for g in heads) for i in idx]
    rowl = lax.broadcasted_iota(jnp.int32, (LANES, LANES), 0)
    coll = lax.broadcasted_iota(jnp.int32, (LANES, LANES), 1)
    diag_blocks = (rowl >> 6) == (coll >> 6)
    s = [x["s"] for x in ch]
    ys = [[None] * nc for _ in ch]
    for step in range(nc):
        ci = [step if x["fwd"] else nc - 1 - step for x in ch]
        sl = [slice(ci[i] * c, (ci[i] + 1) * c) for i in idx]
        xx = [_dot(jnp.concatenate([ahat[i][sl[i]], rhat[i][sl[i]]], 0), s[i], passes, _NT) for i in idx]
        u = [w1[i][sl[i]] + xx[i][:c] for i in idx]
        for i in idx:
            ys[i][ci[i]] = y0[i][sl[i]] + xx[i][c:]
        upd = [_dot(jnp.concatenate([u[i], ch[i]["v"][sl[i]]], 0),
                    jnp.concatenate([bp[i][sl[i]], kp[i][sl[i]]], 0), passes, _TN) for i in idx]
        s = [s[i] * jnp.exp(tot[i][ci[i] * c:ci[i] * c + 1]) + jnp.where(diag_blocks, upd[i], 0.0) for i in idx]
    return [jnp.concatenate(y, 0) for y in ys], s


def _rw_scan_kernel(r_ref, v_ref, kk_ref, lw_ref, kd_ref, bd_ref, s0_ref, y_ref, s_ref, *, passes, gp):
    L = r_ref.shape[0]
    nblk = L // SUPER

    def body(i, carry):
        chains, where = [], []
        for d in range(2):
            masks = _chunk_masks(SUPER, RW_CHUNK, d == 0)
            j = i if d == 0 else nblk - 1 - i
            rows = pl.ds(pl.multiple_of(j * SUPER, SUPER), SUPER)
            for g in range(gp):
                cols = slice(g * LANES, (g + 1) * LANES)
                chains.append(dict(r=r_ref[rows, cols], lw=lw_ref[d, rows, cols], k=kd_ref[d, rows, cols],
                                   v=v_ref[rows, cols], a=-kk_ref[rows, cols], b=bd_ref[d, rows, cols],
                                   s=carry[d * gp + g], masks=masks, fwd=d == 0))
                where.append((d, rows, cols))
        ys, ss = _rw_blocks(chains, passes)
        for (d, rows, cols), y in zip(where, ys):
            y_ref[d, rows, cols] = y
        return tuple(ss)

    s_fin = lax.fori_loop(0, nblk, body, tuple(s0_ref[d, g] for d in range(2) for g in range(gp)))
    for d in range(2):
        for g in range(gp):
            s_ref[d, g] = s_fin[d * gp + g]


def rwkv_scan(r, v, kk, lw, kd, bd, s0_bd, nseq, L, passes, gp):
    ngrp = RW_W // LANES
    w = gp * LANES
    one = pl.BlockSpec((L, w), lambda b, g: (b, g))
    two = pl.BlockSpec((2, L, w), lambda b, g: (0, b, g))
    st = pl.BlockSpec((None, 2, gp, LANES, LANES), lambda b, g: (b, 0, g, 0, 0))
    return pl.pallas_call(
        functools.partial(_rw_scan_kernel, passes=passes, gp=gp),
        grid=(nseq, ngrp // gp),
        in_specs=[one, one, one, two, two, two, st],
        out_specs=[two, st],
        out_shape=[jax.ShapeDtypeStruct((2, nseq * L, RW_W), F32),
                   jax.ShapeDtypeStruct((nseq, 2, ngrp, LANES, LANES), F32)],
        compiler_params=_cparams("arbitrary", "arbitrary"),
    )(r, v, kk, lw, kd, bd, s0_bd)


def _rw_post_kernel(y_ref, bon_ref, g_ref, lnw_ref, lnb_ref, ones_ref, o_ref):
    y = y_ref[0] + y_ref[1]
    ones = ones_ref[...]
    mean = _dot_exact_r(y, ones) * (1.0 / RW_N)
    yc = y - mean
    var = _dot_exact_r(yc * yc, ones) * (1.0 / RW_N)
    yn = yc * lax.rsqrt(var + RW_LN_EPS) * lnw_ref[...] + lnb_ref[...]
    o_ref[...] = ((yn + bon_ref[...]) * g_ref[...]).astype(o_ref.dtype)


def rwkv_post(y2, bonus, g, ln_w, ln_b):
    n = bonus.shape[0]
    tm = 512
    row = pl.BlockSpec((tm, RW_W), lambda i: (i, 0))
    vec = pl.BlockSpec((1, RW_W), lambda i: (0, 0))
    return pl.pallas_call(
        _rw_post_kernel,
        grid=(n // tm,),
        in_specs=[pl.BlockSpec((2, tm, RW_W), lambda i: (0, i, 0)), row, row, vec, vec,
                  pl.BlockSpec((RW_W, RW_W), lambda i: (0, 0))],
        out_specs=row,
        out_shape=jax.ShapeDtypeStruct((n, RW_W), BF16),
        compiler_params=_cparams("arbitrary"),
    )(y2, bonus, g, ln_w.reshape(1, RW_W), ln_b.reshape(1, RW_W),
      jnp.asarray(_block_ones(RW_W, RW_N)).astype(BF16))


def _rw_states_to_blockdiag(s):
    b = s.shape[0]
    s = s.reshape(b, 2, 4, 2, RW_N, RW_N)
    z = jnp.zeros_like(s[:, :, :, 0])
    top = jnp.concatenate([s[:, :, :, 0], z], -1)
    bot = jnp.concatenate([z, s[:, :, :, 1]], -1)
    return jnp.concatenate([top, bot], -2)


def _rw_states_from_blockdiag(s):
    b = s.shape[0]
    return jnp.stack([s[:, :, :, :RW_N, :RW_N], s[:, :, :, RW_N:, RW_N:]], 3).reshape(b, 2, RW_H, RW_N, RW_N)


def _gd_blocks(ch, passes):
    n = SUPER
    c = GD_CHUNK
    nc = n // c
    idx = range(len(ch))
    msk = [x["masks"] for x in ch]
    cum = [_dot_exact_l(x["masks"]["incl_bf"], x["lw"]) for x in ch]
    suf = [_dot_exact_l(x["masks"]["later_bf"], x["lw"]) for x in ch]
    tot = [_dot_exact_l(x["masks"]["same_bf"], x["lw"]) for x in ch]
    lane = lax.broadcasted_iota(jnp.int32, (n, LANES), 1)
    pick0 = _mask_bf16(lane == 0)
    cum_row = [_dot_exact_l(pick0, cum[i], _NT) for i in idx]
    gam = [jnp.exp(jnp.where(msk[i]["incl"], cum[i][:, 0:1] - cum_row[i], -jnp.inf)) for i in idx]
    kb = [x["k"] * x["beta"] for x in ch]
    m = [_dot(jnp.concatenate([kb[i], ch[i]["q"]], 0), ch[i]["k"], passes, _NT) for i in idx]
    a = [jnp.where(msk[i]["strict"], m[i][:n] * gam[i], 0.0) for i in idx]
    qk = [jnp.where(msk[i]["incl"], m[i][n:] * gam[i], 0.0) for i in idx]
    tinv = _tri_inv([-x for x in a], msk, passes)
    e_cum = [jnp.exp(cum[i]) for i in idx]
    uw = [_dot(tinv[i], jnp.concatenate([ch[i]["v"] * ch[i]["beta"], kb[i] * e_cum[i]], 1), passes) for i in idx]
    qq = [_dot(qk[i], uw[i], passes) for i in idx]
    u = [uw[i][:, :LANES] for i in idx]
    w = [uw[i][:, LANES:] for i in idx]
    o0 = [qq[i][:, :LANES] for i in idx]
    qhat = [ch[i]["q"] * e_cum[i] - qq[i][:, LANES:] for i in idx]
    kd = [ch[i]["k"] * jnp.exp(suf[i]) for i in idx]
    s = [x["s"] for x in ch]
    os_ = [[None] * nc for _ in ch]
    for step in range(nc):
        ci = [step if x["fwd"] else nc - 1 - step for x in ch]
        sl = [slice(ci[i] * c, (ci[i] + 1) * c) for i in idx]
        xx = [_dot(jnp.concatenate([w[i][sl[i]], qhat[i][sl[i]]], 0), s[i], passes) for i in idx]
        vn = [u[i][sl[i]] - xx[i][:c] for i in idx]
        for i in idx:
            os_[i][ci[i]] = o0[i][sl[i]] + xx[i][c:]
        upd = [_dot(kd[i][sl[i]], vn[i], passes, _TN) for i in idx]
        s = [s[i] * jnp.exp(tot[i][ci[i] * c:ci[i] * c + 1]) + upd[i] for i in idx]
    return [jnp.concatenate(o, 0) for o in os_], s


def _gd_kernel(pq_ref, pk_ref, pv_ref, pz_ref, pab_ref, wq_ref, wk_ref, wv_ref, alog_ref, dtb_ref, ng_ref,
               s0_ref, o_ref, s_ref, q_scr, k_scr, v_scr, gb_scr, o_scr, *, passes, hg):
    L = pq_ref.shape[0]
    h0 = pl.program_id(1) * hg
    l2n = lambda t: t * lax.rsqrt(jnp.sum(t * t, -1, keepdims=True) + 1e-6)
    pab = pab_ref[...]
    lane = lax.broadcasted_iota(jnp.int32, pab.shape, 1)
    g_all = -jnp.exp(alog_ref[...]) * _softplus(pab + dtb_ref[...])
    b_all = _sigmoid(pab)
    for hh in range(hg):
        cols = slice(hh * LANES, (hh + 1) * LANES)
        q_scr[:, cols] = l2n(_silu(_conv3(pq_ref[:, cols], wq_ref[:, cols]))) * (GD_DK ** -0.5)
        k_scr[:, cols] = l2n(_silu(_conv3(pk_ref[:, cols], wk_ref[:, cols])))
        v_scr[:, cols] = _silu(_conv3(pv_ref[:, cols], wv_ref[:, cols]))
        for d in range(2):
            g = jnp.sum(jnp.where(lane == d * GD_H + h0 + hh, g_all, 0.0), -1, keepdims=True)
            bt = jnp.sum(jnp.where(lane == 2 * GD_H + d * GD_H + h0 + hh, b_all, 0.0), -1, keepdims=True)
            gb_scr[d, :, cols] = jnp.broadcast_to(g, (L, LANES))
            gb_scr[2 + d, :, cols] = jnp.broadcast_to(bt, (L, LANES))
    nblk = L // SUPER

    def body(i, carry):
        chains, where = [], []
        for d in range(2):
            masks = _chunk_masks(SUPER, GD_CHUNK, d == 0)
            j = i if d == 0 else nblk - 1 - i
            rows = pl.ds(pl.multiple_of(j * SUPER, SUPER), SUPER)
            for hh in range(hg):
                cols = slice(hh * LANES, (hh + 1) * LANES)
                chains.append(dict(q=q_scr[rows, cols], k=k_scr[rows, cols], v=v_scr[rows, cols],
                                   lw=gb_scr[d, rows, cols], beta=gb_scr[2 + d, rows, cols],
                                   s=carry[d * hg + hh], masks=masks, fwd=d == 0))
                where.append((d, rows, cols))
        os_, ss = _gd_blocks(chains, passes)
        for (d, rows, cols), o in zip(where, os_):
            o_scr[d, rows, cols] = o
        return tuple(ss)

    s_fin = lax.fori_loop(0, nblk, body, tuple(s0_ref[d, hh] for d in range(2) for hh in range(hg)))
    for d in range(2):
        for hh in range(hg):
            s_ref[d, hh] = s_fin[d * hg + hh]
    for hh in range(hg):
        cols = slice(hh * LANES, (hh + 1) * LANES)
        o = o_scr[0, :, cols] + o_scr[1, :, cols]
        o = o * lax.rsqrt(jnp.mean(o * o, -1, keepdims=True) + NORM_EPS) * ng_ref[...]
        o_ref[:, cols] = (o * _silu(pz_ref[:, cols])).astype(o_ref.dtype)


def gdn_mix(p, row0, nseq, L, conv_w, a_log, dt_bias, norm_g, s0, passes, hg):
    b0 = row0 // L
    w = hg * LANES
    nhb = GD_H // hg
    col = lambda part: pl.BlockSpec((L, w), lambda b, h: (b0 + b, part * nhb + h))
    wcol = lambda part: pl.BlockSpec((3, w), lambda b, h: (0, part * nhb + h))
    vec = pl.BlockSpec((1, LANES), lambda b, h: (0, 0))
    st = pl.BlockSpec((None, 2, hg, LANES, LANES), lambda b, h: (b, 0, h, 0, 0))
    alog_row = jnp.zeros((1, LANES), F32).at[0, :2 * GD_H].set(a_log.reshape(-1))
    dtb_row = jnp.zeros((1, LANES), F32).at[0, :2 * GD_H].set(dt_bias.reshape(-1))
    return pl.pallas_call(
        functools.partial(_gd_kernel, passes=passes, hg=hg),
        grid=(nseq, nhb),
        in_specs=[col(0), col(1), col(2), col(3),
                  pl.BlockSpec((L, LANES), lambda b, h: (b0 + b, 4 * GD_H)),
                  wcol(0), wcol(1), wcol(2), vec, vec, vec, st],
        out_specs=[pl.BlockSpec((L, w), lambda b, h: (b, h)), st],
        out_shape=[jax.ShapeDtypeStruct((nseq * L, GD_H * LANES), BF16),
                   jax.ShapeDtypeStruct((nseq, 2, GD_H, LANES, LANES), F32)],
        scratch_shapes=[pltpu.VMEM((L, w), F32), pltpu.VMEM((L, w), F32), pltpu.VMEM((L, w), F32),
                        pltpu.VMEM((4, L, w), F32), pltpu.VMEM((2, L, w), F32)],
        compiler_params=_cparams("arbitrary", "arbitrary"),
    )(p, p, p, p, p, conv_w, conv_w, conv_w, alog_row, dtb_row, norm_g.reshape(1, LANES), s0)


def _router_kernel(x_ref, g_ref, sc_ref, sh_ref, wr_ref, h_ref, comb_ref):
    h = _norm_mod(x_ref[...], g_ref[...], sc_ref[...], sh_ref[...])
    h_ref[...] = h.astype(BF16)
    logits = _dot(h, wr_ref[...], 6)
    lane = lax.broadcasted_iota(jnp.int32, logits.shape, 1)
    logits = jnp.where(lane < N_EXP, logits, -jnp.inf)
    m1 = jnp.max(logits, -1, keepdims=True)
    i1 = jnp.min(jnp.where(logits == m1, lane, LANES), -1, keepdims=True)
    rest = jnp.where(lane == i1, -jnp.inf, logits)
    m2 = jnp.max(rest, -1, keepdims=True)
    i2 = jnp.min(jnp.where(rest == m2, lane, LANES), -1, keepdims=True)
    e2 = jnp.exp(m2 - m1)
    g1 = 1.0 / (1.0 + e2)
    g2 = e2 / (1.0 + e2)
    comb_ref[...] = jnp.where(lane == i1, g1, 0.0) + jnp.where(lane == i2, g2, 0.0)


def moe_router(x, g, sc_t, sh_t, router):
    tm = 512
    per = TM // tm
    wr = jnp.zeros((D, LANES), F32).at[:, :N_EXP].set(router)
    return pl.pallas_call(
        _router_kernel,
        grid=(T_ALL // tm,),
        in_specs=[pl.BlockSpec((tm, D), lambda i: (i, 0)),
                  pl.BlockSpec((1, D), lambda i: (0, 0)),
                  pl.BlockSpec((None, 1, D), lambda i: (i // per, 0, 0)),
                  pl.BlockSpec((None, 1, D), lambda i: (i // per, 0, 0)),
                  pl.BlockSpec((D, LANES), lambda i: (0, 0))],
        out_specs=[pl.BlockSpec((tm, D), lambda i: (i, 0)), pl.BlockSpec((tm, LANES), lambda i: (i, 0))],
        out_shape=[jax.ShapeDtypeStruct((T_ALL, D), BF16), jax.ShapeDtypeStruct((T_ALL, LANES), F32)],
        compiler_params=_cparams("arbitrary"),
    )(x, g.reshape(1, D), sc_t, sh_t, wr)


def _moe_kernel(h_ref, comb_ref, wg_ref, wu_ref, wo_ref, x_ref, gate_ref, fg_ref, o_ref, acc):
    e = pl.program_id(1)
    f = pl.program_id(2)

    @pl.when((e == 0) & (f == 0))
    def _():
        acc[...] = jnp.zeros_like(acc)

    h = h_ref[...]
    comb = comb_ref[...]
    lane = lax.broadcasted_iota(jnp.int32, comb.shape, 1)
    ce = jnp.sum(jnp.where(lane == e, comb, 0.0), -1, keepdims=True)
    act = _silu(_dg(h, wg_ref[...])) * _dg(h, wu_ref[...])
    acc[...] += _dg((act * ce).astype(BF16), wo_ref[...])

    @pl.when((e == pl.num_programs(1) - 1) & (f == pl.num_programs(2) - 1))
    def _():
        xn = x_ref[...] + gate_ref[...] * acc[...]
        y = xn * lax.rsqrt(jnp.mean(xn * xn, -1, keepdims=True) + NORM_EPS)
        o_ref[...] = y * fg_ref[...]


def moe_ffn_final(h_bf16, comb, w_in_bf16, w_out_bf16, x, gate_t, final_g):
    tn = 512
    nf = E_FF // tn
    return pl.pallas_call(
        _moe_kernel,
        grid=(T_ALL // TM, N_EXP, nf),
        in_specs=[pl.BlockSpec((TM, D), lambda i, e, f: (i, 0)),
                  pl.BlockSpec((TM, LANES), lambda i, e, f: (i, 0)),
                  pl.BlockSpec((None, D, tn), lambda i, e, f: (e, 0, f)),
                  pl.BlockSpec((None, D, tn), lambda i, e, f: (e, 0, f + nf)),
                  pl.BlockSpec((None, tn, D), lambda i, e, f: (e, f, 0)),
                  pl.BlockSpec((TM, D), lambda i, e, f: (i, 0)),
                  pl.BlockSpec((None, 1, D), lambda i, e, f: (i, 0, 0)),
                  pl.BlockSpec((1, D), lambda i, e, f: (0, 0))],
        out_specs=pl.BlockSpec((TM, D), lambda i, e, f: (i, 0)),
        out_shape=jax.ShapeDtypeStruct((T_ALL, D), F32),
        scratch_shapes=[pltpu.VMEM((TM, D), F32)],
        compiler_params=_cparams("arbitrary", "arbitrary", "arbitrary"),
    )(h_bf16, comb, w_in_bf16, w_in_bf16, w_out_bf16, x, gate_t, final_g.reshape(1, D))


def _tile_rows(mod_l, k):
    idx = np.concatenate([np.zeros(T_CTX // TM, np.int32),
                          1 + np.arange(T_DEN // TM, dtype=np.int32) // (L_DEN // TM)])
    return mod_l[idx, k * D:(k + 1) * D][:, None, :]


def _pad_cols(w, n):
    return jnp.pad(w, ((0, 0), (0, n - w.shape[1])))


def kernel(x_prompt, x_sample, state_rwkv, state_gdn, c, c_ctx, ada_w, ada_b, norm_mix_g, norm_ffn_g, final_norm_g, e_w_in, e_hy_conv_w, e_hy_conv_b, e_hf_w1, e_hf_b1, e_hf_freq1, e_hf_w2, e_hf_b2, e_hf_freq2, e_hf_w3, e_hy_bias, e_rw_mu, e_rw_w0, e_rw_w2, e_rw_a0, e_rw_a2, e_rw_g2, e_rw_kk, e_rw_ka, e_rw_rk, e_rw_ln_w, e_rw_ln_b, e_w_out, e_ffn_w_in, e_ffn_w_out, o_w_in, o_conv_w, o_A_log, o_dt_bias, o_norm_g, o_w_out, o_router, o_moe_w_in, o_moe_w_out):
    passes = 1
    cond16 = jnp.zeros((16, D), F32).at[0].set(c_ctx).at[1:1 + B_DEN].set(c)
    mod = modulation(cond16, ada_w, ada_b)
    x = assemble_tokens(x_prompt, x_sample)

    m0 = [_tile_rows(mod[0], k) for k in range(6)]
    p = norm_matmul(x, norm_mix_g[0], m0[1], m0[0], _pad_cols(e_w_in[0], P_EVEN_PAD).astype(BF16), 256)
    ys_hy, ys_rw, st_rw = [], [], None
    for row0, nseq, L in ((0, B_CTX, L_CTX), (T_CTX, B_DEN, L_DEN)):
        spec = hyena_filter_spectrum(L, e_hf_w1[0], e_hf_b1[0], e_hf_freq1[0], e_hf_w2[0], e_hf_b2[0],
                                     e_hf_freq2[0], e_hf_w3[0])
        ys_hy.append(hyena_mix(p, row0, nseq, L, e_hy_conv_w[0], e_hy_conv_b[0].reshape(1, -1), spec,
                               e_hy_bias[0], 1))
        r, v, kk, lw, kd, bd, bonus, g = rwkv_prep(p, row0, nseq, L, e_rw_mu[0], e_rw_w0[0], e_rw_w2[0],
                                                   e_rw_a0[0], e_rw_a2[0], e_rw_g2[0], e_rw_kk[0], e_rw_ka[0],
                                                   e_rw_rk[0])
        if row0 == 0:
            s0 = jnp.zeros((nseq, 2, 4, LANES, LANES), F32)
        else:
            s0 = _rw_states_to_blockdiag(state_rwkv[:, 0])
        y2, s_new = rwkv_scan(r, v, kk, lw, kd, bd, s0, nseq, L, passes, 4)
        if row0 == 0:
            st_rw = _rw_states_from_blockdiag(s_new)
        ys_rw.append(rwkv_post(y2, bonus, g, e_rw_ln_w[0], e_rw_ln_b[0]))
    y_mix = jnp.concatenate([jnp.concatenate(ys_hy, 0), jnp.concatenate(ys_rw, 0)], 1)
    x = matmul_residual(y_mix, e_w_out[0].astype(BF16), x, m0[2])
    act = norm_swiglu(x, norm_ffn_g[0], m0[4], m0[3], e_ffn_w_in[0].astype(BF16), D_FF, 256)
    x = matmul_residual(act, e_ffn_w_out[0].astype(BF16), x, m0[5])

    m1 = [_tile_rows(mod[1], k) for k in range(6)]
    p = norm_matmul(x, norm_mix_g[1], m1[1], m1[0], _pad_cols(o_w_in[0], P_ODD_PAD).astype(BF16), 384)
    os_, st_gd = [], None
    for row0, nseq, L in ((0, B_CTX, L_CTX), (T_CTX, B_DEN, L_DEN)):
        if row0 == 0:
            s0 = jnp.zeros((nseq, 2, GD_H, LANES, LANES), F32)
        else:
            s0 = state_gdn[:, 0]
        o, s_new = gdn_mix(p, row0, nseq, L, o_conv_w[0], o_A_log[0], o_dt_bias[0], o_norm_g[0], s0, passes,
                           8 if L == L_CTX else 4)
        if row0 == 0:
            st_gd = s_new
        os_.append(o)
    x = matmul_residual(jnp.concatenate(os_, 0), o_w_out[0].astype(BF16), x, m1[2])
    h, comb = moe_router(x, norm_ffn_g[1], m1[4], m1[3], o_router[0])
    y = moe_ffn_final(h, comb, o_moe_w_in[0].astype(BF16), o_moe_w_out[0].astype(BF16), x, m1[5], final_norm_g)

    y_prompt = y[:T_CTX].reshape(B_CTX, L_CTX, D)
    y_sample = y[T_CTX:].reshape(B_DEN, L_DEN, D)
    return (y_prompt, y_sample, st_rw[:, None], st_gd[:, None])
```

```python
import functools
import math

import numpy as np
import jax
import jax.numpy as jnp
from jax import lax
from jax.experimental import pallas as pl
from jax.experimental.pallas import tpu as pltpu

F32 = jnp.float32
BF16 = jnp.bfloat16

D = 1024
B_CTX, L_CTX = 32, 256
B_DEN, L_DEN = 8, 1024
T_CTX = B_CTX * L_CTX
T_DEN = B_DEN * L_DEN
T_ALL = T_CTX + T_DEN
GRID_W = 64
NORM_EPS = 1e-6

HY_W = 512
HY_EMB = 33
HY_BANDS = 16
HY_FFN = 64
HY_TARGET, HY_FAST, HY_SLOW = 1e-2, 0.3, 1.5

RW_W = 512
RW_N = 64
RW_H = 8
RW_LORA = 224
RW_LN_EPS = 64e-5
P_EVEN = 3 * HY_W + 3 * RW_W + RW_LORA
P_EVEN_PAD = 3328

GD_H = 8
GD_DK = 128
GD_QKV = 3072
P_ODD = 4128
P_ODD_PAD = 4224

D_FF = 2816
N_EXP = 8
E_FF = 3584
MOE_TILE = 1024
MOE_ROWS = 2 * T_ALL + N_EXP * MOE_TILE
MOE_TILES = MOE_ROWS // MOE_TILE

LANES = 128
TM = 1024
RW_CHUNK = 32
GD_CHUNK = 64
SUPER = 128
VMEM_LIMIT = 56 * 1024 * 1024

_NN = (((1,), (0,)), ((), ()))
_NT = (((1,), (1,)), ((), ()))
_TN = (((0,), (0,)), ((), ()))


def _cparams(*sem):
    return pltpu.CompilerParams(dimension_semantics=sem, vmem_limit_bytes=VMEM_LIMIT)


def _dg(a, b, dims=_NN):
    return lax.dot_general(a, b, dims, preferred_element_type=F32)


def _split2(x):
    hi = x.astype(BF16)
    lo = (x - hi.astype(F32)).astype(BF16)
    return hi, lo


def _split3(x):
    x0 = x.astype(BF16)
    r1 = x - x0.astype(F32)
    x1 = r1.astype(BF16)
    x2 = (r1 - x1.astype(F32)).astype(BF16)
    return x0, x1, x2


def _dot(a, b, passes=1, dims=_NN):
    if passes == 1:
        return _dg(a.astype(BF16), b.astype(BF16), dims)
    if passes == 3:
        ah, al = _split2(a)
        bh, bl = _split2(b)
        return _dg(ah, bh, dims) + (_dg(ah, bl, dims) + _dg(al, bh, dims))
    a0, a1, a2 = _split3(a)
    b0, b1, b2 = _split3(b)
    small = _dg(a0, b2, dims) + _dg(a1, b1, dims) + _dg(a2, b0, dims)
    mid = _dg(a0, b1, dims) + _dg(a1, b0, dims)
    return _dg(a0, b0, dims) + (mid + small)


def _dot_exact_l(m, x, dims=_NN):
    x0, x1, x2 = _split3(x)
    return _dg(m, x0, dims) + (_dg(m, x1, dims) + _dg(m, x2, dims))


def _dot_exact_r(x, m, dims=_NN):
    x0, x1, x2 = _split3(x)
    return _dg(x0, m, dims) + (_dg(x1, m, dims) + _dg(x2, m, dims))


def _sigmoid(x):
    return 1.0 / (1.0 + jnp.exp(-x))


def _silu(x):
    return x * _sigmoid(x)


def _softplus(x):
    return jnp.maximum(x, 0.0) + jnp.log(1.0 + jnp.exp(-jnp.abs(x)))


def _shift_rows(x):
    n = x.shape[0]
    row = lax.broadcasted_iota(jnp.int32, x.shape, 0)
    prev = jnp.where(row == 0, 0.0, pltpu.roll(x, 1, 0))
    nxt = jnp.where(row == n - 1, 0.0, pltpu.roll(x, n - 1, 0))
    return prev, nxt


def _conv3(x, w, b=None):
    prev, nxt = _shift_rows(x)
    y = prev * w[0:1] + x * w[1:2] + nxt * w[2:3]
    return y if b is None else y + b


def _chunk_masks(n, chunk, fwd):
    row = lax.broadcasted_iota(jnp.int32, (n, n), 0)
    col = lax.broadcasted_iota(jnp.int32, (n, n), 1)
    sh = int(math.log2(chunk))
    same = (row >> sh) == (col >> sh)
    before = (col < row) if fwd else (col > row)
    pair = (row >> 1) == (col >> 1)
    joins = [((row >> (lvl + 1)) == (col >> (lvl + 1))) & ((row >> lvl) != (col >> lvl))
             for lvl in range(1, sh)]
    incl = same & (before | (row == col))
    return dict(strict=same & before, incl=incl, eye=jnp.where(row == col, 1.0, 0.0), pair=pair, joins=joins,
                incl_bf=_mask_bf16(incl))


def _mask_bf16(m):
    return jnp.where(m, 1.0, 0.0).astype(BF16)


def _chunk_totals(cum, chunk, fwd):
    n = cum.shape[0]
    rows = [cum[(ci + 1) * chunk - 1:(ci + 1) * chunk] if fwd else cum[ci * chunk:ci * chunk + 1]
            for ci in range(n // chunk)]
    return jnp.concatenate([jnp.broadcast_to(r, (chunk, cum.shape[1])) for r in rows], 0)


def _tri_inv(xs, masks, passes):
    ts = [m["eye"] + jnp.where(m["pair"], x, 0.0) for x, m in zip(xs, masks)]
    for lvl in range(len(masks[0]["joins"])):
        ps = [_dot(jnp.where(m["joins"][lvl], x, 0.0), t, passes) for x, m, t in zip(xs, masks, ts)]
        ts = [t + _dot(t, p, passes) for t, p in zip(ts, ps)]
    return ts


@functools.lru_cache(maxsize=None)
def _pos_table():
    t = np.arange(L_DEN)
    row = (t // GRID_W).astype(np.float32)
    col = (t % GRID_W).astype(np.float32)
    q = D // 4
    omega = np.exp(-math.log(10000.0) * np.arange(q, dtype=np.float32) / q).astype(np.float32)
    enc = lambda pos: np.concatenate([np.sin(pos[:, None] * omega), np.cos(pos[:, None] * omega)], -1)
    return np.concatenate([enc(row), enc(col)], -1).astype(np.float32)


@functools.lru_cache(maxsize=None)
def _dft_tables(L):
    f = np.arange(L, dtype=np.int64)
    m = (f[:, None] * f[None, :]) % (2 * L)
    ang = np.pi * m.astype(np.float64) / L
    return np.cos(ang).astype(np.float32), np.sin(ang).astype(np.float32)


@functools.lru_cache(maxsize=None)
def _hyena_static(L):
    k = np.arange(L, dtype=np.float32)
    t = k / np.float32(L - 1)
    bands = np.linspace(1e-4, HY_BANDS - 1, HY_BANDS, dtype=np.float32)
    ang = (np.float32(2.0 * math.pi) * k / np.float32(L))[:, None] * bands[None, :]
    feats = np.concatenate([t[:, None], np.cos(ang), -np.sin(ang)], -1).astype(np.float32)
    feats_p = np.zeros((L, LANES), np.float32)
    feats_p[:, :HY_EMB] = feats
    deltas = np.abs(np.linspace(math.log(HY_TARGET) / HY_FAST, math.log(HY_TARGET) / HY_SLOW, HY_W,
                                dtype=np.float32))
    window = np.exp(-t[:, None] * deltas[None, :]).astype(np.float32)
    return feats_p, window


def _block_ones(n, blk):
    i = np.arange(n) // blk
    return (i[:, None] == i[None, :]).astype(np.float32)


def _mod_kernel(c_ref, w_ref, b_ref, o_ref):
    o_ref[...] = _dot(_silu(c_ref[...]), w_ref[...], 6) + b_ref[...]


def modulation(cond16, ada_w, ada_b):
    depth = ada_w.shape[0]
    tn = 1024
    return pl.pallas_call(
        _mod_kernel,
        grid=(depth, 6 * D // tn),
        in_specs=[pl.BlockSpec((16, D), lambda i, j: (0, 0)),
                  pl.BlockSpec((None, D, tn), lambda i, j: (i, 0, j)),
                  pl.BlockSpec((None, 1, tn), lambda i, j: (i, 0, j))],
        out_specs=pl.BlockSpec((None, 16, tn), lambda i, j: (i, 0, j)),
        out_shape=jax.ShapeDtypeStruct((depth, 16, 6 * D), F32),
        compiler_params=_cparams("arbitrary", "arbitrary"),
    )(cond16, ada_w, ada_b.reshape(depth, 1, 6 * D))


def _assemble_kernel(xp_ref, xs_ref, pos_ref, o_ref):
    i = pl.program_id(0)

    @pl.when(i < T_CTX // 256)
    def _():
        o_ref[...] = xp_ref[...]

    @pl.when(i >= T_CTX // 256)
    def _():
        o_ref[...] = xs_ref[...] + pos_ref[...]


def assemble_tokens(x_prompt, x_sample):
    nc = T_CTX // 256
    pos = jnp.asarray(_pos_table())
    return pl.pallas_call(
        _assemble_kernel,
        grid=(T_ALL // 256,),
        in_specs=[pl.BlockSpec((256, D), lambda i: (jnp.minimum(i, nc - 1), 0)),
                  pl.BlockSpec((256, D), lambda i: (jnp.maximum(i - nc, 0), 0)),
                  pl.BlockSpec((256, D), lambda i: (jnp.maximum(i - nc, 0) % (L_DEN // 256), 0))],
        out_specs=pl.BlockSpec((256, D), lambda i: (i, 0)),
        out_shape=jax.ShapeDtypeStruct((T_ALL, D), F32),
        compiler_params=_cparams("arbitrary"),
    )(x_prompt.reshape(T_CTX, D), x_sample.reshape(T_DEN, D), pos)


def _norm_mod(x, g, sc, sh):
    y = x * lax.rsqrt(jnp.mean(x * x, -1, keepdims=True) + NORM_EPS)
    return (y * g) * (1.0 + sc) + sh


def _norm_mm_kernel(x_ref, g_ref, sc_ref, sh_ref, w_ref, o_ref, h_scr):
    @pl.when(pl.program_id(1) == 0)
    def _():
        h_scr[...] = _norm_mod(x_ref[...], g_ref[...], sc_ref[...], sh_ref[...]).astype(BF16)

    o_ref[...] = _dg(h_scr[...], w_ref[...]).astype(o_ref.dtype)


def norm_matmul(x, g, sc_t, sh_t, w_bf16, tn, out_dtype=F32):
    n = w_bf16.shape[1]
    return pl.pallas_call(
        _norm_mm_kernel,
        grid=(T_ALL // TM, n // tn),
        in_specs=[pl.BlockSpec((TM, D), lambda i, j: (i, 0)),
                  pl.BlockSpec((1, D), lambda i, j: (0, 0)),
                  pl.BlockSpec((None, 1, D), lambda i, j: (i, 0, 0)),
                  pl.BlockSpec((None, 1, D), lambda i, j: (i, 0, 0)),
                  pl.BlockSpec((D, tn), lambda i, j: (0, j))],
        out_specs=pl.BlockSpec((TM, tn), lambda i, j: (i, j)),
        out_shape=jax.ShapeDtypeStruct((T_ALL, n), out_dtype),
        scratch_shapes=[pltpu.VMEM((TM, D), BF16)],
        compiler_params=_cparams("arbitrary", "arbitrary"),
    )(x, g.reshape(1, D), sc_t, sh_t, w_bf16)


def _norm_swiglu_kernel(x_ref, g_ref, sc_ref, sh_ref, wg_ref, wu_ref, o_ref, h_scr):
    @pl.when(pl.program_id(1) == 0)
    def _():
        h_scr[...] = _norm_mod(x_ref[...], g_ref[...], sc_ref[...], sh_ref[...]).astype(BF16)

    h = h_scr[...]
    o_ref[...] = (_silu(_dg(h, wg_ref[...])) * _dg(h, wu_ref[...])).astype(o_ref.dtype)


def norm_swiglu(x, g, sc_t, sh_t, w_in_bf16, dff, tn):
    nj = dff // tn
    return pl.pallas_call(
        _norm_swiglu_kernel,
        grid=(T_ALL // TM, nj),
        in_specs=[pl.BlockSpec((TM, D), lambda i, j: (i, 0)),
                  pl.BlockSpec((1, D), lambda i, j: (0, 0)),
                  pl.BlockSpec((None, 1, D), lambda i, j: (i, 0, 0)),
                  pl.BlockSpec((None, 1, D), lambda i, j: (i, 0, 0)),
                  pl.BlockSpec((D, tn), lambda i, j: (0, j)),
                  pl.BlockSpec((D, tn), lambda i, j: (0, j + nj))],
        out_specs=pl.BlockSpec((TM, tn), lambda i, j: (i, j)),
        out_shape=jax.ShapeDtypeStruct((T_ALL, dff), BF16),
        scratch_shapes=[pltpu.VMEM((TM, D), BF16)],
        compiler_params=_cparams("arbitrary", "arbitrary"),
    )(x, g.reshape(1, D), sc_t, sh_t, w_in_bf16, w_in_bf16)


def _mm_res_kernel(y_ref, w_ref, x_ref, gate_ref, o_ref):
    o_ref[...] = x_ref[...] + gate_ref[...] * _dg(y_ref[...], w_ref[...])


def matmul_residual(y_bf16, w_bf16, x, gate_t, tn=256):
    k = y_bf16.shape[1]
    return pl.pallas_call(
        _mm_res_kernel,
        grid=(T_ALL // TM, D // tn),
        in_specs=[pl.BlockSpec((TM, k), lambda i, j: (i, 0)),
                  pl.BlockSpec((k, tn), lambda i, j: (0, j)),
                  pl.BlockSpec((TM, tn), lambda i, j: (i, j)),
                  pl.BlockSpec((None, 1, tn), lambda i, j: (i, 0, j))],
        out_specs=pl.BlockSpec((TM, tn), lambda i, j: (i, j)),
        out_shape=jax.ShapeDtypeStruct((T_ALL, D), F32),
        compiler_params=_cparams("arbitrary", "arbitrary"),
    )(y_bf16, w_bf16, x, gate_t)


def _hy_filter_kernel(feat_ref, w1_ref, b1_ref, f1_ref, w2_ref, b2_ref, f2_ref, w3f_ref, w3b_ref, win_ref,
                      c_ref, s_ref, hr_ref, hi_ref, hn_ref):
    L = feat_ref.shape[0]
    h = jnp.sin(f1_ref[...] * (_dot(feat_ref[...], w1_ref[...], 6) + b1_ref[...]))
    h = jnp.sin(f2_ref[...] * (_dot(h, w2_ref[...], 6) + b2_ref[...]))
    win = win_ref[...]
    fw = _dot(h, w3f_ref[...], 6) * win
    bw = _dot(h, w3b_ref[...], 6) * win
    row = lax.broadcasted_iota(jnp.int32, fw.shape, 0)
    bw = jnp.where(row == 0, 0.0, bw)
    nrm = jnp.sum(jnp.abs(fw), 0, keepdims=True) + jnp.sum(jnp.abs(bw), 0, keepdims=True)
    ev = (fw + bw) / nrm
    od = (bw - fw) / nrm
    alt = (1 - 2 * (row & 1)).astype(F32)
    hr_ref[...] = _dot(c_ref[...], ev, 6)
    hi_ref[...] = _dot(s_ref[...], od, 6)
    hn_ref[...] = jnp.broadcast_to(jnp.sum(ev * alt, 0, keepdims=True), (8, ev.shape[1]))
    del L


def hyena_filter_spectrum(L, w1, b1, f1, w2, b2, f2, w3):
    feats, window = _hyena_static(L)
    cos_t, sin_t = _dft_tables(L)
    tc = 128
    ncb = HY_W // tc
    w1p = jnp.zeros((LANES, HY_FFN), F32).at[:HY_EMB].set(w1)
    const = lambda shape: pl.BlockSpec(shape, lambda o, c: (0,) * len(shape))
    return pl.pallas_call(
        _hy_filter_kernel,
        grid=(2, ncb),
        in_specs=[const((L, LANES)), const((LANES, HY_FFN)), const((1, HY_FFN)), const((1, HY_FFN)),
                  const((HY_FFN, HY_FFN)), const((1, HY_FFN)), const((1, HY_FFN)),
                  pl.BlockSpec((HY_FFN, tc), lambda o, c: (0, o * ncb + c)),
                  pl.BlockSpec((HY_FFN, tc), lambda o, c: (0, 2 * ncb + o * ncb + c)),
                  pl.BlockSpec((L, tc), lambda o, c: (0, c)),
                  const((L, L)), const((L, L))],
        out_specs=[pl.BlockSpec((None, L, tc), lambda o, c: (o, 0, c)),
                   pl.BlockSpec((None, L, tc), lambda o, c: (o, 0, c)),
                   pl.BlockSpec((None, 8, tc), lambda o, c: (o, 0, c))],
        out_shape=[jax.ShapeDtypeStruct((2, L, HY_W), F32), jax.ShapeDtypeStruct((2, L, HY_W), F32),
                   jax.ShapeDtypeStruct((2, 8, HY_W), F32)],
        compiler_params=_cparams("arbitrary", "arbitrary"),
    )(jnp.asarray(feats), w1p, b1.reshape(1, -1), f1.reshape(1, -1), w2, b2.reshape(1, -1), f2.reshape(1, -1),
      w3, w3, jnp.asarray(window), jnp.asarray(cos_t), jnp.asarray(sin_t))


def _hy_mix_kernel(pv_ref, p1_ref, p2_ref, wv_ref, w1_ref, w2_ref, bv_ref, b1_ref, b2_ref,
                   hr_ref, hi_ref, hn_ref, bias_ref, ch_ref, cl_ref, sh_ref, sl_ref, o_ref, *, passes):
    L = pv_ref.shape[0]
    z = _conv3(pv_ref[...], wv_ref[...], bv_ref[...])
    gates = (_conv3(p1_ref[...], w1_ref[...], b1_ref[...]), _conv3(p2_ref[...], w2_ref[...], b2_ref[...]))
    row = lax.broadcasted_iota(jnp.int32, z.shape, 0)
    alt = (1 - 2 * (row & 1)).astype(F32)
    ch, cl, sh, sl = ch_ref[...], cl_ref[...], sh_ref[...], sl_ref[...]

    def tdot(th, tl, x):
        if passes == 1:
            return _dg(th, x.astype(BF16))
        xh, xl = _split2(x)
        return _dg(th, xh) + (_dg(th, xl) + _dg(tl, xh))

    inv_l = 1.0 / L
    for o in range(2):
        hr, hi, hn = hr_ref[o], hi_ref[o], hn_ref[o][0:1]
        zc = tdot(ch, cl, z)
        zs = tdot(sh, sl, z)
        zn = jnp.sum(z * alt, 0, keepdims=True)
        yr = zc * hr + zs * hi
        yi = zc * hi - zs * hr
        wr = jnp.where(row == 0, 0.5 * inv_l, inv_l)
        conv = tdot(ch, cl, yr * wr) - tdot(sh, sl, yi * inv_l) + alt * (zn * hn * (0.5 * inv_l))
        z = gates[o] * (conv + z * bias_ref[o:o + 1])
    o_ref[...] = z.astype(o_ref.dtype)


def hyena_mix(p, row0, nseq, L, conv_w, conv_b, spec, bias, passes):
    hr, hi, hn = spec
    cos_t, sin_t = _dft_tables(L)
    ch, cl = _np_split2(cos_t)
    sh, sl = _np_split2(sin_t)
    tc = 128
    ncb = HY_W // tc
    b0 = row0 // L
    pspec = lambda part: pl.BlockSpec((L, tc), lambda b, c: (b0 + b, part * ncb + c))
    wspec = lambda part: pl.BlockSpec((3, tc), lambda b, c: (0, part * ncb + c))
    bspec = lambda part: pl.BlockSpec((1, tc), lambda b, c: (0, part * ncb + c))
    hspec = lambda rows: pl.BlockSpec((2, rows, tc), lambda b, c: (0, 0, c))
    tab = pl.BlockSpec((L, L), lambda b, c: (0, 0))
    return pl.pallas_call(
        functools.partial(_hy_mix_kernel, passes=passes),
        grid=(nseq, ncb),
        in_specs=[pspec(0), pspec(1), pspec(2), wspec(0), wspec(1), wspec(2), bspec(0), bspec(1), bspec(2),
                  hspec(L), hspec(L), hspec(8), pl.BlockSpec((2, tc), lambda b, c: (0, c)), tab, tab, tab, tab],
        out_specs=pl.BlockSpec((L, tc), lambda b, c: (b, c)),
        out_shape=jax.ShapeDtypeStruct((nseq * L, HY_W), BF16),
        compiler_params=_cparams("arbitrary", "arbitrary"),
    )(p, p, p, conv_w, conv_w, conv_w, conv_b, conv_b, conv_b, hr, hi, hn, bias,
      jnp.asarray(ch), jnp.asarray(cl), jnp.asarray(sh), jnp.asarray(sl))


def _np_split2(x):
    hi = x.astype(jnp.bfloat16)
    lo = (x - hi.astype(np.float32)).astype(jnp.bfloat16)
    return hi, lo


def _rw_prep_kernel(pr_ref, pk_ref, pv_ref, pl_ref, mur_ref, muk_ref, muv_ref, mul_ref, w0_ref, a0_ref,
                    w2_ref, a2_ref, g2_ref, kkw_ref, kaw_ref, rkw_ref, ones_ref,
                    r_ref, v_ref, kk_ref, lw_ref, kd_ref, bd_ref, bon_ref, g_ref):
    def shift(p, mu):
        prev, nxt = _shift_rows(p)
        return p + (0.5 * (prev + nxt) - p) * mu

    r = shift(pr_ref[...], mur_ref[...])
    k = shift(pk_ref[...], muk_ref[...])
    v = shift(pv_ref[...], muv_ref[...])
    lo = shift(pl_ref[...], mul_ref[...])
    ones = ones_ref[...]
    g_ref[...] = _dot(_sigmoid(lo), g2_ref[...], 1)
    kkr = k * kkw_ref[...]
    kk = kkr / jnp.maximum(jnp.sqrt(_dot_exact_r(kkr * kkr, ones)), 1e-12)
    th = jnp.tanh(lo)
    bon = jnp.zeros_like(r)
    for d in range(2):
        w = -_softplus(-(w0_ref[d:d + 1] + _dot(th, w2_ref[d], 3))) - 0.5
        lw_ref[d] = -jnp.exp(w)
        a = _sigmoid(a0_ref[d:d + 1] + _dot(lo, a2_ref[d], 1))
        kd = k * (1.0 + (a - 1.0) * kaw_ref[...])
        kd_ref[d] = kd
        bd_ref[d] = kk * a
        bon = bon + _dot_exact_r(r * kd * rkw_ref[...], ones) * v
    r_ref[...] = r
    v_ref[...] = v
    kk_ref[...] = kk
    bon_ref[...] = bon


def rwkv_prep(p, row0, nseq, L, mu, w0, w2, a0, a2, g2, k_k, k_a, r_k):
    b0 = row0 // L
    cb = LANES
    ncb = RW_W // cb
    c0 = 3 * HY_W // cb
    wide = lambda part: pl.BlockSpec((L, cb), lambda b, c: (b0 + b, c0 + part * ncb + c))
    lora = pl.BlockSpec((L, 256), lambda b, c: (b0 + b, (3 * HY_W + 3 * RW_W) // 256))
    muw = lambda part: pl.BlockSpec((1, cb), lambda b, c: (0, part * ncb + c))
    vec = lambda rows: pl.BlockSpec((rows, cb), lambda b, c: (0, c))
    mu_p = jnp.zeros((1, 3 * RW_W + 256), F32).at[0, :3 * RW_W + RW_LORA].set(mu)
    w2f = jnp.zeros((2, 256, RW_W), F32).at[0, 0:32].set(w2[0]).at[1, 32:64].set(w2[1])
    a2f = jnp.zeros((2, 256, RW_W), F32).at[0, 64:96].set(a2[0]).at[1, 96:128].set(a2[1])
    g2f = jnp.zeros((256, RW_W), F32).at[128:224].set(g2)
    n = nseq * L
    one = jax.ShapeDtypeStruct((n, RW_W), F32)
    two = jax.ShapeDtypeStruct((2, n, RW_W), F32)
    ospec1 = pl.BlockSpec((L, cb), lambda b, c: (b, c))
    ospec2 = pl.BlockSpec((2, L, cb), lambda b, c: (0, b, c))
    return pl.pallas_call(
        _rw_prep_kernel,
        grid=(nseq, ncb),
        in_specs=[wide(0), wide(1), wide(2), lora, muw(0), muw(1), muw(2),
                  pl.BlockSpec((1, 256), lambda b, c: (0, 3 * RW_W // 256)),
                  vec(2), vec(2),
                  pl.BlockSpec((2, 256, cb), lambda b, c: (0, 0, c)),
                  pl.BlockSpec((2, 256, cb), lambda b, c: (0, 0, c)),
                  pl.BlockSpec((256, cb), lambda b, c: (0, c)),
                  vec(1), vec(1), vec(1),
                  pl.BlockSpec((cb, cb), lambda b, c: (0, 0))],
        out_specs=[ospec1, ospec1, ospec1, ospec2, ospec2, ospec2, ospec1, ospec1],
        out_shape=[one, one, one, two, two, two, one, one],
        compiler_params=_cparams("arbitrary", "arbitrary"),
    )(p, p, p, p, mu_p, mu_p, mu_p, mu_p, w0, a0, w2f, a2f, g2f, k_k.reshape(1, RW_W), k_a.reshape(1, RW_W),
      r_k.reshape(1, RW_W), jnp.asarray(_block_ones(cb, RW_N)).astype(BF16))


def _rw_blocks(ch, passes):
    n = SUPER
    c = RW_CHUNK
    nc = n // c
    idx = range(len(ch))
    cum = [_dot_exact_l(x["masks"]["incl_bf"], x["lw"]) for x in ch]
    tot = [_chunk_totals(cum[i], c, ch[i]["fwd"]) for i in idx]
    suf = [tot[i] - cum[i] for i in idx]
    e_neg =[jnp.exp(-cum[i]) for i in idx]
    at = [ch[i]["a"] * jnp.exp(cum[i] - ch[i]["lw"]) for i in idx]
    rt = [ch[i]["r"] * jnp.exp(cum[i]) for i in idx]
    bk = [jnp.concatenate([ch[i]["b"] * e_neg[i], ch[i]["k"] * e_neg[i]], 0) for i in idx]
    e_suf = [jnp.exp(suf[i]) for i in idx]
    bp = [ch[i]["b"] * e_suf[i] for i in idx]
    kp = [ch[i]["k"] * e_suf[i] for i in idx]
    lane = lax.broadcasted_iota(jnp.int32, (1, LANES), 1)
    heads = range(LANES // RW_N)
    sub = [(i, g) for i in idx for g in heads]
    mg = [(lane >> 6) == g for g in heads]
    at_g = [jnp.where(mg[g], at[i], 0.0) for i, g in sub]
    v_g = [jnp.where(mg[g], ch[i]["v"], 0.0) for i, g in sub]
    m = [_dot(jnp.concatenate([at_g[j], jnp.where(mg[g], rt[i], 0.0)], 0), bk[i], passes, _NT)
         for j, (i, g) in enumerate(sub)]
    smask = [ch[i]["masks"] for i, g in sub]
    ab = [jnp.where(smask[j]["strict"], m[j][:n, :n], 0.0) for j in range(len(sub))]
    ak = [jnp.where(smask[j]["strict"], m[j][:n, n:], 0.0) for j in range(len(sub))]
    rb = [jnp.where(smask[j]["incl"], m[j][n:, :n], 0.0) for j in range(len(sub))]
    rk = [jnp.where(smask[j]["incl"], m[j][n:, n:], 0.0) for j in range(len(sub))]
    tinv = _tri_inv(ab, smask, passes)
    akv = [_dot(ak[j], v_g[j], passes) for j in range(len(sub))]
    aw = [_dot(tinv[j], jnp.concatenate([at_g[j], akv[j]], 1), passes) for j in range(len(sub))]
    ry = [_dot(rb[j], aw[j], passes) for j in range(len(sub))]
    rkv = [_dot(rk[j], v_g[j], passes) for j in range(len(sub))]
    nh = len(heads)
    ahat = [sum(aw[i * nh + g][:, :LANES] for g in heads) for i in idx]
    w1 = [sum(aw[i * nh + g][:, LANES:] for g in heads) for i in idx]
    rhat = [rt[i] + sum(ry[i * nh + g][:, :LANES] for g in heads) for i in idx]
    y0 = [sum(ry[i * nh + g][:, LANES:] + rkv[i * nh + g] for g in heads) for i in idx]
    rowl = lax.broadcasted_iota(jnp.int32, (LANES, LANES), 0)
    coll = lax.broadcasted_iota(jnp.int32, (LANES, LANES), 1)
    diag_blocks = (rowl >> 6) == (coll >> 6)
    s = [x["s"] for x in ch]
    ys = [[None] * nc for _ in ch]
    for step in range(nc):
        ci = [step if x["fwd"] else nc - 1 - step for x in ch]
        sl = [slice(ci[i] * c, (ci[i] + 1) * c) for i in idx]
        xx = [_dot(jnp.concatenate([ahat[i][sl[i]], rhat[i][sl[i]]], 0), s[i], passes, _NT) for i in idx]
        u = [w1[i][sl[i]] + xx[i][:c] for i in idx]
        for i in idx:
            ys[i][ci[i]] = y0[i][sl[i]] + xx[i][c:]
        upd = [_dot(jnp.concatenate([u[i], ch[i]["v"][sl[i]]], 0),
                    jnp.concatenate([bp[i][sl[i]], kp[i][sl[i]]], 0), passes, _TN) for i in idx]
        s = [s[i] * jnp.exp(tot[i][ci[i] * c:ci[i] * c + 1]) + jnp.where(diag_blocks, upd[i], 0.0) for i in idx]
    return [jnp.concatenate(y, 0) for y in ys], s


def _rw_scan_kernel(r_ref, v_ref, kk_ref, lw_ref, kd_ref, bd_ref, s0_ref, y_ref, s_ref, *, passes, gp):
    L = r_ref.shape[0]
    nblk = L // SUPER

    def body(i, carry):
        chains, where = [], []
        for d in range(2):
            masks = _chunk_masks(SUPER, RW_CHUNK, d == 0)
            j = i if d == 0 else nblk - 1 - i
            rows = pl.ds(pl.multiple_of(j * SUPER, SUPER), SUPER)
            for g in range(gp):
                cols = slice(g * LANES, (g + 1) * LANES)
                chains.append(dict(r=r_ref[rows, cols], lw=lw_ref[d, rows, cols], k=kd_ref[d, rows, cols],
                                   v=v_ref[rows, cols], a=-kk_ref[rows, cols], b=bd_ref[d, rows, cols],
                                   s=carry[d * gp + g], masks=masks, fwd=d == 0))
                where.append((d, rows, cols))
        ys, ss = _rw_blocks(chains, passes)
        for (d, rows, cols), y in zip(where, ys):
            y_ref[d, rows, cols] = y
        return tuple(ss)

    s_fin = lax.fori_loop(0, nblk, body, tuple(s0_ref[d, g] for d in range(2) for g in range(gp)))
    for d in range(2):
        for g in range(gp):
            s_ref[d, g] = s_fin[d * gp + g]


def rwkv_scan(r, v, kk, lw, kd, bd, s0_bd, nseq, L, passes, gp):
    ngrp = RW_W // LANES
    w = gp * LANES
    one = pl.BlockSpec((L, w), lambda b, g: (b, g))
    two = pl.BlockSpec((2, L, w), lambda b, g: (0, b, g))
    st = pl.BlockSpec((None, 2, gp, LANES, LANES), lambda b, g: (b, 0, g, 0, 0))
    return pl.pallas_call(
        functools.partial(_rw_scan_kernel, passes=passes, gp=gp),
        grid=(nseq, ngrp // gp),
        in_specs=[one, one, one, two, two, two, st],
        out_specs=[two, st],
        out_shape=[jax.ShapeDtypeStruct((2, nseq * L, RW_W), F32),
                   jax.ShapeDtypeStruct((nseq, 2, ngrp, LANES, LANES), F32)],
        compiler_params=_cparams("arbitrary", "arbitrary"),
    )(r, v, kk, lw, kd, bd, s0_bd)


def _rw_post_kernel(y_ref, bon_ref, g_ref, lnw_ref, lnb_ref, ones_ref, o_ref):
    y = y_ref[0] + y_ref[1]
    ones = ones_ref[...]
    mean = _dot_exact_r(y, ones) * (1.0 / RW_N)
    yc = y - mean
    var = _dot_exact_r(yc * yc, ones) * (1.0 / RW_N)
    yn = yc * lax.rsqrt(var + RW_LN_EPS) * lnw_ref[...] + lnb_ref[...]
    o_ref[...] = ((yn + bon_ref[...]) * g_ref[...]).astype(o_ref.dtype)


def rwkv_post(y2, bonus, g, ln_w, ln_b):
    n = bonus.shape[0]
    tm = 512
    row = pl.BlockSpec((tm, RW_W), lambda i: (i, 0))
    vec = pl.BlockSpec((1, RW_W), lambda i: (0, 0))
    return pl.pallas_call(
        _rw_post_kernel,
        grid=(n // tm,),
        in_specs=[pl.BlockSpec((2, tm, RW_W), lambda i: (0, i, 0)), row, row, vec, vec,
                  pl.BlockSpec((RW_W, RW_W), lambda i: (0, 0))],
        out_specs=row,
        out_shape=jax.ShapeDtypeStruct((n, RW_W), BF16),
        compiler_params=_cparams("arbitrary"),
    )(y2, bonus, g, ln_w.reshape(1, RW_W), ln_b.reshape(1, RW_W),
      jnp.asarray(_block_ones(RW_W, RW_N)).astype(BF16))


def _rw_states_to_blockdiag(s):
    b = s.shape[0]
    s = s.reshape(b, 2, 4, 2, RW_N, RW_N)
    z = jnp.zeros_like(s[:, :, :, 0])
    top = jnp.concatenate([s[:, :, :, 0], z], -1)
    bot = jnp.concatenate([z, s[:, :, :, 1]], -1)
    return jnp.concatenate([top, bot], -2)


def _rw_states_from_blockdiag(s):
    b = s.shape[0]
    return jnp.stack([s[:, :, :, :RW_N, :RW_N], s[:, :, :, RW_N:, RW_N:]], 3).reshape(b, 2, RW_H, RW_N, RW_N)


def _gd_blocks(ch, passes):
    n = SUPER
    c = GD_CHUNK
    nc = n // c
    idx = range(len(ch))
    msk = [x["masks"] for x in ch]
    cum = [_dot_exact_l(x["masks"]["incl_bf"], x["lw"]) for x in ch]
    tot = [_chunk_totals(cum[i], c, ch[i]["fwd"]) for i in idx]
    suf = [tot[i] - cum[i] for i in idx]
    cum_row = [cum[i].T for i in idx]
    gam = [jnp.exp(jnp.where(msk[i]["incl"], cum[i][:, 0:1] - cum_row[i], -jnp.inf)) for i in idx]
    kb = [x["k"] * x["beta"] for x in ch]
    m = [_dot(jnp.concatenate([kb[i], ch[i]["q"]], 0), ch[i]["k"], passes, _NT) for i in idx]
    a = [jnp.where(msk[i]["strict"], m[i][:n] * gam[i], 0.0) for i in idx]
    qk = [jnp.where(msk[i]["incl"], m[i][n:] * gam[i], 0.0) for i in idx]
    tinv = _tri_inv([-x for x in a], msk, passes)
    e_cum = [jnp.exp(cum[i]) for i in idx]
    uw = [_dot(tinv[i], jnp.concatenate([ch[i]["v"] * ch[i]["beta"], kb[i] * e_cum[i]], 1), passes) for i in idx]
    qq = [_dot(qk[i], uw[i], passes) for i in idx]
    u = [uw[i][:, :LANES] for i in idx]
    w = [uw[i][:, LANES:] for i in idx]
    o0 = [qq[i][:, :LANES] for i in idx]
    qhat = [ch[i]["q"] * e_cum[i] - qq[i][:, LANES:] for i in idx]
    kd = [ch[i]["k"] * jnp.exp(suf[i]) for i in idx]
    s = [x["s"] for x in ch]
    os_ = [[None] * nc for _ in ch]
    for step in range(nc):
        ci = [step if x["fwd"] else nc - 1 - step for x in ch]
        sl = [slice(ci[i] * c, (ci[i] + 1) * c) for i in idx]
        xx = [_dot(jnp.concatenate([w[i][sl[i]], qhat[i][sl[i]]], 0), s[i], passes) for i in idx]
        vn = [u[i][sl[i]] - xx[i][:c] for i in idx]
        for i in idx:
            os_[i][ci[i]] = o0[i][sl[i]] + xx[i][c:]
        upd = [_dot(kd[i][sl[i]], vn[i], passes, _TN) for i in idx]
        s = [s[i] * jnp.exp(tot[i][ci[i] * c:ci[i] * c + 1]) + upd[i] for i in idx]
    return [jnp.concatenate(o, 0) for o in os_], s


def _gd_kernel(pq_ref, pk_ref, pv_ref, pz_ref, pab_ref, wq_ref, wk_ref, wv_ref, alog_ref, dtb_ref, ng_ref,
               s0_ref, o_ref, s_ref, q_scr, k_scr, v_scr, gb_scr, o_scr, *, passes, hg):
    L = pq_ref.shape[0]
    h0 = pl.program_id(1) * hg
    l2n = lambda t: t * lax.rsqrt(jnp.sum(t * t, -1, keepdims=True) + 1e-6)
    pab = pab_ref[...]
    lane = lax.broadcasted_iota(jnp.int32, pab.shape, 1)
    g_all = -jnp.exp(alog_ref[...]) * _softplus(pab + dtb_ref[...])
    b_all = _sigmoid(pab)
    for hh in range(hg):
        cols = slice(hh * LANES, (hh + 1) * LANES)
        q_scr[:, cols] = l2n(_silu(_conv3(pq_ref[:, cols], wq_ref[:, cols]))) * (GD_DK ** -0.5)
        k_scr[:, cols] = l2n(_silu(_conv3(pk_ref[:, cols], wk_ref[:, cols])))
        v_scr[:, cols] = _silu(_conv3(pv_ref[:, cols], wv_ref[:, cols]))
        for d in range(2):
            g = jnp.sum(jnp.where(lane == d * GD_H + h0 + hh, g_all, 0.0), -1, keepdims=True)
            bt = jnp.sum(jnp.where(lane == 2 * GD_H + d * GD_H + h0 + hh, b_all, 0.0), -1, keepdims=True)
            gb_scr[d, :, cols] = jnp.broadcast_to(g, (L, LANES))
            gb_scr[2 + d, :, cols] = jnp.broadcast_to(bt, (L, LANES))
    nblk = L // SUPER

    def body(i, carry):
        chains, where = [], []
        for d in range(2):
            masks = _chunk_masks(SUPER, GD_CHUNK, d == 0)
            j = i if d == 0 else nblk - 1 - i
            rows = pl.ds(pl.multiple_of(j * SUPER, SUPER), SUPER)
            for hh in range(hg):
                cols = slice(hh * LANES, (hh + 1) * LANES)
                chains.append(dict(q=q_scr[rows, cols], k=k_scr[rows, cols], v=v_scr[rows, cols],
                                   lw=gb_scr[d, rows, cols], beta=gb_scr[2 + d, rows, cols],
                                   s=carry[d * hg + hh], masks=masks, fwd=d == 0))
                where.append((d, rows, cols))
        os_, ss = _gd_blocks(chains, passes)
        for (d, rows, cols), o in zip(where, os_):
            o_scr[d, rows, cols] = o
        return tuple(ss)

    s_fin = lax.fori_loop(0, nblk, body, tuple(s0_ref[d, hh] for d in range(2) for hh in range(hg)))
    for d in range(2):
        for hh in range(hg):
            s_ref[d, hh] = s_fin[d * hg + hh]
    for hh in range(hg):
        cols = slice(hh * LANES, (hh + 1) * LANES)
        o = o_scr[0, :, cols] + o_scr[1, :, cols]
        o = o * lax.rsqrt(jnp.mean(o * o, -1, keepdims=True) + NORM_EPS) * ng_ref[...]
        o_ref[:, cols] = (o * _silu(pz_ref[:, cols])).astype(o_ref.dtype)


def gdn_mix(p, row0, nseq, L, conv_w, a_log, dt_bias, norm_g, s0, passes, hg):
    b0 = row0 // L
    w = hg * LANES
    nhb = GD_H // hg
    col = lambda part: pl.BlockSpec((L, w), lambda b, h: (b0 + b, part * nhb + h))
    wcol = lambda part: pl.BlockSpec((3, w), lambda b, h: (0, part * nhb + h))
    vec = pl.BlockSpec((1, LANES), lambda b, h: (0, 0))
    st = pl.BlockSpec((None, 2, hg, LANES, LANES), lambda b, h: (b, 0, h, 0, 0))
    alog_row = jnp.zeros((1, LANES), F32).at[0, :2 * GD_H].set(a_log.reshape(-1))
    dtb_row = jnp.zeros((1, LANES), F32).at[0, :2 * GD_H].set(dt_bias.reshape(-1))
    return pl.pallas_call(
        functools.partial(_gd_kernel, passes=passes, hg=hg),
        grid=(nseq, nhb),
        in_specs=[col(0), col(1), col(2), col(3),
                  pl.BlockSpec((L, LANES), lambda b, h: (b0 + b, 4 * GD_H)),
                  wcol(0), wcol(1), wcol(2), vec, vec, vec, st],
        out_specs=[pl.BlockSpec((L, w), lambda b, h: (b, h)), st],
        out_shape=[jax.ShapeDtypeStruct((nseq * L, GD_H * LANES), BF16),
                   jax.ShapeDtypeStruct((nseq, 2, GD_H, LANES, LANES), F32)],
        scratch_shapes=[pltpu.VMEM((L, w), F32), pltpu.VMEM((L, w), F32), pltpu.VMEM((L, w), F32),
                        pltpu.VMEM((4, L, w), F32), pltpu.VMEM((2, L, w), F32)],
        compiler_params=_cparams("arbitrary", "arbitrary"),
    )(p, p, p, p, p, conv_w, conv_w, conv_w, alog_row, dtb_row, norm_g.reshape(1, LANES), s0)


def _router_kernel(x_ref, g_ref, sc_ref, sh_ref, wr_ref, h_ref, gate_ref, idx_ref):
    h = _norm_mod(x_ref[...], g_ref[...], sc_ref[...], sh_ref[...])
    h_ref[...] = h
    logits = _dot(h, wr_ref[...], 6)
    lane = lax.broadcasted_iota(jnp.int32, logits.shape, 1)
    logits = jnp.where(lane < N_EXP, logits, -jnp.inf)
    m1 = jnp.max(logits, -1, keepdims=True)
    i1 = jnp.min(jnp.where(logits == m1, lane, LANES), -1, keepdims=True)
    rest = jnp.where(lane == i1, -jnp.inf, logits)
    m2 = jnp.max(rest, -1, keepdims=True)
    i2 = jnp.min(jnp.where(rest == m2, lane, LANES), -1, keepdims=True)
    e2 = jnp.exp(m2 - m1)
    g1 = 1.0 / (1.0 + e2)
    g2 = e2 / (1.0 + e2)
    gate_ref[...] = jnp.where(lane == 0, g1, 0.0) + jnp.where(lane == 1, g2, 0.0)
    idx_ref[...] = jnp.where(lane == 0, i1, 0) + jnp.where(lane == 1, i2, 0)


def moe_router(x, g, sc_t, sh_t, router):
    tm = 512
    per = TM // tm
    wr = jnp.zeros((D, LANES), F32).at[:, :N_EXP].set(router)
    return pl.pallas_call(
        _router_kernel,
        grid=(T_ALL // tm,),
        in_specs=[pl.BlockSpec((tm, D), lambda i: (i, 0)),
                  pl.BlockSpec((1, D), lambda i: (0, 0)),
                  pl.BlockSpec((None, 1, D), lambda i: (i // per, 0, 0)),
                  pl.BlockSpec((None, 1, D), lambda i: (i // per, 0, 0)),
                  pl.BlockSpec((D, LANES), lambda i: (0, 0))],
        out_specs=[pl.BlockSpec((tm, D), lambda i: (i, 0)), pl.BlockSpec((tm, LANES), lambda i: (i, 0)),
                   pl.BlockSpec((tm, LANES), lambda i: (i, 0))],
        out_shape=[jax.ShapeDtypeStruct((T_ALL, D), F32), jax.ShapeDtypeStruct((T_ALL, LANES), F32),
                   jax.ShapeDtypeStruct((T_ALL, LANES), jnp.int32)],
        compiler_params=_cparams("arbitrary"),
    )(x, g.reshape(1, D), sc_t, sh_t, wr)


def moe_slot_positions(idx):
    e_flat = idx[:, :2].reshape(-1)
    onehot = (e_flat[:, None] == jnp.arange(N_EXP, dtype=jnp.int32)[None, :]).astype(jnp.int32)
    csum = jnp.cumsum(onehot, axis=0)
    rank = jnp.sum(onehot * (csum - 1), axis=1)
    counts = csum[-1]
    gsize = ((counts + MOE_TILE - 1) // MOE_TILE) * MOE_TILE
    gend = jnp.cumsum(gsize)
    pos = jnp.sum(onehot * (gend - gsize)[None, :], axis=1) + rank
    tile_start = jnp.arange(MOE_TILES, dtype=jnp.int32) * MOE_TILE
    tile_expert = jnp.minimum(jnp.sum((gend[None, :] <= tile_start[:, None]).astype(jnp.int32), axis=1), N_EXP - 1)
    return pos.astype(jnp.int32), tile_expert.astype(jnp.int32), (gend[-1:] // MOE_TILE).astype(jnp.int32)


def _row_copy(src, s, dst, d, sem):
    return pltpu.make_async_copy(src.at[pl.ds(s, 1)], dst.at[pl.ds(d, 1)], sem)


def _dispatch_kernel(pos_ref, h_ref, xs_in_ref, xs_ref, sem):
    del xs_in_ref
    tm = h_ref.shape[0]
    base = pl.program_id(0) * tm

    def issue(r, carry):
        for k in range(2):
            _row_copy(h_ref, r, xs_ref, pos_ref[(base + r) * 2 + k], sem).start()
        return carry

    def drain(r, carry):
        for k in range(2):
            _row_copy(h_ref, r, xs_ref, pos_ref[(base + r) * 2 + k], sem).wait()
        return carry

    lax.fori_loop(0, tm, issue, 0)
    lax.fori_loop(0, tm, drain, 0)


def moe_dispatch(h, pos):
    tm = 512
    return pl.pallas_call(
        _dispatch_kernel,
        grid_spec=pltpu.PrefetchScalarGridSpec(
            num_scalar_prefetch=1,
            grid=(T_ALL // tm,),
            in_specs=[pl.BlockSpec((tm, D), lambda i, pos: (i, 0)), pl.BlockSpec(memory_space=pl.ANY)],
            out_specs=pl.BlockSpec(memory_space=pl.ANY),
            scratch_shapes=[pltpu.SemaphoreType.DMA]),
        out_shape=jax.ShapeDtypeStruct((MOE_ROWS, D), F32),
        input_output_aliases={2: 0},
        compiler_params=_cparams("arbitrary"),
    )(pos, h, jnp.zeros((MOE_ROWS, D), F32))


def _expert_kernel(te_ref, nu_ref, xs_ref, wg_ref, wu_ref, wo_ref, o_ref, x_scr, acc):
    i = pl.program_id(0)
    f = pl.program_id(1)
    used = i < nu_ref[0]

    @pl.when(used & (f == 0))
    def _():
        x_scr[...] = xs_ref[...].astype(BF16)
        acc[...] = jnp.zeros_like(acc)

    @pl.when(used)
    def _():
        x = x_scr[...]
        act = _silu(_dg(x, wg_ref[...])) * _dg(x, wu_ref[...])
        acc[...] += _dg(act.astype(BF16), wo_ref[...])

    last = f == pl.num_programs(1) - 1

    @pl.when(used & last)
    def _():
        o_ref[...] = acc[...]

    @pl.when(jnp.logical_not(used) & last)
    def _():
        o_ref[...] = jnp.zeros_like(o_ref)


def moe_experts(xs, tile_expert, n_used, w_in_bf16, w_out_bf16):
    tn = 512
    nf = E_FF // tn
    live = lambda i, f, nu: jnp.where(i < nu[0], f, 0)
    return pl.pallas_call(
        _expert_kernel,
        grid_spec=pltpu.PrefetchScalarGridSpec(
            num_scalar_prefetch=2,
            grid=(MOE_TILES, nf),
            in_specs=[pl.BlockSpec((MOE_TILE, D), lambda i, f, te, nu: (i, 0)),
                      pl.BlockSpec((None, D, tn), lambda i, f, te, nu: (te[i], 0, live(i, f, nu))),
                      pl.BlockSpec((None, D, tn), lambda i, f, te, nu: (te[i], 0, live(i, f, nu) + nf)),
                      pl.BlockSpec((None, tn, D), lambda i, f, te, nu: (te[i], live(i, f, nu), 0))],
            out_specs=pl.BlockSpec((MOE_TILE, D), lambda i, f, te, nu: (i, 0)),
            scratch_shapes=[pltpu.VMEM((MOE_TILE, D), BF16), pltpu.VMEM((MOE_TILE, D), F32)]),
        out_shape=jax.ShapeDtypeStruct((MOE_ROWS, D), F32),
        compiler_params=_cparams("arbitrary", "arbitrary"),
    )(tile_expert, n_used, xs, w_in_bf16, w_in_bf16, w_out_bf16)


def _combine_kernel(pos_ref, ys_ref, gates_ref, x_ref, gate_ref, fg_ref, o_ref, y_scr, sem):
    tm = x_ref.shape[0]
    base = pl.program_id(0) * tm

    def issue(r, carry):
        for k in range(2):
            _row_copy(ys_ref, pos_ref[(base + r) * 2 + k], y_scr.at[k], r, sem).start()
        return carry

    def drain(r, carry):
        for k in range(2):
            _row_copy(ys_ref, pos_ref[(base + r) * 2 + k], y_scr.at[k], r, sem).wait()
        return carry

    lax.fori_loop(0, tm, issue, 0)
    lax.fori_loop(0, tm, drain, 0)
    gates = gates_ref[...]
    lane = lax.broadcasted_iota(jnp.int32, gates.shape, 1)
    g0 = jnp.sum(jnp.where(lane == 0, gates, 0.0), -1, keepdims=True)
    g1 = jnp.sum(jnp.where(lane == 1, gates, 0.0), -1, keepdims=True)
    xn = x_ref[...] + gate_ref[...] * (y_scr[0] * g0 + y_scr[1] * g1)
    y = xn * lax.rsqrt(jnp.mean(xn * xn, -1, keepdims=True) + NORM_EPS)
    o_ref[...] = y * fg_ref[...]


def moe_combine_final(ys, pos, gates, x, gate_t, final_g):
    tm = 256
    per = TM // tm
    return pl.pallas_call(
        _combine_kernel,
        grid_spec=pltpu.PrefetchScalarGridSpec(
            num_scalar_prefetch=1,
            grid=(T_ALL // tm,),
            in_specs=[pl.BlockSpec(memory_space=pl.ANY),
                      pl.BlockSpec((tm, LANES), lambda i, pos: (i, 0)),
                      pl.BlockSpec((tm, D), lambda i, pos: (i, 0)),
                      pl.BlockSpec((None, 1, D), lambda i, pos: (i // per, 0, 0)),
                      pl.BlockSpec((1, D), lambda i, pos: (0, 0))],
            out_specs=pl.BlockSpec((tm, D), lambda i, pos: (i, 0)),
            scratch_shapes=[pltpu.VMEM((2, tm, D), F32), pltpu.SemaphoreType.DMA]),
        out_shape=jax.ShapeDtypeStruct((T_ALL, D), F32),
        compiler_params=_cparams("arbitrary"),
    )(pos, ys, gates, x, gate_t, final_g.reshape(1, D))


def _tile_rows(mod_l, k):
    cols = mod_l[:, k * D:(k + 1) * D]
    ctx = jnp.broadcast_to(cols[0:1], (T_CTX // TM, D))
    den = jnp.repeat(cols[1:1 + B_DEN], L_DEN // TM, axis=0)
    return jnp.concatenate([ctx, den], 0)[:, None, :]


def _pad_cols(w, n):
    return jnp.pad(w, ((0, 0), (0, n - w.shape[1])))


def kernel(x_prompt, x_sample, state_rwkv, state_gdn, c, c_ctx, ada_w, ada_b, norm_mix_g, norm_ffn_g, final_norm_g, e_w_in, e_hy_conv_w, e_hy_conv_b, e_hf_w1, e_hf_b1, e_hf_freq1, e_hf_w2, e_hf_b2, e_hf_freq2, e_hf_w3, e_hy_bias, e_rw_mu, e_rw_w0, e_rw_w2, e_rw_a0, e_rw_a2, e_rw_g2, e_rw_kk, e_rw_ka, e_rw_rk, e_rw_ln_w, e_rw_ln_b, e_w_out, e_ffn_w_in, e_ffn_w_out, o_w_in, o_conv_w, o_A_log, o_dt_bias, o_norm_g, o_w_out, o_router, o_moe_w_in, o_moe_w_out):
    passes = 1
    cond16 = jnp.zeros((16, D), F32).at[0].set(c_ctx).at[1:1 + B_DEN].set(c)
    mod = modulation(cond16, ada_w, ada_b)
    x = assemble_tokens(x_prompt, x_sample)

    m0 = [_tile_rows(mod[0], k) for k in range(6)]
    p = norm_matmul(x, norm_mix_g[0], m0[1], m0[0], _pad_cols(e_w_in[0], P_EVEN_PAD).astype(BF16), 256)
    ys_hy, ys_rw, st_rw = [], [], None
    for row0, nseq, L in ((0, B_CTX, L_CTX), (T_CTX, B_DEN, L_DEN)):
        spec = hyena_filter_spectrum(L, e_hf_w1[0], e_hf_b1[0], e_hf_freq1[0], e_hf_w2[0], e_hf_b2[0],
                                     e_hf_freq2[0], e_hf_w3[0])
        ys_hy.append(hyena_mix(p, row0, nseq, L, e_hy_conv_w[0], e_hy_conv_b[0].reshape(1, -1), spec,
                               e_hy_bias[0], 1))
        r, v, kk, lw, kd, bd, bonus, g = rwkv_prep(p, row0, nseq, L, e_rw_mu[0], e_rw_w0[0], e_rw_w2[0],
                                                   e_rw_a0[0], e_rw_a2[0], e_rw_g2[0], e_rw_kk[0], e_rw_ka[0],
                                                   e_rw_rk[0])
        if row0 == 0:
            s0 = jnp.zeros((nseq, 2, 4, LANES, LANES), F32)
        else:
            s0 = _rw_states_to_blockdiag(state_rwkv[:, 0])
        y2, s_new = rwkv_scan(r, v, kk, lw, kd, bd, s0, nseq, L, passes, 4)
        if row0 == 0:
            st_rw = _rw_states_from_blockdiag(s_new)
        ys_rw.append(rwkv_post(y2, bonus, g, e_rw_ln_w[0], e_rw_ln_b[0]))
    y_mix = jnp.concatenate([jnp.concatenate(ys_hy, 0), jnp.concatenate(ys_rw, 0)], 1)
    x = matmul_residual(y_mix, e_w_out[0].astype(BF16), x, m0[2])
    act = norm_swiglu(x, norm_ffn_g[0], m0[4], m0[3], e_ffn_w_in[0].astype(BF16), D_FF, 256)
    x = matmul_residual(act, e_ffn_w_out[0].astype(BF16), x, m0[5])

    m1 = [_tile_rows(mod[1], k) for k in range(6)]
    p = norm_matmul(x, norm_mix_g[1], m1[1], m1[0], _pad_cols(o_w_in[0], P_ODD_PAD).astype(BF16), 384)
    os_, st_gd = [], None
    for row0, nseq, L in ((0, B_CTX, L_CTX), (T_CTX, B_DEN, L_DEN)):
        if row0 == 0:
            s0 = jnp.zeros((nseq, 2, GD_H, LANES, LANES), F32)
        else:
            s0 = state_gdn[:, 0]
        o, s_new = gdn_mix(p, row0, nseq, L, o_conv_w[0], o_A_log[0], o_dt_bias[0], o_norm_g[0], s0, passes,
                           8 if L == L_CTX else 4)
        if row0 == 0:
            st_gd = s_new
        os_.append(o)
    x = matmul_residual(jnp.concatenate(os_, 0), o_w_out[0].astype(BF16), x, m1[2])
    h, gates, idx = moe_router(x, norm_ffn_g[1], m1[4], m1[3], o_router[0])
    pos, tile_expert, n_used = moe_slot_positions(idx)
    ys = moe_experts(moe_dispatch(h, pos), tile_expert, n_used, o_moe_w_in[0].astype(BF16),
                     o_moe_w_out[0].astype(BF16))
    y = moe_combine_final(ys, pos, gates, x, m1[5], final_norm_g)

    y_prompt = y[:T_CTX].reshape(B_CTX, L_CTX, D)
    y_sample = y[T_CTX:].reshape(B_DEN, L_DEN, D)
    return (y_prompt, y_sample, st_rw[:, None], st_gd[:, None])
```

```python
import functools
import math

import numpy as np
import jax
import jax.numpy as jnp
from jax import lax
from jax.experimental import pallas as pl
from jax.experimental.pallas import tpu as pltpu

F32 = jnp.float32
BF16 = jnp.bfloat16

D = 1024
B_CTX, L_CTX = 32, 256
B_DEN, L_DEN = 8, 1024
T_CTX = B_CTX * L_CTX
T_DEN = B_DEN * L_DEN
T_ALL = T_CTX + T_DEN
GRID_W = 64
NORM_EPS = 1e-6

HY_W = 512
HY_EMB = 33
HY_BANDS = 16
HY_FFN = 64
HY_TARGET, HY_FAST, HY_SLOW = 1e-2, 0.3, 1.5

RW_W = 512
RW_N = 64
RW_H = 8
RW_LORA = 224
RW_LN_EPS = 64e-5
P_EVEN = 3 * HY_W + 3 * RW_W + RW_LORA
P_EVEN_PAD = 3328

GD_H = 8
GD_DK = 128
GD_QKV = 3072
P_ODD = 4128
P_ODD_PAD = 4224

D_FF = 2816
N_EXP = 8
E_FF = 3584
MOE_TILE = 1024
MOE_ROWS = 2 * T_ALL + N_EXP * MOE_TILE
MOE_TILES = MOE_ROWS // MOE_TILE

LANES = 128
TM = 1024
RW_CHUNK = 32
GD_CHUNK = 64
SUPER = 128
VMEM_LIMIT = 56 * 1024 * 1024

_NN = (((1,), (0,)), ((), ()))
_NT = (((1,), (1,)), ((), ()))
_TN = (((0,), (0,)), ((), ()))


def _cparams(*sem):
    return pltpu.CompilerParams(dimension_semantics=sem, vmem_limit_bytes=VMEM_LIMIT)


def _dg(a, b, dims=_NN):
    return lax.dot_general(a, b, dims, preferred_element_type=F32)


def _split2(x):
    hi = x.astype(BF16)
    lo = (x - hi.astype(F32)).astype(BF16)
    return hi, lo


def _split3(x):
    x0 = x.astype(BF16)
    r1 = x - x0.astype(F32)
    x1 = r1.astype(BF16)
    x2 = (r1 - x1.astype(F32)).astype(BF16)
    return x0, x1, x2


def _dot(a, b, passes=1, dims=_NN):
    if passes == 1:
        return _dg(a.astype(BF16), b.astype(BF16), dims)
    if passes == 3:
        ah, al = _split2(a)
        bh, bl = _split2(b)
        return _dg(ah, bh, dims) + (_dg(ah, bl, dims) + _dg(al, bh, dims))
    a0, a1, a2 = _split3(a)
    b0, b1, b2 = _split3(b)
    small = _dg(a0, b2, dims) + _dg(a1, b1, dims) + _dg(a2, b0, dims)
    mid = _dg(a0, b1, dims) + _dg(a1, b0, dims)
    return _dg(a0, b0, dims) + (mid + small)


def _dot_exact_l(m, x, dims=_NN):
    x0, x1, x2 = _split3(x)
    return _dg(m, x0, dims) + (_dg(m, x1, dims) + _dg(m, x2, dims))


def _dot_exact_r(x, m, dims=_NN):
    x0, x1, x2 = _split3(x)
    return _dg(x0, m, dims) + (_dg(x1, m, dims) + _dg(x2, m, dims))


def _sigmoid(x):
    return 1.0 / (1.0 + jnp.exp(-x))


def _silu(x):
    return x * _sigmoid(x)


def _softplus(x):
    return jnp.maximum(x, 0.0) + jnp.log(1.0 + jnp.exp(-jnp.abs(x)))


def _shift_rows(x):
    n = x.shape[0]
    row = lax.broadcasted_iota(jnp.int32, x.shape, 0)
    prev = jnp.where(row == 0, 0.0, pltpu.roll(x, 1, 0))
    nxt = jnp.where(row == n - 1, 0.0, pltpu.roll(x, n - 1, 0))
    return prev, nxt


def _conv3(x, w, b=None):
    prev, nxt = _shift_rows(x)
    y = prev * w[0:1] + x * w[1:2] + nxt * w[2:3]
    return y if b is None else y + b


def _chunk_masks(n, chunk, fwd):
    row = lax.broadcasted_iota(jnp.int32, (n, n), 0)
    col = lax.broadcasted_iota(jnp.int32, (n, n), 1)
    sh = int(math.log2(chunk))
    same = (row >> sh) == (col >> sh)
    before = (col < row) if fwd else (col > row)
    pair = (row >> 1) == (col >> 1)
    joins = [((row >> (lvl + 1)) == (col >> (lvl + 1))) & ((row >> lvl) != (col >> lvl))
             for lvl in range(1, sh)]
    incl = same & (before | (row == col))
    return dict(strict=same & before, incl=incl, eye=jnp.where(row == col, 1.0, 0.0), pair=pair, joins=joins,
                incl_bf=_mask_bf16(incl))


def _mask_bf16(m):
    return jnp.where(m, 1.0, 0.0).astype(BF16)


def _chunk_totals(cum, chunk, fwd):
    n = cum.shape[0]
    rows = [cum[(ci + 1) * chunk - 1:(ci + 1) * chunk] if fwd else cum[ci * chunk:ci * chunk + 1]
            for ci in range(n // chunk)]
    return jnp.concatenate([jnp.broadcast_to(r, (chunk, cum.shape[1])) for r in rows], 0)


def _tri_inv(xs, masks, passes):
    ts = [m["eye"] + jnp.where(m["pair"], x, 0.0) for x, m in zip(xs, masks)]
    for lvl in range(len(masks[0]["joins"])):
        ps = [_dot(jnp.where(m["joins"][lvl], x, 0.0), t, passes) for x, m, t in zip(xs, masks, ts)]
        ts = [t + _dot(t, p, passes) for t, p in zip(ts, ps)]
    return ts


@functools.lru_cache(maxsize=None)
def _pos_table():
    t = np.arange(L_DEN)
    row = (t // GRID_W).astype(np.float32)
    col = (t % GRID_W).astype(np.float32)
    q = D // 4
    omega = np.exp(-math.log(10000.0) * np.arange(q, dtype=np.float32) / q).astype(np.float32)
    enc = lambda pos: np.concatenate([np.sin(pos[:, None] * omega), np.cos(pos[:, None] * omega)], -1)
    return np.concatenate([enc(row), enc(col)], -1).astype(np.float32)


@functools.lru_cache(maxsize=None)
def _dft_tables(L):
    f = np.arange(L, dtype=np.int64)
    m = (f[:, None] * f[None, :]) % (2 * L)
    ang = np.pi * m.astype(np.float64) / L
    return np.cos(ang).astype(np.float32), np.sin(ang).astype(np.float32)


@functools.lru_cache(maxsize=None)
def _hyena_static(L):
    k = np.arange(L, dtype=np.float32)
    t = k / np.float32(L - 1)
    bands = np.linspace(1e-4, HY_BANDS - 1, HY_BANDS, dtype=np.float32)
    ang = (np.float32(2.0 * math.pi) * k / np.float32(L))[:, None] * bands[None, :]
    feats = np.concatenate([t[:, None], np.cos(ang), -np.sin(ang)], -1).astype(np.float32)
    feats_p = np.zeros((L, LANES), np.float32)
    feats_p[:, :HY_EMB] = feats
    deltas = np.abs(np.linspace(math.log(HY_TARGET) / HY_FAST, math.log(HY_TARGET) / HY_SLOW, HY_W,
                                dtype=np.float32))
    window = np.exp(-t[:, None] * deltas[None, :]).astype(np.float32)
    return feats_p, window


def _block_ones(n, blk):
    i = np.arange(n) // blk
    return (i[:, None] == i[None, :]).astype(np.float32)


def _mod_kernel(c_ref, w_ref, b_ref, o_ref):
    o_ref[...] = _dot(_silu(c_ref[...]), w_ref[...], 6) + b_ref[...]


def modulation(cond16, ada_w, ada_b):
    depth = ada_w.shape[0]
    tn = 1024
    return pl.pallas_call(
        _mod_kernel,
        grid=(depth, 6 * D // tn),
        in_specs=[pl.BlockSpec((16, D), lambda i, j: (0, 0)),
                  pl.BlockSpec((None, D, tn), lambda i, j: (i, 0, j)),
                  pl.BlockSpec((None, 1, tn), lambda i, j: (i, 0, j))],
        out_specs=pl.BlockSpec((None, 16, tn), lambda i, j: (i, 0, j)),
        out_shape=jax.ShapeDtypeStruct((depth, 16, 6 * D), F32),
        compiler_params=_cparams("arbitrary", "arbitrary"),
    )(cond16, ada_w, ada_b.reshape(depth, 1, 6 * D))


def _assemble_kernel(xp_ref, xs_ref, pos_ref, o_ref):
    i = pl.program_id(0)

    @pl.when(i < T_CTX // 256)
    def _():
        o_ref[...] = xp_ref[...]

    @pl.when(i >= T_CTX // 256)
    def _():
        o_ref[...] = xs_ref[...] + pos_ref[...]


def assemble_tokens(x_prompt, x_sample):
    nc = T_CTX // 256
    pos = jnp.asarray(_pos_table())
    return pl.pallas_call(
        _assemble_kernel,
        grid=(T_ALL // 256,),
        in_specs=[pl.BlockSpec((256, D), lambda i: (jnp.minimum(i, nc - 1), 0)),
                  pl.BlockSpec((256, D), lambda i: (jnp.maximum(i - nc, 0), 0)),
                  pl.BlockSpec((256, D), lambda i: (jnp.maximum(i - nc, 0) % (L_DEN // 256), 0))],
        out_specs=pl.BlockSpec((256, D), lambda i: (i, 0)),
        out_shape=jax.ShapeDtypeStruct((T_ALL, D), F32),
        compiler_params=_cparams("arbitrary"),
    )(x_prompt.reshape(T_CTX, D), x_sample.reshape(T_DEN, D), pos)


def _norm_mod(x, g, sc, sh):
    y = x * lax.rsqrt(jnp.mean(x * x, -1, keepdims=True) + NORM_EPS)
    return (y * g) * (1.0 + sc) + sh


def _norm_mm_kernel(x_ref, g_ref, sc_ref, sh_ref, w_ref, o_ref, h_scr):
    @pl.when(pl.program_id(1) == 0)
    def _():
        h_scr[...] = _norm_mod(x_ref[...], g_ref[...], sc_ref[...], sh_ref[...]).astype(BF16)

    o_ref[...] = _dg(h_scr[...], w_ref[...]).astype(o_ref.dtype)


def norm_matmul(x, g, sc_t, sh_t, w_bf16, tn, out_dtype=F32):
    n = w_bf16.shape[1]
    return pl.pallas_call(
        _norm_mm_kernel,
        grid=(T_ALL // TM, n // tn),
        in_specs=[pl.BlockSpec((TM, D), lambda i, j: (i, 0)),
                  pl.BlockSpec((1, D), lambda i, j: (0, 0)),
                  pl.BlockSpec((None, 1, D), lambda i, j: (i, 0, 0)),
                  pl.BlockSpec((None, 1, D), lambda i, j: (i, 0, 0)),
                  pl.BlockSpec((D, tn), lambda i, j: (0, j))],
        out_specs=pl.BlockSpec((TM, tn), lambda i, j: (i, j)),
        out_shape=jax.ShapeDtypeStruct((T_ALL, n), out_dtype),
        scratch_shapes=[pltpu.VMEM((TM, D), BF16)],
        compiler_params=_cparams("arbitrary", "arbitrary"),
    )(x, g.reshape(1, D), sc_t, sh_t, w_bf16)


def _norm_swiglu_kernel(x_ref, g_ref, sc_ref, sh_ref, wg_ref, wu_ref, o_ref, h_scr):
    @pl.when(pl.program_id(1) == 0)
    def _():
        h_scr[...] = _norm_mod(x_ref[...], g_ref[...], sc_ref[...], sh_ref[...]).astype(BF16)

    h = h_scr[...]
    o_ref[...] = (_silu(_dg(h, wg_ref[...])) * _dg(h, wu_ref[...])).astype(o_ref.dtype)


def norm_swiglu(x, g, sc_t, sh_t, w_in_bf16, dff, tn):
    nj = dff // tn
    return pl.pallas_call(
        _norm_swiglu_kernel,
        grid=(T_ALL // TM, nj),
        in_specs=[pl.BlockSpec((TM, D), lambda i, j: (i, 0)),
                  pl.BlockSpec((1, D), lambda i, j: (0, 0)),
                  pl.BlockSpec((None, 1, D), lambda i, j: (i, 0, 0)),
                  pl.BlockSpec((None, 1, D), lambda i, j: (i, 0, 0)),
                  pl.BlockSpec((D, tn), lambda i, j: (0, j)),
                  pl.BlockSpec((D, tn), lambda i, j: (0, j + nj))],
        out_specs=pl.BlockSpec((TM, tn), lambda i, j: (i, j)),
        out_shape=jax.ShapeDtypeStruct((T_ALL, dff), BF16),
        scratch_shapes=[pltpu.VMEM((TM, D), BF16)],
        compiler_params=_cparams("arbitrary", "arbitrary"),
    )(x, g.reshape(1, D), sc_t, sh_t, w_in_bf16, w_in_bf16)


def _mm_res_kernel(y_ref, w_ref, x_ref, gate_ref, o_ref):
    o_ref[...] = x_ref[...] + gate_ref[...] * _dg(y_ref[...], w_ref[...])


def matmul_residual(y_bf16, w_bf16, x, gate_t, tn):
    k = y_bf16.shape[1]
    return pl.pallas_call(
        _mm_res_kernel,
        grid=(T_ALL // TM, D // tn),
        in_specs=[pl.BlockSpec((TM, k), lambda i, j: (i, 0)),
                  pl.BlockSpec((k, tn), lambda i, j: (0, j)),
                  pl.BlockSpec((TM, tn), lambda i, j: (i, j)),
                  pl.BlockSpec((None, 1, tn), lambda i, j: (i, 0, j))],
        out_specs=pl.BlockSpec((TM, tn), lambda i, j: (i, j)),
        out_shape=jax.ShapeDtypeStruct((T_ALL, D), F32),
        compiler_params=_cparams("arbitrary", "arbitrary"),
    )(y_bf16, w_bf16, x, gate_t)


def _hy_filter_kernel(feat_ref, w1_ref, b1_ref, f1_ref, w2_ref, b2_ref, f2_ref, w3f_ref, w3b_ref, win_ref,
                      c_ref, s_ref, hr_ref, hi_ref, hn_ref):
    L = feat_ref.shape[0]
    h = jnp.sin(f1_ref[...] * (_dot(feat_ref[...], w1_ref[...], 6) + b1_ref[...]))
    h = jnp.sin(f2_ref[...] * (_dot(h, w2_ref[...], 6) + b2_ref[...]))
    win = win_ref[...]
    fw = _dot(h, w3f_ref[...], 6) * win
    bw = _dot(h, w3b_ref[...], 6) * win
    row = lax.broadcasted_iota(jnp.int32, fw.shape, 0)
    bw = jnp.where(row == 0, 0.0, bw)
    nrm = jnp.sum(jnp.abs(fw), 0, keepdims=True) + jnp.sum(jnp.abs(bw), 0, keepdims=True)
    ev = (fw + bw) / nrm
    od = (bw - fw) / nrm
    alt = (1 - 2 * (row & 1)).astype(F32)
    hr_ref[...] = _dot(c_ref[...], ev, 6)
    hi_ref[...] = _dot(s_ref[...], od, 6)
    hn_ref[...] = jnp.broadcast_to(jnp.sum(ev * alt, 0, keepdims=True), (8, ev.shape[1]))
    del L


def hyena_filter_spectrum(L, w1, b1, f1, w2, b2, f2, w3):
    feats, window = _hyena_static(L)
    cos_t, sin_t = _dft_tables(L)
    tc = 128
    ncb = HY_W // tc
    w1p = jnp.zeros((LANES, HY_FFN), F32).at[:HY_EMB].set(w1)
    const = lambda shape: pl.BlockSpec(shape, lambda o, c: (0,) * len(shape))
    return pl.pallas_call(
        _hy_filter_kernel,
        grid=(2, ncb),
        in_specs=[const((L, LANES)), const((LANES, HY_FFN)), const((1, HY_FFN)), const((1, HY_FFN)),
                  const((HY_FFN, HY_FFN)), const((1, HY_FFN)), const((1, HY_FFN)),
                  pl.BlockSpec((HY_FFN, tc), lambda o, c: (0, o * ncb + c)),
                  pl.BlockSpec((HY_FFN, tc), lambda o, c: (0, 2 * ncb + o * ncb + c)),
                  pl.BlockSpec((L, tc), lambda o, c: (0, c)),
                  const((L, L)), const((L, L))],
        out_specs=[pl.BlockSpec((None, L, tc), lambda o, c: (o, 0, c)),
                   pl.BlockSpec((None, L, tc), lambda o, c: (o, 0, c)),
                   pl.BlockSpec((None, 8, tc), lambda o, c: (o, 0, c))],
        out_shape=[jax.ShapeDtypeStruct((2, L, HY_W), F32), jax.ShapeDtypeStruct((2, L, HY_W), F32),
                   jax.ShapeDtypeStruct((2, 8, HY_W), F32)],
        compiler_params=_cparams("arbitrary", "arbitrary"),
    )(jnp.asarray(feats), w1p, b1.reshape(1, -1), f1.reshape(1, -1), w2, b2.reshape(1, -1), f2.reshape(1, -1),
      w3, w3, jnp.asarray(window), jnp.asarray(cos_t), jnp.asarray(sin_t))


def _hy_mix_kernel(pv_ref, p1_ref, p2_ref, wv_ref, w1_ref, w2_ref, bv_ref, b1_ref, b2_ref,
                   hr_ref, hi_ref, hn_ref, bias_ref, ch_ref, cl_ref, sh_ref, sl_ref, o_ref, *, passes):
    L = pv_ref.shape[0]
    z = _conv3(pv_ref[...], wv_ref[...], bv_ref[...])
    gates = (_conv3(p1_ref[...], w1_ref[...], b1_ref[...]), _conv3(p2_ref[...], w2_ref[...], b2_ref[...]))
    row = lax.broadcasted_iota(jnp.int32, z.shape, 0)
    alt = (1 - 2 * (row & 1)).astype(F32)
    ch, cl, sh, sl = ch_ref[...], cl_ref[...], sh_ref[...], sl_ref[...]

    def tdot(th, tl, x):
        if passes == 1:
            return _dg(th, x.astype(BF16))
        xh, xl = _split2(x)
        return _dg(th, xh) + (_dg(th, xl) + _dg(tl, xh))

    inv_l = 1.0 / L
    for o in range(2):
        hr, hi, hn = hr_ref[o], hi_ref[o], hn_ref[o][0:1]
        zc = tdot(ch, cl, z)
        zs = tdot(sh, sl, z)
        zn = jnp.sum(z * alt, 0, keepdims=True)
        yr = zc * hr + zs * hi
        yi = zc * hi - zs * hr
        wr = jnp.where(row == 0, 0.5 * inv_l, inv_l)
        conv = tdot(ch, cl, yr * wr) - tdot(sh, sl, yi * inv_l) + alt * (zn * hn * (0.5 * inv_l))
        z = gates[o] * (conv + z * bias_ref[o:o + 1])
    o_ref[...] = z.astype(o_ref.dtype)


def hyena_mix(p, row0, nseq, L, conv_w, conv_b, spec, bias, passes):
    hr, hi, hn = spec
    cos_t, sin_t = _dft_tables(L)
    ch, cl = _np_split2(cos_t)
    sh, sl = _np_split2(sin_t)
    tc = 512 if L <= 256 else 256
    ncb = HY_W // tc
    b0 = row0 // L
    pspec = lambda part: pl.BlockSpec((L, tc), lambda b, c: (b0 + b, part * ncb + c))
    wspec = lambda part: pl.BlockSpec((3, tc), lambda b, c: (0, part * ncb + c))
    bspec = lambda part: pl.BlockSpec((1, tc), lambda b, c: (0, part * ncb + c))
    hspec = lambda rows: pl.BlockSpec((2, rows, tc), lambda b, c: (0, 0, c))
    tab = pl.BlockSpec((L, L), lambda b, c: (0, 0))
    return pl.pallas_call(
        functools.partial(_hy_mix_kernel, passes=passes),
        grid=(nseq, ncb),
        in_specs=[pspec(0), pspec(1), pspec(2), wspec(0), wspec(1), wspec(2), bspec(0), bspec(1), bspec(2),
                  hspec(L), hspec(L), hspec(8), pl.BlockSpec((2, tc), lambda b, c: (0, c)), tab, tab, tab, tab],
        out_specs=pl.BlockSpec((L, tc), lambda b, c: (b, c)),
        out_shape=jax.ShapeDtypeStruct((nseq * L, HY_W), BF16),
        compiler_params=_cparams("arbitrary", "arbitrary"),
    )(p, p, p, conv_w, conv_w, conv_w, conv_b, conv_b, conv_b, hr, hi, hn, bias,
      jnp.asarray(ch), jnp.asarray(cl), jnp.asarray(sh), jnp.asarray(sl))


def _np_split2(x):
    hi = x.astype(jnp.bfloat16)
    lo = (x - hi.astype(np.float32)).astype(jnp.bfloat16)
    return hi, lo


def _rw_prep_kernel(pr_ref, pk_ref, pv_ref, pl_ref, mur_ref, muk_ref, muv_ref, mul_ref, w0_ref, a0_ref,
                    w2_ref, a2_ref, g2_ref, kkw_ref, kaw_ref, rkw_ref, ones_ref,
                    r_ref, v_ref, kk_ref, lw_ref, kd_ref, bd_ref, bon_ref, g_ref):
    def shift(p, mu):
        prev, nxt = _shift_rows(p)
        return p + (0.5 * (prev + nxt) - p) * mu

    r = shift(pr_ref[...], mur_ref[...])
    k = shift(pk_ref[...], muk_ref[...])
    v = shift(pv_ref[...], muv_ref[...])
    lo = shift(pl_ref[...], mul_ref[...])
    ones = ones_ref[...]
    g_ref[...] = _dot(_sigmoid(lo), g2_ref[...], 1)
    kkr = k * kkw_ref[...]
    kk = kkr / jnp.maximum(jnp.sqrt(_dot_exact_r(kkr * kkr, ones)), 1e-12)
    th = jnp.tanh(lo)
    bon = jnp.zeros_like(r)
    for d in range(2):
        w = -_softplus(-(w0_ref[d:d + 1] + _dot(th, w2_ref[d], 3))) - 0.5
        lw_ref[d] = -jnp.exp(w)
        a = _sigmoid(a0_ref[d:d + 1] + _dot(lo, a2_ref[d], 1))
        kd = k * (1.0 + (a - 1.0) * kaw_ref[...])
        kd_ref[d] = kd
        bd_ref[d] = kk * a
        bon = bon + _dot_exact_r(r * kd * rkw_ref[...], ones) * v
    r_ref[...] = r
    v_ref[...] = v
    kk_ref[...] = kk
    bon_ref[...] = bon


def rwkv_prep(p, row0, nseq, L, mu, w0, w2, a0, a2, g2, k_k, k_a, r_k):
    b0 = row0 // L
    cb = LANES
    ncb = RW_W // cb
    c0 = 3 * HY_W // cb
    wide = lambda part: pl.BlockSpec((L, cb), lambda b, c: (b0 + b, c0 + part * ncb + c))
    lora = pl.BlockSpec((L, 256), lambda b, c: (b0 + b, (3 * HY_W + 3 * RW_W) // 256))
    muw = lambda part: pl.BlockSpec((1, cb), lambda b, c: (0, part * ncb + c))
    vec = lambda rows: pl.BlockSpec((rows, cb), lambda b, c: (0, c))
    mu_p = jnp.zeros((1, 3 * RW_W + 256), F32).at[0, :3 * RW_W + RW_LORA].set(mu)
    w2f = jnp.zeros((2, 256, RW_W), F32).at[0, 0:32].set(w2[0]).at[1, 32:64].set(w2[1])
    a2f = jnp.zeros((2, 256, RW_W), F32).at[0, 64:96].set(a2[0]).at[1, 96:128].set(a2[1])
    g2f = jnp.zeros((256, RW_W), F32).at[128:224].set(g2)
    n = nseq * L
    one = jax.ShapeDtypeStruct((n, RW_W), F32)
    two = jax.ShapeDtypeStruct((2, n, RW_W), F32)
    ospec1 = pl.BlockSpec((L, cb), lambda b, c: (b, c))
    ospec2 = pl.BlockSpec((2, L, cb), lambda b, c: (0, b, c))
    return pl.pallas_call(
        _rw_prep_kernel,
        grid=(nseq, ncb),
        in_specs=[wide(0), wide(1), wide(2), lora, muw(0), muw(1), muw(2),
                  pl.BlockSpec((1, 256), lambda b, c: (0, 3 * RW_W // 256)),
                  vec(2), vec(2),
                  pl.BlockSpec((2, 256, cb), lambda b, c: (0, 0, c)),
                  pl.BlockSpec((2, 256, cb), lambda b, c: (0, 0, c)),
                  pl.BlockSpec((256, cb), lambda b, c: (0, c)),
                  vec(1), vec(1), vec(1),
                  pl.BlockSpec((cb, cb), lambda b, c: (0, 0))],
        out_specs=[ospec1, ospec1, ospec1, ospec2, ospec2, ospec2, ospec1, ospec1],
        out_shape=[one, one, one, two, two, two, one, one],
        compiler_params=_cparams("arbitrary", "arbitrary"),
    )(p, p, p, p, mu_p, mu_p, mu_p, mu_p, w0, a0, w2f, a2f, g2f, k_k.reshape(1, RW_W), k_a.reshape(1, RW_W),
      r_k.reshape(1, RW_W), jnp.asarray(_block_ones(cb, RW_N)).astype(BF16))


def _rw_blocks(ch, passes):
    n = SUPER
    c = RW_CHUNK
    nc = n // c
    idx = range(len(ch))
    cum = [_dot_exact_l(x["masks"]["incl_bf"], x["lw"]) for x in ch]
    tot = [_chunk_totals(cum[i], c, ch[i]["fwd"]) for i in idx]
    suf = [tot[i] - cum[i] for i in idx]
    e_neg =[jnp.exp(-cum[i]) for i in idx]
    at = [ch[i]["a"] * jnp.exp(cum[i] - ch[i]["lw"]) for i in idx]
    rt = [ch[i]["r"] * jnp.exp(cum[i]) for i in idx]
    bk = [jnp.concatenate([ch[i]["b"] * e_neg[i], ch[i]["k"] * e_neg[i]], 0) for i in idx]
    e_suf = [jnp.exp(suf[i]) for i in idx]
    bp = [ch[i]["b"] * e_suf[i] for i in idx]
    kp = [ch[i]["k"] * e_suf[i] for i in idx]
    lane = lax.broadcasted_iota(jnp.int32, (1, LANES), 1)
    heads = range(LANES // RW_N)
    sub = [(i, g) for i in idx for g in heads]
    mg = [(lane >> 6) == g for g in heads]
    at_g = [jnp.where(mg[g], at[i], 0.0) for i, g in sub]
    v_g = [jnp.where(mg[g], ch[i]["v"], 0.0) for i, g in sub]
    m = [_dot(jnp.concatenate([at_g[j], jnp.where(mg[g], rt[i], 0.0)], 0), bk[i], passes, _NT)
         for j, (i, g) in enumerate(sub)]
    smask = [ch[i]["masks"] for i, g in sub]
    ab = [jnp.where(smask[j]["strict"], m[j][:n, :n], 0.0) for j in range(len(sub))]
    ak = [jnp.where(smask[j]["strict"], m[j][:n, n:], 0.0) for j in range(len(sub))]
    rb = [jnp.where(smask[j]["incl"], m[j][n:, :n], 0.0) for j in range(len(sub))]
    rk = [jnp.where(smask[j]["incl"], m[j][n:, n:], 0.0) for j in range(len(sub))]
    tinv = _tri_inv(ab, smask, passes)
    akv = [_dot(ak[j], v_g[j], passes) for j in range(len(sub))]
    aw = [_dot(tinv[j], jnp.concatenate([at_g[j], akv[j]], 1), passes) for j in range(len(sub))]
    ry = [_dot(rb[j], aw[j], passes) for j in range(len(sub))]
    rkv = [_dot(rk[j], v_g[j], passes) for j in range(len(sub))]
    nh = len(heads)
    ahat = [sum(aw[i * nh + g][:, :LANES] for g in heads) for i in idx]
    w1 = [sum(aw[i * nh + g][:, LANES:] for g in heads) for i in idx]
    rhat = [rt[i] + sum(ry[i * nh + g][:, :LANES] for g in heads) for i in idx]
    y0 = [sum(ry[i * nh + g][:, LANES:] + rkv[i * nh + g] for g in heads) for i in idx]
    rowl = lax.broadcasted_iota(jnp.int32, (LANES, LANES), 0)
    coll = lax.broadcasted_iota(jnp.int32, (LANES, LANES), 1)
    diag_blocks = (rowl >> 6) == (coll >> 6)
    s = [x["s"] for x in ch]
    ys = [[None] * nc for _ in ch]
    for step in range(nc):
        ci = [step if x["fwd"] else nc - 1 - step for x in ch]
        sl = [slice(ci[i] * c, (ci[i] + 1) * c) for i in idx]
        xx = [_dot(jnp.concatenate([ahat[i][sl[i]], rhat[i][sl[i]]], 0), s[i], passes, _NT) for i in idx]
        u = [w1[i][sl[i]] + xx[i][:c] for i in idx]
        for i in idx:
            ys[i][ci[i]] = y0[i][sl[i]] + xx[i][c:]
        upd = [_dot(jnp.concatenate([u[i], ch[i]["v"][sl[i]]], 0),
                    jnp.concatenate([bp[i][sl[i]], kp[i][sl[i]]], 0), passes, _TN) for i in idx]
        s = [s[i] * jnp.exp(tot[i][ci[i] * c:ci[i] * c + 1]) + jnp.where(diag_blocks, upd[i], 0.0) for i in idx]
    return [jnp.concatenate(y, 0) for y in ys], s


def _rw_scan_kernel(r_ref, v_ref, kk_ref, lw_ref, kd_ref, bd_ref, s0_ref, y_ref, s_ref, *, passes, gp):
    L = r_ref.shape[0]
    nblk = L // SUPER

    def body(i, carry):
        chains, where = [], []
        for d in range(2):
            masks = _chunk_masks(SUPER, RW_CHUNK, d == 0)
            j = i if d == 0 else nblk - 1 - i
            rows = pl.ds(pl.multiple_of(j * SUPER, SUPER), SUPER)
            for g in range(gp):
                cols = slice(g * LANES, (g + 1) * LANES)
                chains.append(dict(r=r_ref[rows, cols], lw=lw_ref[d, rows, cols], k=kd_ref[d, rows, cols],
                                   v=v_ref[rows, cols], a=-kk_ref[rows, cols], b=bd_ref[d, rows, cols],
                                   s=carry[d * gp + g], masks=masks, fwd=d == 0))
                where.append((d, rows, cols))
        ys, ss = _rw_blocks(chains, passes)
        for (d, rows, cols), y in zip(where, ys):
            y_ref[d, rows, cols] = y
        return tuple(ss)

    s_fin = lax.fori_loop(0, nblk, body, tuple(s0_ref[d, g] for d in range(2) for g in range(gp)))
    for d in range(2):
        for g in range(gp):
            s_ref[d, g] = s_fin[d * gp + g]


def rwkv_scan(r, v, kk, lw, kd, bd, s0_bd, nseq, L, passes, gp):
    ngrp = RW_W // LANES
    w = gp * LANES
    one = pl.BlockSpec((L, w), lambda b, g: (b, g))
    two = pl.BlockSpec((2, L, w), lambda b, g: (0, b, g))
    st = pl.BlockSpec((None, 2, gp, LANES, LANES), lambda b, g: (b, 0, g, 0, 0))
    return pl.pallas_call(
        functools.partial(_rw_scan_kernel, passes=passes, gp=gp),
        grid=(nseq, ngrp // gp),
        in_specs=[one, one, one, two, two, two, st],
        out_specs=[two, st],
        out_shape=[jax.ShapeDtypeStruct((2, nseq * L, RW_W), F32),
                   jax.ShapeDtypeStruct((nseq, 2, ngrp, LANES, LANES), F32)],
        compiler_params=_cparams("arbitrary", "arbitrary"),
    )(r, v, kk, lw, kd, bd, s0_bd)


def _rw_post_kernel(y_ref, bon_ref, g_ref, lnw_ref, lnb_ref, ones_ref, o_ref):
    y = y_ref[0] + y_ref[1]
    ones = ones_ref[...]
    mean = _dot_exact_r(y, ones) * (1.0 / RW_N)
    yc = y - mean
    var = _dot_exact_r(yc * yc, ones) * (1.0 / RW_N)
    yn = yc * lax.rsqrt(var + RW_LN_EPS) * lnw_ref[...] + lnb_ref[...]
    o_ref[...] = ((yn + bon_ref[...]) * g_ref[...]).astype(o_ref.dtype)


def rwkv_post(y2, bonus, g, ln_w, ln_b):
    n = bonus.shape[0]
    tm = 512
    row = pl.BlockSpec((tm, RW_W), lambda i: (i, 0))
    vec = pl.BlockSpec((1, RW_W), lambda i: (0, 0))
    return pl.pallas_call(
        _rw_post_kernel,
        grid=(n // tm,),
        in_specs=[pl.BlockSpec((2, tm, RW_W), lambda i: (0, i, 0)), row, row, vec, vec,
                  pl.BlockSpec((RW_W, RW_W), lambda i: (0, 0))],
        out_specs=row,
        out_shape=jax.ShapeDtypeStruct((n, RW_W), BF16),
        compiler_params=_cparams("arbitrary"),
    )(y2, bonus, g, ln_w.reshape(1, RW_W), ln_b.reshape(1, RW_W),
      jnp.asarray(_block_ones(RW_W, RW_N)).astype(BF16))


def _rw_states_to_blockdiag(s):
    b = s.shape[0]
    s = s.reshape(b, 2, 4, 2, RW_N, RW_N)
    z = jnp.zeros_like(s[:, :, :, 0])
    top = jnp.concatenate([s[:, :, :, 0], z], -1)
    bot = jnp.concatenate([z, s[:, :, :, 1]], -1)
    return jnp.concatenate([top, bot], -2)


def _rw_states_from_blockdiag(s):
    b = s.shape[0]
    return jnp.stack([s[:, :, :, :RW_N, :RW_N], s[:, :, :, RW_N:, RW_N:]], 3).reshape(b, 2, RW_H, RW_N, RW_N)


def _gd_blocks(ch, passes):
    n = SUPER
    c = GD_CHUNK
    nc = n // c
    idx = range(len(ch))
    msk = [x["masks"] for x in ch]
    cum = [_dot_exact_l(x["masks"]["incl_bf"], x["lw"]) for x in ch]
    tot = [_chunk_totals(cum[i], c, ch[i]["fwd"]) for i in idx]
    suf = [tot[i] - cum[i] for i in idx]
    cum_row = [cum[i].T for i in idx]
    gam = [jnp.exp(jnp.where(msk[i]["incl"], cum[i][:, 0:1] - cum_row[i], -jnp.inf)) for i in idx]
    kb = [x["k"] * x["beta"] for x in ch]
    m = [_dot(jnp.concatenate([kb[i], ch[i]["q"]], 0), ch[i]["k"], passes, _NT) for i in idx]
    a = [jnp.where(msk[i]["strict"], m[i][:n] * gam[i], 0.0) for i in idx]
    qk = [jnp.where(msk[i]["incl"], m[i][n:] * gam[i], 0.0) for i in idx]
    tinv = _tri_inv([-x for x in a], msk, passes)
    e_cum = [jnp.exp(cum[i]) for i in idx]
    uw = [_dot(tinv[i], jnp.concatenate([ch[i]["v"] * ch[i]["beta"], kb[i] * e_cum[i]], 1), passes) for i in idx]
    qq = [_dot(qk[i], uw[i], passes) for i in idx]
    u = [uw[i][:, :LANES] for i in idx]
    w = [uw[i][:, LANES:] for i in idx]
    o0 = [qq[i][:, :LANES] for i in idx]
    qhat = [ch[i]["q"] * e_cum[i] - qq[i][:, LANES:] for i in idx]
    kd = [ch[i]["k"] * jnp.exp(suf[i]) for i in idx]
    s = [x["s"] for x in ch]
    os_ = [[None] * nc for _ in ch]
    for step in range(nc):
        ci = [step if x["fwd"] else nc - 1 - step for x in ch]
        sl = [slice(ci[i] * c, (ci[i] + 1) * c) for i in idx]
        xx = [_dot(jnp.concatenate([w[i][sl[i]], qhat[i][sl[i]]], 0), s[i], passes) for i in idx]
        vn = [u[i][sl[i]] - xx[i][:c] for i in idx]
        for i in idx:
            os_[i][ci[i]] = o0[i][sl[i]] + xx[i][c:]
        upd = [_dot(kd[i][sl[i]], vn[i], passes, _TN) for i in idx]
        s = [s[i] * jnp.exp(tot[i][ci[i] * c:ci[i] * c + 1]) + upd[i] for i in idx]
    return [jnp.concatenate(o, 0) for o in os_], s


def _gd_kernel(pq_ref, pk_ref, pv_ref, pz_ref, pab_ref, wq_ref, wk_ref, wv_ref, alog_ref, dtb_ref, ng_ref,
               s0_ref, o_ref, s_ref, q_scr, k_scr, v_scr, gb_scr, o_scr, *, passes, hg):
    L = pq_ref.shape[0]
    h0 = pl.program_id(1) * hg
    l2n = lambda t: t * lax.rsqrt(jnp.sum(t * t, -1, keepdims=True) + 1e-6)
    pab = pab_ref[...]
    lane = lax.broadcasted_iota(jnp.int32, pab.shape, 1)
    g_all = -jnp.exp(alog_ref[...]) * _softplus(pab + dtb_ref[...])
    b_all = _sigmoid(pab)
    for hh in range(hg):
        cols = slice(hh * LANES, (hh + 1) * LANES)
        q_scr[:, cols] = l2n(_silu(_conv3(pq_ref[:, cols], wq_ref[:, cols]))) * (GD_DK ** -0.5)
        k_scr[:, cols] = l2n(_silu(_conv3(pk_ref[:, cols], wk_ref[:, cols])))
        v_scr[:, cols] = _silu(_conv3(pv_ref[:, cols], wv_ref[:, cols]))
        for d in range(2):
            g = jnp.sum(jnp.where(lane == d * GD_H + h0 + hh, g_all, 0.0), -1, keepdims=True)
            bt = jnp.sum(jnp.where(lane == 2 * GD_H + d * GD_H + h0 + hh, b_all, 0.0), -1, keepdims=True)
            gb_scr[d, :, cols] = jnp.broadcast_to(g, (L, LANES))
            gb_scr[2 + d, :, cols] = jnp.broadcast_to(bt, (L, LANES))
    nblk = L // SUPER

    def body(i, carry):
        chains, where = [], []
        for d in range(2):
            masks = _chunk_masks(SUPER, GD_CHUNK, d == 0)
            j = i if d == 0 else nblk - 1 - i
            rows = pl.ds(pl.multiple_of(j * SUPER, SUPER), SUPER)
            for hh in range(hg):
                cols = slice(hh * LANES, (hh + 1) * LANES)
                chains.append(dict(q=q_scr[rows, cols], k=k_scr[rows, cols], v=v_scr[rows, cols],
                                   lw=gb_scr[d, rows, cols], beta=gb_scr[2 + d, rows, cols],
                                   s=carry[d * hg + hh], masks=masks, fwd=d == 0))
                where.append((d, rows, cols))
        os_, ss = _gd_blocks(chains, passes)
        for (d, rows, cols), o in zip(where, os_):
            o_scr[d, rows, cols] = o
        return tuple(ss)

    s_fin = lax.fori_loop(0, nblk, body, tuple(s0_ref[d, hh] for d in range(2) for hh in range(hg)))
    for d in range(2):
        for hh in range(hg):
            s_ref[d, hh] = s_fin[d * hg + hh]
    for hh in range(hg):
        cols = slice(hh * LANES, (hh + 1) * LANES)
        o = o_scr[0, :, cols] + o_scr[1, :, cols]
        o = o * lax.rsqrt(jnp.mean(o * o, -1, keepdims=True) + NORM_EPS) * ng_ref[...]
        o_ref[:, cols] = (o * _silu(pz_ref[:, cols])).astype(o_ref.dtype)


def gdn_mix(p, row0, nseq, L, conv_w, a_log, dt_bias, norm_g, s0, passes, hg):
    b0 = row0 // L
    w = hg * LANES
    nhb = GD_H // hg
    col = lambda part: pl.BlockSpec((L, w), lambda b, h: (b0 + b, part * nhb + h))
    wcol = lambda part: pl.BlockSpec((3, w), lambda b, h: (0, part * nhb + h))
    vec = pl.BlockSpec((1, LANES), lambda b, h: (0, 0))
    st = pl.BlockSpec((None, 2, hg, LANES, LANES), lambda b, h: (b, 0, h, 0, 0))
    alog_row = jnp.zeros((1, LANES), F32).at[0, :2 * GD_H].set(a_log.reshape(-1))
    dtb_row = jnp.zeros((1, LANES), F32).at[0, :2 * GD_H].set(dt_bias.reshape(-1))
    return pl.pallas_call(
        functools.partial(_gd_kernel, passes=passes, hg=hg),
        grid=(nseq, nhb),
        in_specs=[col(0), col(1), col(2), col(3),
                  pl.BlockSpec((L, LANES), lambda b, h: (b0 + b, 4 * GD_H)),
                  wcol(0), wcol(1), wcol(2), vec, vec, vec, st],
        out_specs=[pl.BlockSpec((L, w), lambda b, h: (b, h)), st],
        out_shape=[jax.ShapeDtypeStruct((nseq * L, GD_H * LANES), BF16),
                   jax.ShapeDtypeStruct((nseq, 2, GD_H, LANES, LANES), F32)],
        scratch_shapes=[pltpu.VMEM((L, w), F32), pltpu.VMEM((L, w), F32), pltpu.VMEM((L, w), F32),
                        pltpu.VMEM((4, L, w), F32), pltpu.VMEM((2, L, w), F32)],
        compiler_params=_cparams("arbitrary", "arbitrary"),
    )(p, p, p, p, p, conv_w, conv_w, conv_w, alog_row, dtb_row, norm_g.reshape(1, LANES), s0)


def _router_kernel(x_ref, g_ref, sc_ref, sh_ref, wr_ref, h_ref, gate_ref, idx_ref):
    h = _norm_mod(x_ref[...], g_ref[...], sc_ref[...], sh_ref[...])
    h_ref[...] = h
    logits = _dot(h, wr_ref[...], 6)
    lane = lax.broadcasted_iota(jnp.int32, logits.shape, 1)
    logits = jnp.where(lane < N_EXP, logits, -jnp.inf)
    m1 = jnp.max(logits, -1, keepdims=True)
    i1 = jnp.min(jnp.where(logits == m1, lane, LANES), -1, keepdims=True)
    rest = jnp.where(lane == i1, -jnp.inf, logits)
    m2 = jnp.max(rest, -1, keepdims=True)
    i2 = jnp.min(jnp.where(rest == m2, lane, LANES), -1, keepdims=True)
    e2 = jnp.exp(m2 - m1)
    g1 = 1.0 / (1.0 + e2)
    g2 = e2 / (1.0 + e2)
    gate_ref[...] = jnp.where(lane == 0, g1, 0.0) + jnp.where(lane == 1, g2, 0.0)
    idx_ref[...] = jnp.where(lane == 0, i1, 0) + jnp.where(lane == 1, i2, 0)


def moe_router(x, g, sc_t, sh_t, router):
    tm = 512
    per = TM // tm
    wr = jnp.zeros((D, LANES), F32).at[:, :N_EXP].set(router)
    return pl.pallas_call(
        _router_kernel,
        grid=(T_ALL // tm,),
        in_specs=[pl.BlockSpec((tm, D), lambda i: (i, 0)),
                  pl.BlockSpec((1, D), lambda i: (0, 0)),
                  pl.BlockSpec((None, 1, D), lambda i: (i // per, 0, 0)),
                  pl.BlockSpec((None, 1, D), lambda i: (i // per, 0, 0)),
                  pl.BlockSpec((D, LANES), lambda i: (0, 0))],
        out_specs=[pl.BlockSpec((tm, D), lambda i: (i, 0)), pl.BlockSpec((tm, LANES), lambda i: (i, 0)),
                   pl.BlockSpec((tm, LANES), lambda i: (i, 0))],
        out_shape=[jax.ShapeDtypeStruct((T_ALL, D), F32), jax.ShapeDtypeStruct((T_ALL, LANES), F32),
                   jax.ShapeDtypeStruct((T_ALL, LANES), jnp.int32)],
        compiler_params=_cparams("arbitrary"),
    )(x, g.reshape(1, D), sc_t, sh_t, wr)


def moe_slot_positions(idx):
    e_flat = idx[:, :2].reshape(-1)
    onehot = (e_flat[:, None] == jnp.arange(N_EXP, dtype=jnp.int32)[None, :]).astype(jnp.int32)
    csum = jnp.cumsum(onehot, axis=0)
    rank = jnp.sum(onehot * (csum - 1), axis=1)
    counts = csum[-1]
    gsize = ((counts + MOE_TILE - 1) // MOE_TILE) * MOE_TILE
    gend = jnp.cumsum(gsize)
    pos = jnp.sum(onehot * (gend - gsize)[None, :], axis=1) + rank
    tile_start = jnp.arange(MOE_TILES, dtype=jnp.int32) * MOE_TILE
    tile_expert = jnp.minimum(jnp.sum((gend[None, :] <= tile_start[:, None]).astype(jnp.int32), axis=1), N_EXP - 1)
    return pos.astype(jnp.int32), tile_expert.astype(jnp.int32), (gend[-1:] // MOE_TILE).astype(jnp.int32)


def _row_copy(src, s, dst, d, sem):
    return pltpu.make_async_copy(src.at[pl.ds(s, 1)], dst.at[pl.ds(d, 1)], sem)


def _dispatch_kernel(pos_ref, h_ref, xs_in_ref, xs_ref, sem):
    del xs_in_ref
    tm = h_ref.shape[0]
    base = pl.program_id(0) * tm

    def issue(r, carry):
        for k in range(2):
            _row_copy(h_ref, r, xs_ref, pos_ref[(base + r) * 2 + k], sem).start()
        return carry

    lax.fori_loop(0, tm, issue, 0, unroll=8)
    for k in range(2):
        pltpu.make_async_copy(h_ref, xs_ref.at[pl.ds(0, tm)], sem).wait()


def moe_dispatch(h, pos):
    tm = 512
    return pl.pallas_call(
        _dispatch_kernel,
        grid_spec=pltpu.PrefetchScalarGridSpec(
            num_scalar_prefetch=1,
            grid=(T_ALL // tm,),
            in_specs=[pl.BlockSpec((tm, D), lambda i, pos: (i, 0)), pl.BlockSpec(memory_space=pl.ANY)],
            out_specs=pl.BlockSpec(memory_space=pl.ANY),
            scratch_shapes=[pltpu.SemaphoreType.DMA]),
        out_shape=jax.ShapeDtypeStruct((MOE_ROWS, D), F32),
        input_output_aliases={2: 0},
        compiler_params=_cparams("arbitrary"),
    )(pos, h, jnp.zeros((MOE_ROWS, D), F32))


def _expert_kernel(te_ref, nu_ref, xs_ref, wg_ref, wu_ref, wo_ref, o_ref, x_scr, acc):
    i = pl.program_id(0)
    f = pl.program_id(1)
    used = i < nu_ref[0]

    @pl.when(used & (f == 0))
    def _():
        x_scr[...] = xs_ref[...].astype(BF16)
        acc[...] = jnp.zeros_like(acc)

    @pl.when(used)
    def _():
        x = x_scr[...]
        act = _silu(_dg(x, wg_ref[...])) * _dg(x, wu_ref[...])
        acc[...] += _dg(act.astype(BF16), wo_ref[...])

    last = f == pl.num_programs(1) - 1

    @pl.when(used & last)
    def _():
        o_ref[...] = acc[...]

    @pl.when(jnp.logical_not(used) & last)
    def _():
        o_ref[...] = jnp.zeros_like(o_ref)


def moe_experts(xs, tile_expert, n_used, w_in_bf16, w_out_bf16):
    tn = 512
    nf = E_FF // tn
    live = lambda i, f, nu: jnp.where(i < nu[0], f, 0)
    return pl.pallas_call(
        _expert_kernel,
        grid_spec=pltpu.PrefetchScalarGridSpec(
            num_scalar_prefetch=2,
            grid=(MOE_TILES, nf),
            in_specs=[pl.BlockSpec((MOE_TILE, D), lambda i, f, te, nu: (i, 0)),
                      pl.BlockSpec((None, D, tn), lambda i, f, te, nu: (te[i], 0, live(i, f, nu))),
                      pl.BlockSpec((None, D, tn), lambda i, f, te, nu: (te[i], 0, live(i, f, nu) + nf)),
                      pl.BlockSpec((None, tn, D), lambda i, f, te, nu: (te[i], live(i, f, nu), 0))],
            out_specs=pl.BlockSpec((MOE_TILE, D), lambda i, f, te, nu: (i, 0)),
            scratch_shapes=[pltpu.VMEM((MOE_TILE, D), BF16), pltpu.VMEM((MOE_TILE, D), F32)]),
        out_shape=jax.ShapeDtypeStruct((MOE_ROWS, D), F32),
        compiler_params=_cparams("arbitrary", "arbitrary"),
    )(tile_expert, n_used, xs, w_in_bf16, w_in_bf16, w_out_bf16)


def _combine_kernel(pos_ref, ys_ref, gates_ref, x_ref, gate_ref, fg_ref, o_ref, y_scr, sem):
    tm = x_ref.shape[0]
    base = pl.program_id(0) * tm

    def issue(r, carry):
        for k in range(2):
            _row_copy(ys_ref, pos_ref[(base + r) * 2 + k], y_scr.at[k], r, sem).start()
        return carry

    lax.fori_loop(0, tm, issue, 0, unroll=8)
    for k in range(2):
        pltpu.make_async_copy(ys_ref.at[pl.ds(0, tm)], y_scr.at[k], sem).wait()
    gates = gates_ref[...]
    lane = lax.broadcasted_iota(jnp.int32, gates.shape, 1)
    g0 = jnp.sum(jnp.where(lane == 0, gates, 0.0), -1, keepdims=True)
    g1 = jnp.sum(jnp.where(lane == 1, gates, 0.0), -1, keepdims=True)
    xn = x_ref[...] + gate_ref[...] * (y_scr[0] * g0 + y_scr[1] * g1)
    y = xn * lax.rsqrt(jnp.mean(xn * xn, -1, keepdims=True) + NORM_EPS)
    o_ref[...] = y * fg_ref[...]


def moe_combine_final(ys, pos, gates, x, gate_t, final_g):
    tm = 256
    per = TM // tm
    return pl.pallas_call(
        _combine_kernel,
        grid_spec=pltpu.PrefetchScalarGridSpec(
            num_scalar_prefetch=1,
            grid=(T_ALL // tm,),
            in_specs=[pl.BlockSpec(memory_space=pl.ANY),
                      pl.BlockSpec((tm, LANES), lambda i, pos: (i, 0)),
                      pl.BlockSpec((tm, D), lambda i, pos: (i, 0)),
                      pl.BlockSpec((None, 1, D), lambda i, pos: (i // per, 0, 0)),
                      pl.BlockSpec((1, D), lambda i, pos: (0, 0))],
            out_specs=pl.BlockSpec((tm, D), lambda i, pos: (i, 0)),
            scratch_shapes=[pltpu.VMEM((2, tm, D), F32), pltpu.SemaphoreType.DMA]),
        out_shape=jax.ShapeDtypeStruct((T_ALL, D), F32),
        compiler_params=_cparams("arbitrary"),
    )(pos, ys, gates, x, gate_t, final_g.reshape(1, D))


def _tile_rows(mod_l, k):
    cols = mod_l[:, k * D:(k + 1) * D]
    ctx = jnp.broadcast_to(cols[0:1], (T_CTX // TM, D))
    den = jnp.repeat(cols[1:1 + B_DEN], L_DEN // TM, axis=0)
    return jnp.concatenate([ctx, den], 0)[:, None, :]


def _pad_cols(w, n):
    return jnp.pad(w, ((0, 0), (0, n - w.shape[1])))


def kernel(x_prompt, x_sample, state_rwkv, state_gdn, c, c_ctx, ada_w, ada_b, norm_mix_g, norm_ffn_g, final_norm_g, e_w_in, e_hy_conv_w, e_hy_conv_b, e_hf_w1, e_hf_b1, e_hf_freq1, e_hf_w2, e_hf_b2, e_hf_freq2, e_hf_w3, e_hy_bias, e_rw_mu, e_rw_w0, e_rw_w2, e_rw_a0, e_rw_a2, e_rw_g2, e_rw_kk, e_rw_ka, e_rw_rk, e_rw_ln_w, e_rw_ln_b, e_w_out, e_ffn_w_in, e_ffn_w_out, o_w_in, o_conv_w, o_A_log, o_dt_bias, o_norm_g, o_w_out, o_router, o_moe_w_in, o_moe_w_out):
    passes = 1
    cond16 = jnp.zeros((16, D), F32).at[0].set(c_ctx).at[1:1 + B_DEN].set(c)
    mod = modulation(cond16, ada_w, ada_b)
    x = assemble_tokens(x_prompt, x_sample)

    m0 = [_tile_rows(mod[0], k) for k in range(6)]
    p = norm_matmul(x, norm_mix_g[0], m0[1], m0[0], _pad_cols(e_w_in[0], P_EVEN_PAD).astype(BF16),
                    P_EVEN_PAD // 2)
    ys_hy, ys_rw, st_rw = [], [], None
    for row0, nseq, L in ((0, B_CTX, L_CTX), (T_CTX, B_DEN, L_DEN)):
        spec = hyena_filter_spectrum(L, e_hf_w1[0], e_hf_b1[0], e_hf_freq1[0], e_hf_w2[0], e_hf_b2[0],
                                     e_hf_freq2[0], e_hf_w3[0])
        ys_hy.append(hyena_mix(p, row0, nseq, L, e_hy_conv_w[0], e_hy_conv_b[0].reshape(1, -1), spec,
                               e_hy_bias[0], 1))
        r, v, kk, lw, kd, bd, bonus, g = rwkv_prep(p, row0, nseq, L, e_rw_mu[0], e_rw_w0[0], e_rw_w2[0],
                                                   e_rw_a0[0], e_rw_a2[0], e_rw_g2[0], e_rw_kk[0], e_rw_ka[0],
                                                   e_rw_rk[0])
        if row0 == 0:
            s0 = jnp.zeros((nseq, 2, 4, LANES, LANES), F32)
        else:
            s0 = _rw_states_to_blockdiag(state_rwkv[:, 0])
        y2, s_new = rwkv_scan(r, v, kk, lw, kd, bd, s0, nseq, L, passes, 4)
        if row0 == 0:
            st_rw = _rw_states_from_blockdiag(s_new)
        ys_rw.append(rwkv_post(y2, bonus, g, e_rw_ln_w[0], e_rw_ln_b[0]))
    y_mix = jnp.concatenate([jnp.concatenate(ys_hy, 0), jnp.concatenate(ys_rw, 0)], 1)
    x = matmul_residual(y_mix, e_w_out[0].astype(BF16), x, m0[2], D)
    act = norm_swiglu(x, norm_ffn_g[0], m0[4], m0[3], e_ffn_w_in[0].astype(BF16), D_FF, D_FF // 2)
    x = matmul_residual(act, e_ffn_w_out[0].astype(BF16), x, m0[5], D // 2)

    m1 = [_tile_rows(mod[1], k) for k in range(6)]
    p = norm_matmul(x, norm_mix_g[1], m1[1], m1[0], _pad_cols(o_w_in[0], P_ODD_PAD).astype(BF16),
                    P_ODD_PAD // 3)
    os_, st_gd = [], None
    for row0, nseq, L in ((0, B_CTX, L_CTX), (T_CTX, B_DEN, L_DEN)):
        if row0 == 0:
            s0 = jnp.zeros((nseq, 2, GD_H, LANES, LANES), F32)
        else:
            s0 = state_gdn[:, 0]
        o, s_new = gdn_mix(p, row0, nseq, L, o_conv_w[0], o_A_log[0], o_dt_bias[0], o_norm_g[0], s0, passes,
                           8 if L == L_CTX else 4)
        if row0 == 0:
            st_gd = s_new
        os_.append(o)
    x = matmul_residual(jnp.concatenate(os_, 0), o_w_out[0].astype(BF16), x, m1[2], D)
    h, gates, idx = moe_router(x, norm_ffn_g[1], m1[4], m1[3], o_router[0])
    pos, tile_expert, n_used = moe_slot_positions(idx)
    ys = moe_experts(moe_dispatch(h, pos), tile_expert, n_used, o_moe_w_in[0].astype(BF16),
                     o_moe_w_out[0].astype(BF16))
    y = moe_combine_final(ys, pos, gates, x, m1[5], final_norm_g)

    y_prompt = y[:T_CTX].reshape(B_CTX, L_CTX, D)
    y_sample = y[T_CTX:].reshape(B_DEN, L_DEN, D)
    return (y_prompt, y_sample, st_rw[:, None], st_gd[:, None])
```

```python
import functools
import math

import numpy as np
import jax
import jax.numpy as jnp
from jax import lax
from jax.experimental import pallas as pl
from jax.experimental.pallas import tpu as pltpu

F32 = jnp.float32
BF16 = jnp.bfloat16

D = 1024
B_CTX, L_CTX = 32, 256
B_DEN, L_DEN = 8, 1024
T_CTX = B_CTX * L_CTX
T_DEN = B_DEN * L_DEN
T_ALL = T_CTX + T_DEN
GRID_W = 64
NORM_EPS = 1e-6

HY_W = 512
HY_EMB = 33
HY_BANDS = 16
HY_FFN = 64
HY_TARGET, HY_FAST, HY_SLOW = 1e-2, 0.3, 1.5

RW_W = 512
RW_N = 64
RW_H = 8
RW_LORA = 224
RW_LN_EPS = 64e-5
P_EVEN = 3 * HY_W + 3 * RW_W + RW_LORA
P_EVEN_PAD = 3328

GD_H = 8
GD_DK = 128
GD_QKV = 3072
P_ODD = 4128
P_ODD_PAD = 4224

D_FF = 2816
N_EXP = 8
E_FF = 3584
MOE_TILE = 1024
MOE_ROWS = 2 * T_ALL + N_EXP * MOE_TILE
MOE_TILES = MOE_ROWS // MOE_TILE

LANES = 128
TM = 1024
RW_CHUNK = 32
GD_CHUNK = 64
SUPER = 128
VMEM_LIMIT = 56 * 1024 * 1024

_NN = (((1,), (0,)), ((), ()))
_NT = (((1,), (1,)), ((), ()))
_TN = (((0,), (0,)), ((), ()))


def _cparams(*sem):
    return pltpu.CompilerParams(dimension_semantics=sem, vmem_limit_bytes=VMEM_LIMIT)


def _dg(a, b, dims=_NN):
    return lax.dot_general(a, b, dims, preferred_element_type=F32)


def _split2(x):
    hi = x.astype(BF16)
    lo = (x - hi.astype(F32)).astype(BF16)
    return hi, lo


def _split3(x):
    x0 = x.astype(BF16)
    r1 = x - x0.astype(F32)
    x1 = r1.astype(BF16)
    x2 = (r1 - x1.astype(F32)).astype(BF16)
    return x0, x1, x2


def _dot(a, b, passes=1, dims=_NN):
    if passes == 1:
        return _dg(a.astype(BF16), b.astype(BF16), dims)
    if passes == 3:
        ah, al = _split2(a)
        bh, bl = _split2(b)
        return _dg(ah, bh, dims) + (_dg(ah, bl, dims) + _dg(al, bh, dims))
    a0, a1, a2 = _split3(a)
    b0, b1, b2 = _split3(b)
    small = _dg(a0, b2, dims) + _dg(a1, b1, dims) + _dg(a2, b0, dims)
    mid = _dg(a0, b1, dims) + _dg(a1, b0, dims)
    return _dg(a0, b0, dims) + (mid + small)


def _dot_exact_l(m, x, dims=_NN):
    x0, x1, x2 = _split3(x)
    return _dg(m, x0, dims) + (_dg(m, x1, dims) + _dg(m, x2, dims))


def _dot_exact_r(x, m, dims=_NN):
    x0, x1, x2 = _split3(x)
    return _dg(x0, m, dims) + (_dg(x1, m, dims) + _dg(x2, m, dims))


def _sigmoid(x):
    return 1.0 / (1.0 + jnp.exp(-x))


def _silu(x):
    return x * _sigmoid(x)


def _softplus(x):
    return jnp.maximum(x, 0.0) + jnp.log(1.0 + jnp.exp(-jnp.abs(x)))


def _shift_rows(x):
    n = x.shape[0]
    row = lax.broadcasted_iota(jnp.int32, x.shape, 0)
    prev = jnp.where(row == 0, 0.0, pltpu.roll(x, 1, 0))
    nxt = jnp.where(row == n - 1, 0.0, pltpu.roll(x, n - 1, 0))
    return prev, nxt


def _conv3(x, w, b=None):
    prev, nxt = _shift_rows(x)
    y = prev * w[0:1] + x * w[1:2] + nxt * w[2:3]
    return y if b is None else y + b


def _chunk_masks(n, chunk, fwd):
    row = lax.broadcasted_iota(jnp.int32, (n, n), 0)
    col = lax.broadcasted_iota(jnp.int32, (n, n), 1)
    sh = int(math.log2(chunk))
    same = (row >> sh) == (col >> sh)
    before = (col < row) if fwd else (col > row)
    pair = (row >> 1) == (col >> 1)
    joins = [((row >> (lvl + 1)) == (col >> (lvl + 1))) & ((row >> lvl) != (col >> lvl))
             for lvl in range(1, sh)]
    incl = same & (before | (row == col))
    return dict(strict=same & before, incl=incl, eye=jnp.where(row == col, 1.0, 0.0), pair=pair, joins=joins,
                incl_bf=_mask_bf16(incl))


def _mask_bf16(m):
    return jnp.where(m, 1.0, 0.0).astype(BF16)


def _chunk_totals(cum, chunk, fwd):
    n = cum.shape[0]
    rows = [cum[(ci + 1) * chunk - 1:(ci + 1) * chunk] if fwd else cum[ci * chunk:ci * chunk + 1]
            for ci in range(n // chunk)]
    return jnp.concatenate([jnp.broadcast_to(r, (chunk, cum.shape[1])) for r in rows], 0)


def _tri_inv(xs, masks, passes):
    ts = [m["eye"] + jnp.where(m["pair"], x, 0.0) for x, m in zip(xs, masks)]
    for lvl in range(len(masks[0]["joins"])):
        ps = [_dot(jnp.where(m["joins"][lvl], x, 0.0), t, passes) for x, m, t in zip(xs, masks, ts)]
        ts = [t + _dot(t, p, passes) for t, p in zip(ts, ps)]
    return ts


@functools.lru_cache(maxsize=None)
def _pos_table():
    t = np.arange(L_DEN)
    row = (t // GRID_W).astype(np.float32)
    col = (t % GRID_W).astype(np.float32)
    q = D // 4
    omega = np.exp(-math.log(10000.0) * np.arange(q, dtype=np.float32) / q).astype(np.float32)
    enc = lambda pos: np.concatenate([np.sin(pos[:, None] * omega), np.cos(pos[:, None] * omega)], -1)
    return np.concatenate([enc(row), enc(col)], -1).astype(np.float32)


@functools.lru_cache(maxsize=None)
def _dft_tables(L):
    f = np.arange(L, dtype=np.int64)
    m = (f[:, None] * f[None, :]) % (2 * L)
    ang = np.pi * m.astype(np.float64) / L
    return np.cos(ang).astype(np.float32), np.sin(ang).astype(np.float32)


@functools.lru_cache(maxsize=None)
def _hyena_static(L):
    k = np.arange(L, dtype=np.float32)
    t = k / np.float32(L - 1)
    bands = np.linspace(1e-4, HY_BANDS - 1, HY_BANDS, dtype=np.float32)
    ang = (np.float32(2.0 * math.pi) * k / np.float32(L))[:, None] * bands[None, :]
    feats = np.concatenate([t[:, None], np.cos(ang), -np.sin(ang)], -1).astype(np.float32)
    feats_p = np.zeros((L, LANES), np.float32)
    feats_p[:, :HY_EMB] = feats
    deltas = np.abs(np.linspace(math.log(HY_TARGET) / HY_FAST, math.log(HY_TARGET) / HY_SLOW, HY_W,
                                dtype=np.float32))
    window = np.exp(-t[:, None] * deltas[None, :]).astype(np.float32)
    return feats_p, window


def _block_ones(n, blk):
    i = np.arange(n) // blk
    return (i[:, None] == i[None, :]).astype(np.float32)


def _mod_kernel(c_ref, w_ref, b_ref, o_ref):
    o_ref[...] = _dot(_silu(c_ref[...]), w_ref[...], 6) + b_ref[...]


def modulation(cond16, ada_w, ada_b):
    depth = ada_w.shape[0]
    tn = 1024
    return pl.pallas_call(
        _mod_kernel,
        grid=(depth, 6 * D // tn),
        in_specs=[pl.BlockSpec((16, D), lambda i, j: (0, 0)),
                  pl.BlockSpec((None, D, tn), lambda i, j: (i, 0, j)),
                  pl.BlockSpec((None, 1, tn), lambda i, j: (i, 0, j))],
        out_specs=pl.BlockSpec((None, 16, tn), lambda i, j: (i, 0, j)),
        out_shape=jax.ShapeDtypeStruct((depth, 16, 6 * D), F32),
        compiler_params=_cparams("arbitrary", "arbitrary"),
    )(cond16, ada_w, ada_b.reshape(depth, 1, 6 * D))


def _assemble_kernel(xp_ref, xs_ref, pos_ref, o_ref):
    i = pl.program_id(0)

    @pl.when(i < T_CTX // 256)
    def _():
        o_ref[...] = xp_ref[...]

    @pl.when(i >= T_CTX // 256)
    def _():
        o_ref[...] = xs_ref[...] + pos_ref[...]


def assemble_tokens(x_prompt, x_sample):
    nc = T_CTX // 256
    pos = jnp.asarray(_pos_table())
    return pl.pallas_call(
        _assemble_kernel,
        grid=(T_ALL // 256,),
        in_specs=[pl.BlockSpec((256, D), lambda i: (jnp.minimum(i, nc - 1), 0)),
                  pl.BlockSpec((256, D), lambda i: (jnp.maximum(i - nc, 0), 0)),
                  pl.BlockSpec((256, D), lambda i: (jnp.maximum(i - nc, 0) % (L_DEN // 256), 0))],
        out_specs=pl.BlockSpec((256, D), lambda i: (i, 0)),
        out_shape=jax.ShapeDtypeStruct((T_ALL, D), F32),
        compiler_params=_cparams("arbitrary"),
    )(x_prompt.reshape(T_CTX, D), x_sample.reshape(T_DEN, D), pos)


def _norm_mod(x, g, sc, sh):
    y = x * lax.rsqrt(jnp.mean(x * x, -1, keepdims=True) + NORM_EPS)
    return (y * g) * (1.0 + sc) + sh


def _norm_mm_kernel(x_ref, g_ref, sc_ref, sh_ref, w_ref, o_ref, h_scr):
    @pl.when(pl.program_id(1) == 0)
    def _():
        h_scr[...] = _norm_mod(x_ref[...], g_ref[...], sc_ref[...], sh_ref[...]).astype(BF16)

    o_ref[...] = _dg(h_scr[...], w_ref[...]).astype(o_ref.dtype)


def norm_matmul(x, g, sc_t, sh_t, w_bf16, tn, out_dtype=F32):
    n = w_bf16.shape[1]
    return pl.pallas_call(
        _norm_mm_kernel,
        grid=(T_ALL // TM, n // tn),
        in_specs=[pl.BlockSpec((TM, D), lambda i, j: (i, 0)),
                  pl.BlockSpec((1, D), lambda i, j: (0, 0)),
                  pl.BlockSpec((None, 1, D), lambda i, j: (i, 0, 0)),
                  pl.BlockSpec((None, 1, D), lambda i, j: (i, 0, 0)),
                  pl.BlockSpec((D, tn), lambda i, j: (0, j))],
        out_specs=pl.BlockSpec((TM, tn), lambda i, j: (i, j)),
        out_shape=jax.ShapeDtypeStruct((T_ALL, n), out_dtype),
        scratch_shapes=[pltpu.VMEM((TM, D), BF16)],
        compiler_params=_cparams("arbitrary", "arbitrary"),
    )(x, g.reshape(1, D), sc_t, sh_t, w_bf16)


def _norm_swiglu_kernel(x_ref, g_ref, sc_ref, sh_ref, wg_ref, wu_ref, o_ref, h_scr):
    @pl.when(pl.program_id(1) == 0)
    def _():
        h_scr[...] = _norm_mod(x_ref[...], g_ref[...], sc_ref[...], sh_ref[...]).astype(BF16)

    h = h_scr[...]
    o_ref[...] = (_silu(_dg(h, wg_ref[...])) * _dg(h, wu_ref[...])).astype(o_ref.dtype)


def norm_swiglu(x, g, sc_t, sh_t, w_in_bf16, dff, tn):
    nj = dff // tn
    return pl.pallas_call(
        _norm_swiglu_kernel,
        grid=(T_ALL // TM, nj),
        in_specs=[pl.BlockSpec((TM, D), lambda i, j: (i, 0)),
                  pl.BlockSpec((1, D), lambda i, j: (0, 0)),
                  pl.BlockSpec((None, 1, D), lambda i, j: (i, 0, 0)),
                  pl.BlockSpec((None, 1, D), lambda i, j: (i, 0, 0)),
                  pl.BlockSpec((D, tn), lambda i, j: (0, j)),
                  pl.BlockSpec((D, tn), lambda i, j: (0, j + nj))],
        out_specs=pl.BlockSpec((TM, tn), lambda i, j: (i, j)),
        out_shape=jax.ShapeDtypeStruct((T_ALL, dff), BF16),
        scratch_shapes=[pltpu.VMEM((TM, D), BF16)],
        compiler_params=_cparams("arbitrary", "arbitrary"),
    )(x, g.reshape(1, D), sc_t, sh_t, w_in_bf16, w_in_bf16)


def _mm_res_kernel(y_ref, w_ref, x_ref, gate_ref, o_ref):
    o_ref[...] = x_ref[...] + gate_ref[...] * _dg(y_ref[...], w_ref[...])


def matmul_residual(y_bf16, w_bf16, x, gate_t, tn):
    k = y_bf16.shape[1]
    return pl.pallas_call(
        _mm_res_kernel,
        grid=(T_ALL // TM, D // tn),
        in_specs=[pl.BlockSpec((TM, k), lambda i, j: (i, 0)),
                  pl.BlockSpec((k, tn), lambda i, j: (0, j)),
                  pl.BlockSpec((TM, tn), lambda i, j: (i, j)),
                  pl.BlockSpec((None, 1, tn), lambda i, j: (i, 0, j))],
        out_specs=pl.BlockSpec((TM, tn), lambda i, j: (i, j)),
        out_shape=jax.ShapeDtypeStruct((T_ALL, D), F32),
        compiler_params=_cparams("arbitrary", "arbitrary"),
    )(y_bf16, w_bf16, x, gate_t)


def _hy_filter_kernel(feat_ref, w1_ref, b1_ref, f1_ref, w2_ref, b2_ref, f2_ref, w3f_ref, w3b_ref, win_ref,
                      c_ref, s_ref, hr_ref, hi_ref, hn_ref):
    L = feat_ref.shape[0]
    h = jnp.sin(f1_ref[...] * (_dot(feat_ref[...], w1_ref[...], 6) + b1_ref[...]))
    h = jnp.sin(f2_ref[...] * (_dot(h, w2_ref[...], 6) + b2_ref[...]))
    win = win_ref[...]
    fw = _dot(h, w3f_ref[...], 6) * win
    bw = _dot(h, w3b_ref[...], 6) * win
    row = lax.broadcasted_iota(jnp.int32, fw.shape, 0)
    bw = jnp.where(row == 0, 0.0, bw)
    nrm = jnp.sum(jnp.abs(fw), 0, keepdims=True) + jnp.sum(jnp.abs(bw), 0, keepdims=True)
    ev = (fw + bw) / nrm
    od = (bw - fw) / nrm
    alt = (1 - 2 * (row & 1)).astype(F32)
    hr_ref[...] = _dot(c_ref[...], ev, 3)
    hi_ref[...] = _dot(s_ref[...], od, 3)
    hn_ref[...] = jnp.broadcast_to(jnp.sum(ev * alt, 0, keepdims=True), (8, ev.shape[1]))
    del L


def hyena_filter_spectrum(L, w1, b1, f1, w2, b2, f2, w3):
    feats, window = _hyena_static(L)
    cos_t, sin_t = _dft_tables(L)
    tc = 128
    ncb = HY_W // tc
    w1p = jnp.zeros((LANES, HY_FFN), F32).at[:HY_EMB].set(w1)
    const = lambda shape: pl.BlockSpec(shape, lambda o, c: (0,) * len(shape))
    return pl.pallas_call(
        _hy_filter_kernel,
        grid=(2, ncb),
        in_specs=[const((L, LANES)), const((LANES, HY_FFN)), const((1, HY_FFN)), const((1, HY_FFN)),
                  const((HY_FFN, HY_FFN)), const((1, HY_FFN)), const((1, HY_FFN)),
                  pl.BlockSpec((HY_FFN, tc), lambda o, c: (0, o * ncb + c)),
                  pl.BlockSpec((HY_FFN, tc), lambda o, c: (0, 2 * ncb + o * ncb + c)),
                  pl.BlockSpec((L, tc), lambda o, c: (0, c)),
                  const((L, L)), const((L, L))],
        out_specs=[pl.BlockSpec((None, L, tc), lambda o, c: (o, 0, c)),
                   pl.BlockSpec((None, L, tc), lambda o, c: (o, 0, c)),
                   pl.BlockSpec((None, 8, tc), lambda o, c: (o, 0, c))],
        out_shape=[jax.ShapeDtypeStruct((2, L, HY_W), F32), jax.ShapeDtypeStruct((2, L, HY_W), F32),
                   jax.ShapeDtypeStruct((2, 8, HY_W), F32)],
        compiler_params=_cparams("arbitrary", "arbitrary"),
    )(jnp.asarray(feats), w1p, b1.reshape(1, -1), f1.reshape(1, -1), w2, b2.reshape(1, -1), f2.reshape(1, -1),
      w3, w3, jnp.asarray(window), jnp.asarray(cos_t), jnp.asarray(sin_t))


def _hy_mix_kernel(pv_ref, p1_ref, p2_ref, wv_ref, w1_ref, w2_ref, bv_ref, b1_ref, b2_ref,
                   hr_ref, hi_ref, hn_ref, bias_ref, ch_ref, cl_ref, sh_ref, sl_ref, o_ref, *, passes):
    L = pv_ref.shape[0]
    z = _conv3(pv_ref[...].astype(F32), wv_ref[...], bv_ref[...])
    gates = (_conv3(p1_ref[...].astype(F32), w1_ref[...], b1_ref[...]),
             _conv3(p2_ref[...].astype(F32), w2_ref[...], b2_ref[...]))
    row = lax.broadcasted_iota(jnp.int32, z.shape, 0)
    alt = (1 - 2 * (row & 1)).astype(F32)
    ch, cl, sh, sl = ch_ref[...], cl_ref[...], sh_ref[...], sl_ref[...]

    def tdot(th, tl, x):
        if passes == 1:
            return _dg(th, x.astype(BF16))
        xh, xl = _split2(x)
        return _dg(th, xh) + (_dg(th, xl) + _dg(tl, xh))

    inv_l = 1.0 / L
    for o in range(2):
        hr, hi, hn = hr_ref[o], hi_ref[o], hn_ref[o][0:1]
        zc = tdot(ch, cl, z)
        zs = tdot(sh, sl, z)
        zn = jnp.sum(z * alt, 0, keepdims=True)
        yr = zc * hr + zs * hi
        yi = zc * hi - zs * hr
        wr = jnp.where(row == 0, 0.5 * inv_l, inv_l)
        conv = tdot(ch, cl, yr * wr) - tdot(sh, sl, yi * inv_l) + alt * (zn * hn * (0.5 * inv_l))
        z = gates[o] * (conv + z * bias_ref[o:o + 1])
    o_ref[...] = z.astype(o_ref.dtype)


def hyena_mix(p, row0, nseq, L, conv_w, conv_b, spec, bias, passes):
    hr, hi, hn = spec
    cos_t, sin_t = _dft_tables(L)
    ch, cl = _np_split2(cos_t)
    sh, sl = _np_split2(sin_t)
    tc = 512 if L <= 256 else 256
    ncb = HY_W // tc
    b0 = row0 // L
    pspec = lambda part: pl.BlockSpec((L, tc), lambda b, c: (b0 + b, part * ncb + c))
    wspec = lambda part: pl.BlockSpec((3, tc), lambda b, c: (0, part * ncb + c))
    bspec = lambda part: pl.BlockSpec((1, tc), lambda b, c: (0, part * ncb + c))
    hspec = lambda rows: pl.BlockSpec((2, rows, tc), lambda b, c: (0, 0, c))
    tab = pl.BlockSpec((L, L), lambda b, c: (0, 0))
    return pl.pallas_call(
        functools.partial(_hy_mix_kernel, passes=passes),
        grid=(nseq, ncb),
        in_specs=[pspec(0), pspec(1), pspec(2), wspec(0), wspec(1), wspec(2), bspec(0), bspec(1), bspec(2),
                  hspec(L), hspec(L), hspec(8), pl.BlockSpec((2, tc), lambda b, c: (0, c)), tab, tab, tab, tab],
        out_specs=pl.BlockSpec((L, tc), lambda b, c: (b, c)),
        out_shape=jax.ShapeDtypeStruct((nseq * L, HY_W), BF16),
        compiler_params=_cparams("arbitrary", "arbitrary"),
    )(p, p, p, conv_w, conv_w, conv_w, conv_b, conv_b, conv_b, hr, hi, hn, bias,
      jnp.asarray(ch), jnp.asarray(cl), jnp.asarray(sh), jnp.asarray(sl))


def _np_split2(x):
    hi = x.astype(jnp.bfloat16)
    lo = (x - hi.astype(np.float32)).astype(jnp.bfloat16)
    return hi, lo


def _rw_prep_kernel(pr_ref, pk_ref, pv_ref, pl_ref, mur_ref, muk_ref, muv_ref, mul_ref, w0_ref, a0_ref,
                    w2_ref, a2_ref, g2_ref, kkw_ref, kaw_ref, rkw_ref, ones_ref,
                    r_ref, v_ref, kk_ref, lw_ref, kd_ref, bd_ref, bon_ref, g_ref):
    def shift(p, mu):
        prev, nxt = _shift_rows(p)
        return p + (0.5 * (prev + nxt) - p) * mu

    r = shift(pr_ref[...].astype(F32), mur_ref[...])
    k = shift(pk_ref[...].astype(F32), muk_ref[...])
    v = shift(pv_ref[...].astype(F32), muv_ref[...])
    lo = shift(pl_ref[...].astype(F32), mul_ref[...])
    ones = ones_ref[...]
    g_ref[...] = _dot(_sigmoid(lo), g2_ref[...], 1)
    kkr = k * kkw_ref[...]
    kk = kkr / jnp.maximum(jnp.sqrt(_dot_exact_r(kkr * kkr, ones)), 1e-12)
    th = jnp.tanh(lo)
    bon = jnp.zeros_like(r)
    for d in range(2):
        w = -_softplus(-(w0_ref[d:d + 1] + _dot(th, w2_ref[d], 3))) - 0.5
        lw_ref[d] = -jnp.exp(w)
        a = _sigmoid(a0_ref[d:d + 1] + _dot(lo, a2_ref[d], 1))
        kd = k * (1.0 + (a - 1.0) * kaw_ref[...])
        kd_ref[d] = kd
        bd_ref[d] = kk * a
        bon = bon + _dot_exact_r(r * kd * rkw_ref[...], ones) * v
    r_ref[...] = r
    v_ref[...] = v
    kk_ref[...] = kk
    bon_ref[...] = bon


def rwkv_prep(p, row0, nseq, L, mu, w0, w2, a0, a2, g2, k_k, k_a, r_k):
    b0 = row0 // L
    cb = RW_W if L <= 256 else RW_W // 2
    ncb = RW_W // cb
    c0 = 3 * HY_W // cb
    wide = lambda part: pl.BlockSpec((L, cb), lambda b, c: (b0 + b, c0 + part * ncb + c))
    lora = pl.BlockSpec((L, 256), lambda b, c: (b0 + b, (3 * HY_W + 3 * RW_W) // 256))
    muw = lambda part: pl.BlockSpec((1, cb), lambda b, c: (0, part * ncb + c))
    vec = lambda rows: pl.BlockSpec((rows, cb), lambda b, c: (0, c))
    mu_p = jnp.zeros((1, 3 * RW_W + 256), F32).at[0, :3 * RW_W + RW_LORA].set(mu)
    w2f = jnp.zeros((2, 256, RW_W), F32).at[0, 0:32].set(w2[0]).at[1, 32:64].set(w2[1])
    a2f = jnp.zeros((2, 256, RW_W), F32).at[0, 64:96].set(a2[0]).at[1, 96:128].set(a2[1])
    g2f = jnp.zeros((256, RW_W), F32).at[128:224].set(g2)
    n = nseq * L
    one = jax.ShapeDtypeStruct((n, RW_W), F32)
    two = jax.ShapeDtypeStruct((2, n, RW_W), F32)
    ospec1 = pl.BlockSpec((L, cb), lambda b, c: (b, c))
    ospec2 = pl.BlockSpec((2, L, cb), lambda b, c: (0, b, c))
    return pl.pallas_call(
        _rw_prep_kernel,
        grid=(nseq, ncb),
        in_specs=[wide(0), wide(1), wide(2), lora, muw(0), muw(1), muw(2),
                  pl.BlockSpec((1, 256), lambda b, c: (0, 3 * RW_W // 256)),
                  vec(2), vec(2),
                  pl.BlockSpec((2, 256, cb), lambda b, c: (0, 0, c)),
                  pl.BlockSpec((2, 256, cb), lambda b, c: (0, 0, c)),
                  pl.BlockSpec((256, cb), lambda b, c: (0, c)),
                  vec(1), vec(1), vec(1),
                  pl.BlockSpec((cb, cb), lambda b, c: (0, 0))],
        out_specs=[ospec1, ospec1, ospec1, ospec2, ospec2, ospec2, ospec1, ospec1],
        out_shape=[one, one, one, two, two, two, one, one],
        compiler_params=_cparams("arbitrary", "arbitrary"),
    )(p, p, p, p, mu_p, mu_p, mu_p, mu_p, w0, a0, w2f, a2f, g2f, k_k.reshape(1, RW_W), k_a.reshape(1, RW_W),
      r_k.reshape(1, RW_W), jnp.asarray(_block_ones(cb, RW_N)).astype(BF16))


def _rw_blocks(ch, passes):
    n = SUPER
    c = RW_CHUNK
    nc = n // c
    idx = range(len(ch))
    cum = [_dot_exact_l(x["masks"]["incl_bf"], x["lw"]) for x in ch]
    tot = [_chunk_totals(cum[i], c, ch[i]["fwd"]) for i in idx]
    suf = [tot[i] - cum[i] for i in idx]
    e_neg =[jnp.exp(-cum[i]) for i in idx]
    at = [ch[i]["a"] * jnp.exp(cum[i] - ch[i]["lw"]) for i in idx]
    rt = [ch[i]["r"] * jnp.exp(cum[i]) for i in idx]
    bk = [jnp.concatenate([ch[i]["b"] * e_neg[i], ch[i]["k"] * e_neg[i]], 0) for i in idx]
    e_suf = [jnp.exp(suf[i]) for i in idx]
    bp = [ch[i]["b"] * e_suf[i] for i in idx]
    kp = [ch[i]["k"] * e_suf[i] for i in idx]
    lane = lax.broadcasted_iota(jnp.int32, (1, LANES), 1)
    heads = range(LANES // RW_N)
    sub = [(i, g) for i in idx for g in heads]
    mg = [(lane >> 6) == g for g in heads]
    at_g = [jnp.where(mg[g], at[i], 0.0) for i, g in sub]
    v_g = [jnp.where(mg[g], ch[i]["v"], 0.0) for i, g in sub]
    m = [_dot(jnp.concatenate([at_g[j], jnp.where(mg[g], rt[i], 0.0)], 0), bk[i], passes, _NT)
         for j, (i, g) in enumerate(sub)]
    smask = [ch[i]["masks"] for i, g in sub]
    ab = [jnp.where(smask[j]["strict"], m[j][:n, :n], 0.0) for j in range(len(sub))]
    ak = [jnp.where(smask[j]["strict"], m[j][:n, n:], 0.0) for j in range(len(sub))]
    rb = [jnp.where(smask[j]["incl"], m[j][n:, :n], 0.0) for j in range(len(sub))]
    rk = [jnp.where(smask[j]["incl"], m[j][n:, n:], 0.0) for j in range(len(sub))]
    tinv = _tri_inv(ab, smask, passes)
    akv = [_dot(ak[j], v_g[j], passes) for j in range(len(sub))]
    aw = [_dot(tinv[j], jnp.concatenate([at_g[j], akv[j]], 1), passes) for j in range(len(sub))]
    ry = [_dot(rb[j], aw[j], passes) for j in range(len(sub))]
    rkv = [_dot(rk[j], v_g[j], passes) for j in range(len(sub))]
    nh = len(heads)
    ahat = [sum(aw[i * nh + g][:, :LANES] for g in heads) for i in idx]
    w1 = [sum(aw[i * nh + g][:, LANES:] for g in heads) for i in idx]
    rhat = [rt[i] + sum(ry[i * nh + g][:, :LANES] for g in heads) for i in idx]
    y0 = [sum(ry[i * nh + g][:, LANES:] + rkv[i * nh + g] for g in heads) for i in idx]
    rowl = lax.broadcasted_iota(jnp.int32, (LANES, LANES), 0)
    coll = lax.broadcasted_iota(jnp.int32, (LANES, LANES), 1)
    diag_blocks = (rowl >> 6) == (coll >> 6)
    s = [x["s"] for x in ch]
    ys = [[None] * nc for _ in ch]
    for step in range(nc):
        ci = [step if x["fwd"] else nc - 1 - step for x in ch]
        sl = [slice(ci[i] * c, (ci[i] + 1) * c) for i in idx]
        xx = [_dot(jnp.concatenate([ahat[i][sl[i]], rhat[i][sl[i]]], 0), s[i], passes, _NT) for i in idx]
        u = [w1[i][sl[i]] + xx[i][:c] for i in idx]
        for i in idx:
            ys[i][ci[i]] = y0[i][sl[i]] + xx[i][c:]
        upd = [_dot(jnp.concatenate([u[i], ch[i]["v"][sl[i]]], 0),
                    jnp.concatenate([bp[i][sl[i]], kp[i][sl[i]]], 0), passes, _TN) for i in idx]
        s = [s[i] * jnp.exp(tot[i][ci[i] * c:ci[i] * c + 1]) + jnp.where(diag_blocks, upd[i], 0.0) for i in idx]
    return [jnp.concatenate(y, 0) for y in ys], s


def _rw_scan_kernel(r_ref, v_ref, kk_ref, lw_ref, kd_ref, bd_ref, s0_ref, y_ref, s_ref, *, passes, gp):
    L = r_ref.shape[0]
    nblk = L // SUPER

    def body(i, carry):
        chains, where = [], []
        for d in range(2):
            masks = _chunk_masks(SUPER, RW_CHUNK, d == 0)
            j = i if d == 0 else nblk - 1 - i
            rows = pl.ds(pl.multiple_of(j * SUPER, SUPER), SUPER)
            for g in range(gp):
                cols = slice(g * LANES, (g + 1) * LANES)
                chains.append(dict(r=r_ref[rows, cols], lw=lw_ref[d, rows, cols], k=kd_ref[d, rows, cols],
                                   v=v_ref[rows, cols], a=-kk_ref[rows, cols], b=bd_ref[d, rows, cols],
                                   s=carry[d * gp + g], masks=masks, fwd=d == 0))
                where.append((d, rows, cols))
        ys, ss = _rw_blocks(chains, passes)
        for (d, rows, cols), y in zip(where, ys):
            y_ref[d, rows, cols] = y
        return tuple(ss)

    s_fin = lax.fori_loop(0, nblk, body, tuple(s0_ref[d, g] for d in range(2) for g in range(gp)))
    for d in range(2):
        for g in range(gp):
            s_ref[d, g] = s_fin[d * gp + g]


def rwkv_scan(r, v, kk, lw, kd, bd, s0_bd, nseq, L, passes, gp):
    ngrp = RW_W // LANES
    w = gp * LANES
    one = pl.BlockSpec((L, w), lambda b, g: (b, g))
    two = pl.BlockSpec((2, L, w), lambda b, g: (0, b, g))
    st = pl.BlockSpec((None, 2, gp, LANES, LANES), lambda b, g: (b, 0, g, 0, 0))
    return pl.pallas_call(
        functools.partial(_rw_scan_kernel, passes=passes, gp=gp),
        grid=(nseq, ngrp // gp),
        in_specs=[one, one, one, two, two, two, st],
        out_specs=[two, st],
        out_shape=[jax.ShapeDtypeStruct((2, nseq * L, RW_W), F32),
                   jax.ShapeDtypeStruct((nseq, 2, ngrp, LANES, LANES), F32)],
        compiler_params=_cparams("arbitrary", "arbitrary"),
    )(r, v, kk, lw, kd, bd, s0_bd)


def _rw_post_kernel(y_ref, bon_ref, g_ref, lnw_ref, lnb_ref, ones_ref, o_ref):
    y = y_ref[0] + y_ref[1]
    ones = ones_ref[...]
    mean = _dot_exact_r(y, ones) * (1.0 / RW_N)
    yc = y - mean
    var = _dot_exact_r(yc * yc, ones) * (1.0 / RW_N)
    yn = yc * lax.rsqrt(var + RW_LN_EPS) * lnw_ref[...] + lnb_ref[...]
    o_ref[...] = ((yn + bon_ref[...]) * g_ref[...]).astype(o_ref.dtype)


def rwkv_post(y2, bonus, g, ln_w, ln_b):
    n = bonus.shape[0]
    tm = 512
    row = pl.BlockSpec((tm, RW_W), lambda i: (i, 0))
    vec = pl.BlockSpec((1, RW_W), lambda i: (0, 0))
    return pl.pallas_call(
        _rw_post_kernel,
        grid=(n // tm,),
        in_specs=[pl.BlockSpec((2, tm, RW_W), lambda i: (0, i, 0)), row, row, vec, vec,
                  pl.BlockSpec((RW_W, RW_W), lambda i: (0, 0))],
        out_specs=row,
        out_shape=jax.ShapeDtypeStruct((n, RW_W), BF16),
        compiler_params=_cparams("arbitrary"),
    )(y2, bonus, g, ln_w.reshape(1, RW_W), ln_b.reshape(1, RW_W),
      jnp.asarray(_block_ones(RW_W, RW_N)).astype(BF16))


def _rw_states_to_blockdiag(s):
    b = s.shape[0]
    s = s.reshape(b, 2, 4, 2, RW_N, RW_N)
    z = jnp.zeros_like(s[:, :, :, 0])
    top = jnp.concatenate([s[:, :, :, 0], z], -1)
    bot = jnp.concatenate([z, s[:, :, :, 1]], -1)
    return jnp.concatenate([top, bot], -2)


def _rw_states_from_blockdiag(s):
    b = s.shape[0]
    return jnp.stack([s[:, :, :, :RW_N, :RW_N], s[:, :, :, RW_N:, RW_N:]], 3).reshape(b, 2, RW_H, RW_N, RW_N)


def _gd_blocks(ch, passes):
    n = SUPER
    c = GD_CHUNK
    nc = n // c
    idx = range(len(ch))
    msk = [x["masks"] for x in ch]
    cum = [_dot_exact_l(x["masks"]["incl_bf"], x["lw"]) for x in ch]
    tot = [_chunk_totals(cum[i], c, ch[i]["fwd"]) for i in idx]
    suf = [tot[i] - cum[i] for i in idx]
    cum_row = [cum[i].T for i in idx]
    gam = [jnp.exp(jnp.where(msk[i]["incl"], cum[i][:, 0:1] - cum_row[i], -jnp.inf)) for i in idx]
    kb = [x["k"] * x["beta"] for x in ch]
    m = [_dot(jnp.concatenate([kb[i], ch[i]["q"]], 0), ch[i]["k"], passes, _NT) for i in idx]
    a = [jnp.where(msk[i]["strict"], m[i][:n] * gam[i], 0.0) for i in idx]
    qk = [jnp.where(msk[i]["incl"], m[i][n:] * gam[i], 0.0) for i in idx]
    tinv = _tri_inv([-x for x in a], msk, passes)
    e_cum = [jnp.exp(cum[i]) for i in idx]
    uw = [_dot(tinv[i], jnp.concatenate([ch[i]["v"] * ch[i]["beta"], kb[i] * e_cum[i]], 1), passes) for i in idx]
    qq = [_dot(qk[i], uw[i], passes) for i in idx]
    u = [uw[i][:, :LANES] for i in idx]
    w = [uw[i][:, LANES:] for i in idx]
    o0 = [qq[i][:, :LANES] for i in idx]
    qhat = [ch[i]["q"] * e_cum[i] - qq[i][:, LANES:] for i in idx]
    kd = [ch[i]["k"] * jnp.exp(suf[i]) for i in idx]
    s = [x["s"] for x in ch]
    os_ = [[None] * nc for _ in ch]
    for step in range(nc):
        ci = [step if x["fwd"] else nc - 1 - step for x in ch]
        sl = [slice(ci[i] * c, (ci[i] + 1) * c) for i in idx]
        xx = [_dot(jnp.concatenate([w[i][sl[i]], qhat[i][sl[i]]], 0), s[i], passes) for i in idx]
        vn = [u[i][sl[i]] - xx[i][:c] for i in idx]
        for i in idx:
            os_[i][ci[i]] = o0[i][sl[i]] + xx[i][c:]
        upd = [_dot(kd[i][sl[i]], vn[i], passes, _TN) for i in idx]
        s = [s[i] * jnp.exp(tot[i][ci[i] * c:ci[i] * c + 1]) + upd[i] for i in idx]
    return [jnp.concatenate(o, 0) for o in os_], s


def _gd_kernel(pq_ref, pk_ref, pv_ref, pz_ref, pab_ref, wq_ref, wk_ref, wv_ref, alog_ref, dtb_ref, ng_ref,
               s0_ref, o_ref, s_ref, q_scr, k_scr, v_scr, gb_scr, o_scr, *, passes, hg):
    L = pq_ref.shape[0]
    h0 = pl.program_id(1) * hg
    l2n = lambda t: t * lax.rsqrt(jnp.sum(t * t, -1, keepdims=True) + 1e-6)
    pab = pab_ref[...].astype(F32)
    lane = lax.broadcasted_iota(jnp.int32, pab.shape, 1)
    g_all = -jnp.exp(alog_ref[...]) * _softplus(pab + dtb_ref[...])
    b_all = _sigmoid(pab)
    for hh in range(hg):
        cols = slice(hh * LANES, (hh + 1) * LANES)
        q_scr[:, cols] = l2n(_silu(_conv3(pq_ref[:, cols].astype(F32), wq_ref[:, cols]))) * (GD_DK ** -0.5)
        k_scr[:, cols] = l2n(_silu(_conv3(pk_ref[:, cols].astype(F32), wk_ref[:, cols])))
        v_scr[:, cols] = _silu(_conv3(pv_ref[:, cols].astype(F32), wv_ref[:, cols]))
        for d in range(2):
            g = jnp.sum(jnp.where(lane == d * GD_H + h0 + hh, g_all, 0.0), -1, keepdims=True)
            bt = jnp.sum(jnp.where(lane == 2 * GD_H + d * GD_H + h0 + hh, b_all, 0.0), -1, keepdims=True)
            gb_scr[d, :, cols] = jnp.broadcast_to(g, (L, LANES))
            gb_scr[2 + d, :, cols] = jnp.broadcast_to(bt, (L, LANES))
    nblk = L // SUPER

    def body(i, carry):
        chains, where = [], []
        for d in range(2):
            masks = _chunk_masks(SUPER, GD_CHUNK, d == 0)
            j = i if d == 0 else nblk - 1 - i
            rows = pl.ds(pl.multiple_of(j * SUPER, SUPER), SUPER)
            for hh in range(hg):
                cols = slice(hh * LANES, (hh + 1) * LANES)
                chains.append(dict(q=q_scr[rows, cols], k=k_scr[rows, cols], v=v_scr[rows, cols],
                                   lw=gb_scr[d, rows, cols], beta=gb_scr[2 + d, rows, cols],
                                   s=carry[d * hg + hh], masks=masks, fwd=d == 0))
                where.append((d, rows, cols))
        os_, ss = _gd_blocks(chains, passes)
        for (d, rows, cols), o in zip(where, os_):
            o_scr[d, rows, cols] = o
        return tuple(ss)

    s_fin = lax.fori_loop(0, nblk, body, tuple(s0_ref[d, hh] for d in range(2) for hh in range(hg)))
    for d in range(2):
        for hh in range(hg):
            s_ref[d, hh] = s_fin[d * hg + hh]
    for hh in range(hg):
        cols = slice(hh * LANES, (hh + 1) * LANES)
        o = o_scr[0, :, cols] + o_scr[1, :, cols]
        o = o * lax.rsqrt(jnp.mean(o * o, -1, keepdims=True) + NORM_EPS) * ng_ref[...]
        o_ref[:, cols] = (o * _silu(pz_ref[:, cols].astype(F32))).astype(o_ref.dtype)


def gdn_mix(p, row0, nseq, L, conv_w, a_log, dt_bias, norm_g, s0, passes, hg):
    b0 = row0 // L
    w = hg * LANES
    nhb = GD_H // hg
    col = lambda part: pl.BlockSpec((L, w), lambda b, h: (b0 + b, part * nhb + h))
    wcol = lambda part: pl.BlockSpec((3, w), lambda b, h: (0, part * nhb + h))
    vec = pl.BlockSpec((1, LANES), lambda b, h: (0, 0))
    st = pl.BlockSpec((None, 2, hg, LANES, LANES), lambda b, h: (b, 0, h, 0, 0))
    alog_row = jnp.zeros((1, LANES), F32).at[0, :2 * GD_H].set(a_log.reshape(-1))
    dtb_row = jnp.zeros((1, LANES), F32).at[0, :2 * GD_H].set(dt_bias.reshape(-1))
    return pl.pallas_call(
        functools.partial(_gd_kernel, passes=passes, hg=hg),
        grid=(nseq, nhb),
        in_specs=[col(0), col(1), col(2), col(3),
                  pl.BlockSpec((L, LANES), lambda b, h: (b0 + b, 4 * GD_H)),
                  wcol(0), wcol(1), wcol(2), vec, vec, vec, st],
        out_specs=[pl.BlockSpec((L, w), lambda b, h: (b, h)), st],
        out_shape=[jax.ShapeDtypeStruct((nseq * L, GD_H * LANES), BF16),
                   jax.ShapeDtypeStruct((nseq, 2, GD_H, LANES, LANES), F32)],
        scratch_shapes=[pltpu.VMEM((L, w), F32), pltpu.VMEM((L, w), F32), pltpu.VMEM((L, w), F32),
                        pltpu.VMEM((4, L, w), F32), pltpu.VMEM((2, L, w), F32)],
        compiler_params=_cparams("arbitrary", "arbitrary"),
    )(p, p, p, p, p, conv_w, conv_w, conv_w, alog_row, dtb_row, norm_g.reshape(1, LANES), s0)


def _router_kernel(x_ref, g_ref, sc_ref, sh_ref, wr_ref, h_ref, gate_ref, idx_ref):
    h = _norm_mod(x_ref[...], g_ref[...], sc_ref[...], sh_ref[...])
    h_ref[...] = h
    logits = _dot(h, wr_ref[...], 6)
    lane = lax.broadcasted_iota(jnp.int32, logits.shape, 1)
    logits = jnp.where(lane < N_EXP, logits, -jnp.inf)
    m1 = jnp.max(logits, -1, keepdims=True)
    i1 = jnp.min(jnp.where(logits == m1, lane, LANES), -1, keepdims=True)
    rest = jnp.where(lane == i1, -jnp.inf, logits)
    m2 = jnp.max(rest, -1, keepdims=True)
    i2 = jnp.min(jnp.where(rest == m2, lane, LANES), -1, keepdims=True)
    e2 = jnp.exp(m2 - m1)
    g1 = 1.0 / (1.0 + e2)
    g2 = e2 / (1.0 + e2)
    gate_ref[...] = jnp.where(lane == 0, g1, 0.0) + jnp.where(lane == 1, g2, 0.0)
    idx_ref[...] = jnp.where(lane == 0, i1, 0) + jnp.where(lane == 1, i2, 0)


def moe_router(x, g, sc_t, sh_t, router):
    tm = 512
    per = TM // tm
    wr = jnp.zeros((D, LANES), F32).at[:, :N_EXP].set(router)
    return pl.pallas_call(
        _router_kernel,
        grid=(T_ALL // tm,),
        in_specs=[pl.BlockSpec((tm, D), lambda i: (i, 0)),
                  pl.BlockSpec((1, D), lambda i: (0, 0)),
                  pl.BlockSpec((None, 1, D), lambda i: (i // per, 0, 0)),
                  pl.BlockSpec((None, 1, D), lambda i: (i // per, 0, 0)),
                  pl.BlockSpec((D, LANES), lambda i: (0, 0))],
        out_specs=[pl.BlockSpec((tm, D), lambda i: (i, 0)), pl.BlockSpec((tm, LANES), lambda i: (i, 0)),
                   pl.BlockSpec((tm, LANES), lambda i: (i, 0))],
        out_shape=[jax.ShapeDtypeStruct((T_ALL, D), F32), jax.ShapeDtypeStruct((T_ALL, LANES), F32),
                   jax.ShapeDtypeStruct((T_ALL, LANES), jnp.int32)],
        compiler_params=_cparams("arbitrary"),
    )(x, g.reshape(1, D), sc_t, sh_t, wr)


def moe_slot_positions(idx):
    e_flat = idx[:, :2].reshape(-1)
    onehot = (e_flat[:, None] == jnp.arange(N_EXP, dtype=jnp.int32)[None, :]).astype(jnp.int32)
    csum = jnp.cumsum(onehot, axis=0)
    rank = jnp.sum(onehot * (csum - 1), axis=1)
    counts = csum[-1]
    gsize = ((counts + MOE_TILE - 1) // MOE_TILE) * MOE_TILE
    gend = jnp.cumsum(gsize)
    pos = jnp.sum(onehot * (gend - gsize)[None, :], axis=1) + rank
    tile_start = jnp.arange(MOE_TILES, dtype=jnp.int32) * MOE_TILE
    tile_expert = jnp.minimum(jnp.sum((gend[None, :] <= tile_start[:, None]).astype(jnp.int32), axis=1), N_EXP - 1)
    return pos.astype(jnp.int32), tile_expert.astype(jnp.int32), (gend[-1:] // MOE_TILE).astype(jnp.int32)


def _row_copy(src, s, dst, d, sem):
    return pltpu.make_async_copy(src.at[pl.ds(s, 1)], dst.at[pl.ds(d, 1)], sem)


def _dispatch_kernel(pos_ref, h_ref, xs_in_ref, xs_ref, sem):
    del xs_in_ref
    tm = h_ref.shape[0]
    base = pl.program_id(0) * tm

    def issue(r, carry):
        for k in range(2):
            _row_copy(h_ref, r, xs_ref, pos_ref[(base + r) * 2 + k], sem).start()
        return carry

    lax.fori_loop(0, tm, issue, 0, unroll=8)
    for k in range(2):
        pltpu.make_async_copy(h_ref, xs_ref.at[pl.ds(0, tm)], sem).wait()


def moe_dispatch(h, pos):
    tm = 512
    return pl.pallas_call(
        _dispatch_kernel,
        grid_spec=pltpu.PrefetchScalarGridSpec(
            num_scalar_prefetch=1,
            grid=(T_ALL // tm,),
            in_specs=[pl.BlockSpec((tm, D), lambda i, pos: (i, 0)), pl.BlockSpec(memory_space=pl.ANY)],
            out_specs=pl.BlockSpec(memory_space=pl.ANY),
            scratch_shapes=[pltpu.SemaphoreType.DMA]),
        out_shape=jax.ShapeDtypeStruct((MOE_ROWS, D), F32),
        input_output_aliases={2: 0},
        compiler_params=_cparams("arbitrary"),
    )(pos, h, jnp.zeros((MOE_ROWS, D), F32))


def _expert_kernel(te_ref, nu_ref, xs_ref, wg_ref, wu_ref, wo_ref, o_ref, x_scr, acc):
    i = pl.program_id(0)
    f = pl.program_id(1)
    used = i < nu_ref[0]

    @pl.when(used & (f == 0))
    def _():
        x_scr[...] = xs_ref[...].astype(BF16)
        acc[...] = jnp.zeros_like(acc)

    @pl.when(used)
    def _():
        x = x_scr[...]
        act = _silu(_dg(x, wg_ref[...])) * _dg(x, wu_ref[...])
        acc[...] += _dg(act.astype(BF16), wo_ref[...])

    last = f == pl.num_programs(1) - 1

    @pl.when(used & last)
    def _():
        o_ref[...] = acc[...]

    @pl.when(jnp.logical_not(used) & last)
    def _():
        o_ref[...] = jnp.zeros_like(o_ref)


def moe_experts(xs, tile_expert, n_used, w_in_bf16, w_out_bf16):
    tn = E_FF // 4
    nf = E_FF // tn
    live = lambda i, f, nu: jnp.where(i < nu[0], f, 0)
    return pl.pallas_call(
        _expert_kernel,
        grid_spec=pltpu.PrefetchScalarGridSpec(
            num_scalar_prefetch=2,
            grid=(MOE_TILES, nf),
            in_specs=[pl.BlockSpec((MOE_TILE, D), lambda i, f, te, nu: (i, 0)),
                      pl.BlockSpec((None, D, tn), lambda i, f, te, nu: (te[i], 0, live(i, f, nu))),
                      pl.BlockSpec((None, D, tn), lambda i, f, te, nu: (te[i], 0, live(i, f, nu) + nf)),
                      pl.BlockSpec((None, tn, D), lambda i, f, te, nu: (te[i], live(i, f, nu), 0))],
            out_specs=pl.BlockSpec((MOE_TILE, D), lambda i, f, te, nu: (i, 0)),
            scratch_shapes=[pltpu.VMEM((MOE_TILE, D), BF16), pltpu.VMEM((MOE_TILE, D), F32)]),
        out_shape=jax.ShapeDtypeStruct((MOE_ROWS, D), F32),
        compiler_params=_cparams("arbitrary", "arbitrary"),
    )(tile_expert, n_used, xs, w_in_bf16, w_in_bf16, w_out_bf16)


def _combine_kernel(pos_ref, ys_ref, gates_ref, x_ref, gate_ref, fg_ref, o_ref, y_scr, sem):
    tm = x_ref.shape[0]
    base = pl.program_id(0) * tm

    def issue(r, carry):
        for k in range(2):
            _row_copy(ys_ref, pos_ref[(base + r) * 2 + k], y_scr.at[k], r, sem).start()
        return carry

    lax.fori_loop(0, tm, issue, 0, unroll=8)
    for k in range(2):
        pltpu.make_async_copy(ys_ref.at[pl.ds(0, tm)], y_scr.at[k], sem).wait()
    gates = gates_ref[...]
    lane = lax.broadcasted_iota(jnp.int32, gates.shape, 1)
    g0 = jnp.sum(jnp.where(lane == 0, gates, 0.0), -1, keepdims=True)
    g1 = jnp.sum(jnp.where(lane == 1, gates, 0.0), -1, keepdims=True)
    xn = x_ref[...] + gate_ref[...] * (y_scr[0] * g0 + y_scr[1] * g1)
    y = xn * lax.rsqrt(jnp.mean(xn * xn, -1, keepdims=True) + NORM_EPS)
    o_ref[...] = y * fg_ref[...]


def moe_combine_final(ys, pos, gates, x, gate_t, final_g):
    tm = 256
    per = TM // tm
    return pl.pallas_call(
        _combine_kernel,
        grid_spec=pltpu.PrefetchScalarGridSpec(
            num_scalar_prefetch=1,
            grid=(T_ALL // tm,),
            in_specs=[pl.BlockSpec(memory_space=pl.ANY),
                      pl.BlockSpec((tm, LANES), lambda i, pos: (i, 0)),
                      pl.BlockSpec((tm, D), lambda i, pos: (i, 0)),
                      pl.BlockSpec((None, 1, D), lambda i, pos: (i // per, 0, 0)),
                      pl.BlockSpec((1, D), lambda i, pos: (0, 0))],
            out_specs=pl.BlockSpec((tm, D), lambda i, pos: (i, 0)),
            scratch_shapes=[pltpu.VMEM((2, tm, D), F32), pltpu.SemaphoreType.DMA]),
        out_shape=jax.ShapeDtypeStruct((T_ALL, D), F32),
        compiler_params=_cparams("arbitrary"),
    )(pos, ys, gates, x, gate_t, final_g.reshape(1, D))


def _tile_rows(mod_l, k):
    cols = mod_l[:, k * D:(k + 1) * D]
    ctx = jnp.broadcast_to(cols[0:1], (T_CTX // TM, D))
    den = jnp.repeat(cols[1:1 + B_DEN], L_DEN // TM, axis=0)
    return jnp.concatenate([ctx, den], 0)[:, None, :]


def _pad_cols(w, n):
    return jnp.pad(w, ((0, 0), (0, n - w.shape[1])))


def kernel(x_prompt, x_sample, state_rwkv, state_gdn, c, c_ctx, ada_w, ada_b, norm_mix_g, norm_ffn_g, final_norm_g, e_w_in, e_hy_conv_w, e_hy_conv_b, e_hf_w1, e_hf_b1, e_hf_freq1, e_hf_w2, e_hf_b2, e_hf_freq2, e_hf_w3, e_hy_bias, e_rw_mu, e_rw_w0, e_rw_w2, e_rw_a0, e_rw_a2, e_rw_g2, e_rw_kk, e_rw_ka, e_rw_rk, e_rw_ln_w, e_rw_ln_b, e_w_out, e_ffn_w_in, e_ffn_w_out, o_w_in, o_conv_w, o_A_log, o_dt_bias, o_norm_g, o_w_out, o_router, o_moe_w_in, o_moe_w_out):
    passes = 1
    cond16 = jnp.zeros((16, D), F32).at[0].set(c_ctx).at[1:1 + B_DEN].set(c)
    mod = modulation(cond16, ada_w, ada_b)
    x = assemble_tokens(x_prompt, x_sample)

    m0 = [_tile_rows(mod[0], k) for k in range(6)]
    p = norm_matmul(x, norm_mix_g[0], m0[1], m0[0], _pad_cols(e_w_in[0], P_EVEN_PAD).astype(BF16),
                    P_EVEN_PAD // 2, BF16)
    ys_hy, ys_rw, st_rw = [], [], None
    for row0, nseq, L in ((0, B_CTX, L_CTX), (T_CTX, B_DEN, L_DEN)):
        spec = hyena_filter_spectrum(L, e_hf_w1[0], e_hf_b1[0], e_hf_freq1[0], e_hf_w2[0], e_hf_b2[0],
                                     e_hf_freq2[0], e_hf_w3[0])
        ys_hy.append(hyena_mix(p, row0, nseq, L, e_hy_conv_w[0], e_hy_conv_b[0].reshape(1, -1), spec,
                               e_hy_bias[0], 1))
        r, v, kk, lw, kd, bd, bonus, g = rwkv_prep(p, row0, nseq, L, e_rw_mu[0], e_rw_w0[0], e_rw_w2[0],
                                                   e_rw_a0[0], e_rw_a2[0], e_rw_g2[0], e_rw_kk[0], e_rw_ka[0],
                                                   e_rw_rk[0])
        if row0 == 0:
            s0 = jnp.zeros((nseq, 2, 4, LANES, LANES), F32)
        else:
            s0 = _rw_states_to_blockdiag(state_rwkv[:, 0])
        y2, s_new = rwkv_scan(r, v, kk, lw, kd, bd, s0, nseq, L, passes, 4)
        if row0 == 0:
            st_rw = _rw_states_from_blockdiag(s_new)
        ys_rw.append(rwkv_post(y2, bonus, g, e_rw_ln_w[0], e_rw_ln_b[0]))
    y_mix = jnp.concatenate([jnp.concatenate(ys_hy, 0), jnp.concatenate(ys_rw, 0)], 1)
    x = matmul_residual(y_mix, e_w_out[0].astype(BF16), x, m0[2], D)
    act = norm_swiglu(x, norm_ffn_g[0], m0[4], m0[3], e_ffn_w_in[0].astype(BF16), D_FF, D_FF // 2)
    x = matmul_residual(act, e_ffn_w_out[0].astype(BF16), x, m0[5], D // 2)

    m1 = [_tile_rows(mod[1], k) for k in range(6)]
    p = norm_matmul(x, norm_mix_g[1], m1[1], m1[0], _pad_cols(o_w_in[0], P_ODD_PAD).astype(BF16),
                    P_ODD_PAD // 3, BF16)
    os_, st_gd = [], None
    for row0, nseq, L in ((0, B_CTX, L_CTX), (T_CTX, B_DEN, L_DEN)):
        if row0 == 0:
            s0 = jnp.zeros((nseq, 2, GD_H, LANES, LANES), F32)
        else:
            s0 = state_gdn[:, 0]
        o, s_new = gdn_mix(p, row0, nseq, L, o_conv_w[0], o_A_log[0], o_dt_bias[0], o_norm_g[0], s0, passes,
                           8 if L == L_CTX else 4)
        if row0 == 0:
            st_gd = s_new
        os_.append(o)
    x = matmul_residual(jnp.concatenate(os_, 0), o_w_out[0].astype(BF16), x, m1[2], D)
    h, gates, idx = moe_router(x, norm_ffn_g[1], m1[4], m1[3], o_router[0])
    pos, tile_expert, n_used = moe_slot_positions(idx)
    ys = moe_experts(moe_dispatch(h, pos), tile_expert, n_used, o_moe_w_in[0].astype(BF16),
                     o_moe_w_out[0].astype(BF16))
    y = moe_combine_final(ys, pos, gates, x, m1[5], final_norm_g)

    y_prompt = y[:T_CTX].reshape(B_CTX, L_CTX, D)
    y_sample = y[T_CTX:].reshape(B_DEN, L_DEN, D)
    return (y_prompt, y_sample, st_rw[:, None], st_gd[:, None])
```

```python
import functools
import math

import numpy as np
import jax
import jax.numpy as jnp
from jax import lax
from jax.experimental import pallas as pl
from jax.experimental.pallas import tpu as pltpu

F32 = jnp.float32
BF16 = jnp.bfloat16

D = 1024
B_CTX, L_CTX = 32, 256
B_DEN, L_DEN = 8, 1024
T_CTX = B_CTX * L_CTX
T_DEN = B_DEN * L_DEN
T_ALL = T_CTX + T_DEN
GRID_W = 64
NORM_EPS = 1e-6

HY_W = 512
HY_EMB = 33
HY_BANDS = 16
HY_FFN = 64
HY_TARGET, HY_FAST, HY_SLOW = 1e-2, 0.3, 1.5

RW_W = 512
RW_N = 64
RW_H = 8
RW_LORA = 224
RW_LN_EPS = 64e-5
P_EVEN = 3 * HY_W + 3 * RW_W + RW_LORA
P_EVEN_PAD = 3328

GD_H = 8
GD_DK = 128
GD_QKV = 3072
P_ODD = 4128
P_ODD_PAD = 4224

D_FF = 2816
N_EXP = 8
E_FF = 3584
MOE_TILE = 1024
MOE_ROWS = 2 * T_ALL + N_EXP * MOE_TILE
MOE_TILES = MOE_ROWS // MOE_TILE

LANES = 128
TM = 1024
RW_CHUNK = 32
GD_CHUNK = 64
SUPER = 128
VMEM_LIMIT = 56 * 1024 * 1024

_NN = (((1,), (0,)), ((), ()))
_NT = (((1,), (1,)), ((), ()))
_TN = (((0,), (0,)), ((), ()))


def _cparams(*sem):
    return pltpu.CompilerParams(dimension_semantics=sem, vmem_limit_bytes=VMEM_LIMIT)


def _dg(a, b, dims=_NN):
    return lax.dot_general(a, b, dims, preferred_element_type=F32)


def _split2(x):
    hi = x.astype(BF16)
    lo = (x - hi.astype(F32)).astype(BF16)
    return hi, lo


def _split3(x):
    x0 = x.astype(BF16)
    r1 = x - x0.astype(F32)
    x1 = r1.astype(BF16)
    x2 = (r1 - x1.astype(F32)).astype(BF16)
    return x0, x1, x2


def _dot(a, b, passes=1, dims=_NN):
    if passes == 1:
        return _dg(a.astype(BF16), b.astype(BF16), dims)
    if passes == 3:
        ah, al = _split2(a)
        bh, bl = _split2(b)
        return _dg(ah, bh, dims) + (_dg(ah, bl, dims) + _dg(al, bh, dims))
    a0, a1, a2 = _split3(a)
    b0, b1, b2 = _split3(b)
    small = _dg(a0, b2, dims) + _dg(a1, b1, dims) + _dg(a2, b0, dims)
    mid = _dg(a0, b1, dims) + _dg(a1, b0, dims)
    return _dg(a0, b0, dims) + (mid + small)


def _dot_exact_l(m, x, dims=_NN):
    x0, x1, x2 = _split3(x)
    return _dg(m, x0, dims) + (_dg(m, x1, dims) + _dg(m, x2, dims))


def _dot_exact_r(x, m, dims=_NN):
    x0, x1, x2 = _split3(x)
    return _dg(x0, m, dims) + (_dg(x1, m, dims) + _dg(x2, m, dims))


def _sigmoid(x):
    return 1.0 / (1.0 + jnp.exp(-x))


def _silu(x):
    return x * _sigmoid(x)


def _softplus(x):
    return jnp.maximum(x, 0.0) + jnp.log(1.0 + jnp.exp(-jnp.abs(x)))


def _shift_rows(x):
    n = x.shape[0]
    row = lax.broadcasted_iota(jnp.int32, x.shape, 0)
    prev = jnp.where(row == 0, 0.0, pltpu.roll(x, 1, 0))
    nxt = jnp.where(row == n - 1, 0.0, pltpu.roll(x, n - 1, 0))
    return prev, nxt


def _conv3(x, w, b=None):
    prev, nxt = _shift_rows(x)
    y = prev * w[0:1] + x * w[1:2] + nxt * w[2:3]
    return y if b is None else y + b


def _chunk_masks(n, chunk, fwd):
    row = lax.broadcasted_iota(jnp.int32, (n, n), 0)
    col = lax.broadcasted_iota(jnp.int32, (n, n), 1)
    sh = int(math.log2(chunk))
    same = (row >> sh) == (col >> sh)
    before = (col < row) if fwd else (col > row)
    pair = (row >> 1) == (col >> 1)
    joins = [((row >> (lvl + 1)) == (col >> (lvl + 1))) & ((row >> lvl) != (col >> lvl))
             for lvl in range(1, sh)]
    incl = same & (before | (row == col))
    return dict(strict=same & before, incl=incl, eye=jnp.where(row == col, 1.0, 0.0), pair=pair, joins=joins,
                incl_bf=_mask_bf16(incl))


def _mask_bf16(m):
    return jnp.where(m, 1.0, 0.0).astype(BF16)


def _chunk_totals(cum, chunk, fwd):
    n = cum.shape[0]
    rows = [cum[(ci + 1) * chunk - 1:(ci + 1) * chunk] if fwd else cum[ci * chunk:ci * chunk + 1]
            for ci in range(n // chunk)]
    return jnp.concatenate([jnp.broadcast_to(r, (chunk, cum.shape[1])) for r in rows], 0)


def _tri_inv(xs, masks, passes):
    ts = [m["eye"] + jnp.where(m["pair"], x, 0.0) for x, m in zip(xs, masks)]
    for lvl in range(len(masks[0]["joins"])):
        ps = [_dot(jnp.where(m["joins"][lvl], x, 0.0), t, passes) for x, m, t in zip(xs, masks, ts)]
        ts = [t + _dot(t, p, passes) for t, p in zip(ts, ps)]
    return ts


@functools.lru_cache(maxsize=None)
def _pos_table():
    t = np.arange(L_DEN)
    row = (t // GRID_W).astype(np.float32)
    col = (t % GRID_W).astype(np.float32)
    q = D // 4
    omega = np.exp(-math.log(10000.0) * np.arange(q, dtype=np.float32) / q).astype(np.float32)
    enc = lambda pos: np.concatenate([np.sin(pos[:, None] * omega), np.cos(pos[:, None] * omega)], -1)
    return np.concatenate([enc(row), enc(col)], -1).astype(np.float32)


@functools.lru_cache(maxsize=None)
def _dft_tables(L):
    f = np.arange(L, dtype=np.int64)
    m = (f[:, None] * f[None, :]) % (2 * L)
    ang = np.pi * m.astype(np.float64) / L
    return np.cos(ang).astype(np.float32), np.sin(ang).astype(np.float32)


@functools.lru_cache(maxsize=None)
def _hyena_static(L):
    k = np.arange(L, dtype=np.float32)
    t = k / np.float32(L - 1)
    bands = np.linspace(1e-4, HY_BANDS - 1, HY_BANDS, dtype=np.float32)
    ang = (np.float32(2.0 * math.pi) * k / np.float32(L))[:, None] * bands[None, :]
    feats = np.concatenate([t[:, None], np.cos(ang), -np.sin(ang)], -1).astype(np.float32)
    feats_p = np.zeros((L, LANES), np.float32)
    feats_p[:, :HY_EMB] = feats
    deltas = np.abs(np.linspace(math.log(HY_TARGET) / HY_FAST, math.log(HY_TARGET) / HY_SLOW, HY_W,
                                dtype=np.float32))
    window = np.exp(-t[:, None] * deltas[None, :]).astype(np.float32)
    return feats_p, window


def _block_ones(n, blk):
    i = np.arange(n) // blk
    return (i[:, None] == i[None, :]).astype(np.float32)


def _mod_kernel(c_ref, w_ref, b_ref, o_ref):
    o_ref[...] = _dot(_silu(c_ref[...]), w_ref[...], 6) + b_ref[...]


def modulation(cond16, ada_w, ada_b):
    depth = ada_w.shape[0]
    tn = 1024
    return pl.pallas_call(
        _mod_kernel,
        grid=(depth, 6 * D // tn),
        in_specs=[pl.BlockSpec((16, D), lambda i, j: (0, 0)),
                  pl.BlockSpec((None, D, tn), lambda i, j: (i, 0, j)),
                  pl.BlockSpec((None, 1, tn), lambda i, j: (i, 0, j))],
        out_specs=pl.BlockSpec((None, 16, tn), lambda i, j: (i, 0, j)),
        out_shape=jax.ShapeDtypeStruct((depth, 16, 6 * D), F32),
        compiler_params=_cparams("arbitrary", "arbitrary"),
    )(cond16, ada_w, ada_b.reshape(depth, 1, 6 * D))


def _assemble_kernel(xp_ref, xs_ref, pos_ref, o_ref):
    i = pl.program_id(0)

    @pl.when(i < T_CTX // 256)
    def _():
        o_ref[...] = xp_ref[...]

    @pl.when(i >= T_CTX // 256)
    def _():
        o_ref[...] = xs_ref[...] + pos_ref[...]


def assemble_tokens(x_prompt, x_sample):
    nc = T_CTX // 256
    pos = jnp.asarray(_pos_table())
    return pl.pallas_call(
        _assemble_kernel,
        grid=(T_ALL // 256,),
        in_specs=[pl.BlockSpec((256, D), lambda i: (jnp.minimum(i, nc - 1), 0)),
                  pl.BlockSpec((256, D), lambda i: (jnp.maximum(i - nc, 0), 0)),
                  pl.BlockSpec((256, D), lambda i: (jnp.maximum(i - nc, 0) % (L_DEN // 256), 0))],
        out_specs=pl.BlockSpec((256, D), lambda i: (i, 0)),
        out_shape=jax.ShapeDtypeStruct((T_ALL, D), F32),
        compiler_params=_cparams("arbitrary"),
    )(x_prompt.reshape(T_CTX, D), x_sample.reshape(T_DEN, D), pos)


def _norm_mod(x, g, sc, sh):
    y = x * lax.rsqrt(jnp.mean(x * x, -1, keepdims=True) + NORM_EPS)
    return (y * g) * (1.0 + sc) + sh


def _norm_mm_kernel(x_ref, g_ref, sc_ref, sh_ref, w_ref, o_ref, h_scr):
    @pl.when(pl.program_id(1) == 0)
    def _():
        h_scr[...] = _norm_mod(x_ref[...], g_ref[...], sc_ref[...], sh_ref[...]).astype(BF16)

    o_ref[...] = _dg(h_scr[...], w_ref[...]).astype(o_ref.dtype)


def norm_matmul(x, g, sc_t, sh_t, w_bf16, tn, out_dtype=F32):
    n = w_bf16.shape[1]
    return pl.pallas_call(
        _norm_mm_kernel,
        grid=(T_ALL // TM, n // tn),
        in_specs=[pl.BlockSpec((TM, D), lambda i, j: (i, 0)),
                  pl.BlockSpec((1, D), lambda i, j: (0, 0)),
                  pl.BlockSpec((None, 1, D), lambda i, j: (i, 0, 0)),
                  pl.BlockSpec((None, 1, D), lambda i, j: (i, 0, 0)),
                  pl.BlockSpec((D, tn), lambda i, j: (0, j))],
        out_specs=pl.BlockSpec((TM, tn), lambda i, j: (i, j)),
        out_shape=jax.ShapeDtypeStruct((T_ALL, n), out_dtype),
        scratch_shapes=[pltpu.VMEM((TM, D), BF16)],
        compiler_params=_cparams("arbitrary", "arbitrary"),
    )(x, g.reshape(1, D), sc_t, sh_t, w_bf16)


def _norm_swiglu_kernel(x_ref, g_ref, sc_ref, sh_ref, wg_ref, wu_ref, o_ref, h_scr):
    @pl.when(pl.program_id(1) == 0)
    def _():
        h_scr[...] = _norm_mod(x_ref[...], g_ref[...], sc_ref[...], sh_ref[...]).astype(BF16)

    h = h_scr[...]
    o_ref[...] = (_silu(_dg(h, wg_ref[...])) * _dg(h, wu_ref[...])).astype(o_ref.dtype)


def norm_swiglu(x, g, sc_t, sh_t, w_in_bf16, dff, tn):
    nj = dff // tn
    return pl.pallas_call(
        _norm_swiglu_kernel,
        grid=(T_ALL // TM, nj),
        in_specs=[pl.BlockSpec((TM, D), lambda i, j: (i, 0)),
                  pl.BlockSpec((1, D), lambda i, j: (0, 0)),
                  pl.BlockSpec((None, 1, D), lambda i, j: (i, 0, 0)),
                  pl.BlockSpec((None, 1, D), lambda i, j: (i, 0, 0)),
                  pl.BlockSpec((D, tn), lambda i, j: (0, j)),
                  pl.BlockSpec((D, tn), lambda i, j: (0, j + nj))],
        out_specs=pl.BlockSpec((TM, tn), lambda i, j: (i, j)),
        out_shape=jax.ShapeDtypeStruct((T_ALL, dff), BF16),
        scratch_shapes=[pltpu.VMEM((TM, D), BF16)],
        compiler_params=_cparams("arbitrary", "arbitrary"),
    )(x, g.reshape(1, D), sc_t, sh_t, w_in_bf16, w_in_bf16)


def _mm_res_kernel(y_ref, w_ref, x_ref, gate_ref, o_ref):
    o_ref[...] = x_ref[...] + gate_ref[...] * _dg(y_ref[...], w_ref[...])


def matmul_residual(y_bf16, w_bf16, x, gate_t, tn):
    k = y_bf16.shape[1]
    return pl.pallas_call(
        _mm_res_kernel,
        grid=(T_ALL // TM, D // tn),
        in_specs=[pl.BlockSpec((TM, k), lambda i, j: (i, 0)),
                  pl.BlockSpec((k, tn), lambda i, j: (0, j)),
                  pl.BlockSpec((TM, tn), lambda i, j: (i, j)),
                  pl.BlockSpec((None, 1, tn), lambda i, j: (i, 0, j))],
        out_specs=pl.BlockSpec((TM, tn), lambda i, j: (i, j)),
        out_shape=jax.ShapeDtypeStruct((T_ALL, D), F32),
        compiler_params=_cparams("arbitrary", "arbitrary"),
    )(y_bf16, w_bf16, x, gate_t)


def _hy_filter_kernel(feat_ref, w1_ref, b1_ref, f1_ref, w2_ref, b2_ref, f2_ref, w3f_ref, w3b_ref, win_ref,
                      c_ref, s_ref, hr_ref, hi_ref, hn_ref):
    L = feat_ref.shape[0]
    h = jnp.sin(f1_ref[...] * (_dot(feat_ref[...], w1_ref[...], 6) + b1_ref[...]))
    h = jnp.sin(f2_ref[...] * (_dot(h, w2_ref[...], 6) + b2_ref[...]))
    win = win_ref[...]
    fw = _dot(h, w3f_ref[...], 6) * win
    bw = _dot(h, w3b_ref[...], 6) * win
    row = lax.broadcasted_iota(jnp.int32, fw.shape, 0)
    bw = jnp.where(row == 0, 0.0, bw)
    nrm = jnp.sum(jnp.abs(fw), 0, keepdims=True) + jnp.sum(jnp.abs(bw), 0, keepdims=True)
    ev = (fw + bw) / nrm
    od = (bw - fw) / nrm
    alt = (1 - 2 * (row & 1)).astype(F32)
    hr_ref[...] = _dot(c_ref[...], ev, 3)
    hi_ref[...] = _dot(s_ref[...], od, 3)
    hn_ref[...] = jnp.broadcast_to(jnp.sum(ev * alt, 0, keepdims=True), (8, ev.shape[1]))
    del L


def hyena_filter_spectrum(L, w1, b1, f1, w2, b2, f2, w3):
    feats, window = _hyena_static(L)
    cos_t, sin_t = _dft_tables(L)
    tc = 128
    ncb = HY_W // tc
    w1p = jnp.zeros((LANES, HY_FFN), F32).at[:HY_EMB].set(w1)
    const = lambda shape: pl.BlockSpec(shape, lambda o, c: (0,) * len(shape))
    return pl.pallas_call(
        _hy_filter_kernel,
        grid=(2, ncb),
        in_specs=[const((L, LANES)), const((LANES, HY_FFN)), const((1, HY_FFN)), const((1, HY_FFN)),
                  const((HY_FFN, HY_FFN)), const((1, HY_FFN)), const((1, HY_FFN)),
                  pl.BlockSpec((HY_FFN, tc), lambda o, c: (0, o * ncb + c)),
                  pl.BlockSpec((HY_FFN, tc), lambda o, c: (0, 2 * ncb + o * ncb + c)),
                  pl.BlockSpec((L, tc), lambda o, c: (0, c)),
                  const((L, L)), const((L, L))],
        out_specs=[pl.BlockSpec((None, L, tc), lambda o, c: (o, 0, c)),
                   pl.BlockSpec((None, L, tc), lambda o, c: (o, 0, c)),
                   pl.BlockSpec((None, 8, tc), lambda o, c: (o, 0, c))],
        out_shape=[jax.ShapeDtypeStruct((2, L, HY_W), F32), jax.ShapeDtypeStruct((2, L, HY_W), F32),
                   jax.ShapeDtypeStruct((2, 8, HY_W), F32)],
        compiler_params=_cparams("arbitrary", "arbitrary"),
    )(jnp.asarray(feats), w1p, b1.reshape(1, -1), f1.reshape(1, -1), w2, b2.reshape(1, -1), f2.reshape(1, -1),
      w3, w3, jnp.asarray(window), jnp.asarray(cos_t), jnp.asarray(sin_t))


def _hy_mix_kernel(pv_ref, p1_ref, p2_ref, wv_ref, w1_ref, w2_ref, bv_ref, b1_ref, b2_ref,
                   hr_ref, hi_ref, hn_ref, bias_ref, ch_ref, cl_ref, sh_ref, sl_ref, o_ref, *, passes):
    L = pv_ref.shape[0]
    z = _conv3(pv_ref[...].astype(F32), wv_ref[...], bv_ref[...])
    gates = (_conv3(p1_ref[...].astype(F32), w1_ref[...], b1_ref[...]),
             _conv3(p2_ref[...].astype(F32), w2_ref[...], b2_ref[...]))
    row = lax.broadcasted_iota(jnp.int32, z.shape, 0)
    alt = (1 - 2 * (row & 1)).astype(F32)
    ch, cl, sh, sl = ch_ref[...], cl_ref[...], sh_ref[...], sl_ref[...]

    def tdot(th, tl, x):
        if passes == 1:
            return _dg(th, x.astype(BF16))
        xh, xl = _split2(x)
        return _dg(th, xh) + (_dg(th, xl) + _dg(tl, xh))

    inv_l = 1.0 / L
    for o in range(2):
        hr, hi, hn = hr_ref[o], hi_ref[o], hn_ref[o][0:1]
        zc = tdot(ch, cl, z)
        zs = tdot(sh, sl, z)
        zn = jnp.sum(z * alt, 0, keepdims=True)
        yr = zc * hr + zs * hi
        yi = zc * hi - zs * hr
        wr = jnp.where(row == 0, 0.5 * inv_l, inv_l)
        conv = tdot(ch, cl, yr * wr) - tdot(sh, sl, yi * inv_l) + alt * (zn * hn * (0.5 * inv_l))
        z = gates[o] * (conv + z * bias_ref[o:o + 1])
    o_ref[...] = z.astype(o_ref.dtype)


def hyena_mix(p, row0, nseq, L, conv_w, conv_b, spec, bias, passes):
    hr, hi, hn = spec
    cos_t, sin_t = _dft_tables(L)
    ch, cl = _np_split2(cos_t)
    sh, sl = _np_split2(sin_t)
    tc = 512 if L <= 256 else 256
    ncb = HY_W // tc
    b0 = row0 // L
    pspec = lambda part: pl.BlockSpec((L, tc), lambda b, c: (b0 + b, part * ncb + c))
    wspec = lambda part: pl.BlockSpec((3, tc), lambda b, c: (0, part * ncb + c))
    bspec = lambda part: pl.BlockSpec((1, tc), lambda b, c: (0, part * ncb + c))
    hspec = lambda rows: pl.BlockSpec((2, rows, tc), lambda b, c: (0, 0, c))
    tab = pl.BlockSpec((L, L), lambda b, c: (0, 0))
    return pl.pallas_call(
        functools.partial(_hy_mix_kernel, passes=passes),
        grid=(nseq, ncb),
        in_specs=[pspec(0), pspec(1), pspec(2), wspec(0), wspec(1), wspec(2), bspec(0), bspec(1), bspec(2),
                  hspec(L), hspec(L), hspec(8), pl.BlockSpec((2, tc), lambda b, c: (0, c)), tab, tab, tab, tab],
        out_specs=pl.BlockSpec((L, tc), lambda b, c: (b, c)),
        out_shape=jax.ShapeDtypeStruct((nseq * L, HY_W), BF16),
        compiler_params=_cparams("arbitrary", "arbitrary"),
    )(p, p, p, conv_w, conv_w, conv_w, conv_b, conv_b, conv_b, hr, hi, hn, bias,
      jnp.asarray(ch), jnp.asarray(cl), jnp.asarray(sh), jnp.asarray(sl))


def _np_split2(x):
    hi = x.astype(jnp.bfloat16)
    lo = (x - hi.astype(np.float32)).astype(jnp.bfloat16)
    return hi, lo


def _rw_prep_kernel(pr_ref, pk_ref, pv_ref, pl_ref, mur_ref, muk_ref, muv_ref, mul_ref, w0_ref, a0_ref,
                    w2_ref, a2_ref, g2_ref, kkw_ref, kaw_ref, rkw_ref, ones_ref,
                    r_ref, v_ref, kk_ref, lw_ref, kd_ref, bd_ref, bon_ref, g_ref):
    def shift(p, mu):
        prev, nxt = _shift_rows(p)
        return p + (0.5 * (prev + nxt) - p) * mu

    r = shift(pr_ref[...].astype(F32), mur_ref[...])
    k = shift(pk_ref[...].astype(F32), muk_ref[...])
    v = shift(pv_ref[...].astype(F32), muv_ref[...])
    lo = shift(pl_ref[...].astype(F32), mul_ref[...])
    ones = ones_ref[...]
    g_ref[...] = _dot(_sigmoid(lo), g2_ref[...], 1)
    kkr = k * kkw_ref[...]
    kk = kkr / jnp.maximum(jnp.sqrt(_dot_exact_r(kkr * kkr, ones)), 1e-12)
    th = jnp.tanh(lo)
    bon = jnp.zeros_like(r)
    for d in range(2):
        w = -_softplus(-(w0_ref[d:d + 1] + _dot(th, w2_ref[d], 3))) - 0.5
        lw_ref[d] = -jnp.exp(w)
        a = _sigmoid(a0_ref[d:d + 1] + _dot(lo, a2_ref[d], 1))
        kd = k * (1.0 + (a - 1.0) * kaw_ref[...])
        kd_ref[d] = kd
        bd_ref[d] = kk * a
        bon = bon + _dot_exact_r(r * kd * rkw_ref[...], ones) * v
    r_ref[...] = r
    v_ref[...] = v
    kk_ref[...] = kk
    bon_ref[...] = bon


def rwkv_prep(p, row0, nseq, L, mu, w0, w2, a0, a2, g2, k_k, k_a, r_k):
    b0 = row0 // L
    cb = RW_W if L <= 256 else RW_W // 2
    ncb = RW_W // cb
    c0 = 3 * HY_W // cb
    wide = lambda part: pl.BlockSpec((L, cb), lambda b, c: (b0 + b, c0 + part * ncb + c))
    lora = pl.BlockSpec((L, 256), lambda b, c: (b0 + b, (3 * HY_W + 3 * RW_W) // 256))
    muw = lambda part: pl.BlockSpec((1, cb), lambda b, c: (0, part * ncb + c))
    vec = lambda rows: pl.BlockSpec((rows, cb), lambda b, c: (0, c))
    mu_p = jnp.zeros((1, 3 * RW_W + 256), F32).at[0, :3 * RW_W + RW_LORA].set(mu)
    w2f = jnp.zeros((2, 256, RW_W), F32).at[0, 0:32].set(w2[0]).at[1, 32:64].set(w2[1])
    a2f = jnp.zeros((2, 256, RW_W), F32).at[0, 64:96].set(a2[0]).at[1, 96:128].set(a2[1])
    g2f = jnp.zeros((256, RW_W), F32).at[128:224].set(g2)
    n = nseq * L
    one = jax.ShapeDtypeStruct((n, RW_W), F32)
    two = jax.ShapeDtypeStruct((2, n, RW_W), F32)
    ospec1 = pl.BlockSpec((L, cb), lambda b, c: (b, c))
    ospec2 = pl.BlockSpec((2, L, cb), lambda b, c: (0, b, c))
    return pl.pallas_call(
        _rw_prep_kernel,
        grid=(nseq, ncb),
        in_specs=[wide(0), wide(1), wide(2), lora, muw(0), muw(1), muw(2),
                  pl.BlockSpec((1, 256), lambda b, c: (0, 3 * RW_W // 256)),
                  vec(2), vec(2),
                  pl.BlockSpec((2, 256, cb), lambda b, c: (0, 0, c)),
                  pl.BlockSpec((2, 256, cb), lambda b, c: (0, 0, c)),
                  pl.BlockSpec((256, cb), lambda b, c: (0, c)),
                  vec(1), vec(1), vec(1),
                  pl.BlockSpec((cb, cb), lambda b, c: (0, 0))],
        out_specs=[ospec1, ospec1, ospec1, ospec2, ospec2, ospec2, ospec1, ospec1],
        out_shape=[one, one, one, two, two, two, one, one],
        compiler_params=_cparams("arbitrary", "arbitrary"),
    )(p, p, p, p, mu_p, mu_p, mu_p, mu_p, w0, a0, w2f, a2f, g2f, k_k.reshape(1, RW_W), k_a.reshape(1, RW_W),
      r_k.reshape(1, RW_W), jnp.asarray(_block_ones(cb, RW_N)).astype(BF16))


def _rw_blocks(ch, passes):
    n = SUPER
    c = RW_CHUNK
    nc = n // c
    idx = range(len(ch))
    cum = [_dot_exact_l(x["masks"]["incl_bf"], x["lw"]) for x in ch]
    tot = [_chunk_totals(cum[i], c, ch[i]["fwd"]) for i in idx]
    suf = [tot[i] - cum[i] for i in idx]
    e_neg =[jnp.exp(-cum[i]) for i in idx]
    at = [ch[i]["a"] * jnp.exp(cum[i] - ch[i]["lw"]) for i in idx]
    rt = [ch[i]["r"] * jnp.exp(cum[i]) for i in idx]
    bk = [jnp.concatenate([ch[i]["b"] * e_neg[i], ch[i]["k"] * e_neg[i]], 0) for i in idx]
    e_suf = [jnp.exp(suf[i]) for i in idx]
    bp = [ch[i]["b"] * e_suf[i] for i in idx]
    kp = [ch[i]["k"] * e_suf[i] for i in idx]
    lane = lax.broadcasted_iota(jnp.int32, (1, LANES), 1)
    heads = range(LANES // RW_N)
    sub = [(i, g) for i in idx for g in heads]
    mg = [(lane >> 6) == g for g in heads]
    at_g = [jnp.where(mg[g], at[i], 0.0) for i, g in sub]
    v_g = [jnp.where(mg[g], ch[i]["v"], 0.0) for i, g in sub]
    m = [_dot(jnp.concatenate([at_g[j], jnp.where(mg[g], rt[i], 0.0)], 0), bk[i], passes, _NT)
         for j, (i, g) in enumerate(sub)]
    smask = [ch[i]["masks"] for i, g in sub]
    ab = [jnp.where(smask[j]["strict"], m[j][:n, :n], 0.0) for j in range(len(sub))]
    ak = [jnp.where(smask[j]["strict"], m[j][:n, n:], 0.0) for j in range(len(sub))]
    rb = [jnp.where(smask[j]["incl"], m[j][n:, :n], 0.0) for j in range(len(sub))]
    rk = [jnp.where(smask[j]["incl"], m[j][n:, n:], 0.0) for j in range(len(sub))]
    tinv = _tri_inv(ab, smask, passes)
    akv = [_dot(ak[j], v_g[j], passes) for j in range(len(sub))]
    aw = [_dot(tinv[j], jnp.concatenate([at_g[j], akv[j]], 1), passes) for j in range(len(sub))]
    ry = [_dot(rb[j], aw[j], passes) for j in range(len(sub))]
    rkv = [_dot(rk[j], v_g[j], passes) for j in range(len(sub))]
    nh = len(heads)
    ahat = [sum(aw[i * nh + g][:, :LANES] for g in heads) for i in idx]
    w1 = [sum(aw[i * nh + g][:, LANES:] for g in heads) for i in idx]
    rhat = [rt[i] + sum(ry[i * nh + g][:, :LANES] for g in heads) for i in idx]
    y0 = [sum(ry[i * nh + g][:, LANES:] + rkv[i * nh + g] for g in heads) for i in idx]
    rowl = lax.broadcasted_iota(jnp.int32, (LANES, LANES), 0)
    coll = lax.broadcasted_iota(jnp.int32, (LANES, LANES), 1)
    diag_blocks = (rowl >> 6) == (coll >> 6)
    s = [x["s"] for x in ch]
    ys = [[None] * nc for _ in ch]
    for step in range(nc):
        ci = [step if x["fwd"] else nc - 1 - step for x in ch]
        sl = [slice(ci[i] * c, (ci[i] + 1) * c) for i in idx]
        xx = [_dot(jnp.concatenate([ahat[i][sl[i]], rhat[i][sl[i]]], 0), s[i], passes, _NT) for i in idx]
        u = [w1[i][sl[i]] + xx[i][:c] for i in idx]
        for i in idx:
            ys[i][ci[i]] = y0[i][sl[i]] + xx[i][c:]
        upd = [_dot(jnp.concatenate([u[i], ch[i]["v"][sl[i]]], 0),
                    jnp.concatenate([bp[i][sl[i]], kp[i][sl[i]]], 0), passes, _TN) for i in idx]
        s = [s[i] * jnp.exp(tot[i][ci[i] * c:ci[i] * c + 1]) + jnp.where(diag_blocks, upd[i], 0.0) for i in idx]
    return [jnp.concatenate(y, 0) for y in ys], s


def _rw_scan_kernel(r_ref, v_ref, kk_ref, lw_ref, kd_ref, bd_ref, *rest, passes, gp):
    s0_ref, y_ref, s_ref = rest if len(rest) == 3 else (None,) + rest
    L = r_ref.shape[0]
    nblk = L // SUPER

    def body(i, carry):
        chains, where = [], []
        for d in range(2):
            masks = _chunk_masks(SUPER, RW_CHUNK, d == 0)
            j = i if d == 0 else nblk - 1 - i
            rows = pl.ds(pl.multiple_of(j * SUPER, SUPER), SUPER)
            for g in range(gp):
                cols = slice(g * LANES, (g + 1) * LANES)
                chains.append(dict(r=r_ref[rows, cols], lw=lw_ref[d, rows, cols], k=kd_ref[d, rows, cols],
                                   v=v_ref[rows, cols], a=-kk_ref[rows, cols], b=bd_ref[d, rows, cols],
                                   s=carry[d * gp + g], masks=masks, fwd=d == 0))
                where.append((d, rows, cols))
        ys, ss = _rw_blocks(chains, passes)
        for (d, rows, cols), y in zip(where, ys):
            y_ref[d, rows, cols] = y
        return tuple(ss)

    zero = jnp.zeros((LANES, LANES), F32)
    s_fin = lax.fori_loop(0, nblk, body,
                          tuple(zero if s0_ref is None else s0_ref[d, g] for d in range(2) for g in range(gp)))
    for d in range(2):
        for g in range(gp):
            s_ref[d, g] = s_fin[d * gp + g]


def rwkv_scan(r, v, kk, lw, kd, bd, s0_bd, nseq, L, passes, gp):
    ngrp = RW_W // LANES
    w = gp * LANES
    one = pl.BlockSpec((L, w), lambda b, g: (b, g))
    two = pl.BlockSpec((2, L, w), lambda b, g: (0, b, g))
    st = pl.BlockSpec((None, 2, gp, LANES, LANES), lambda b, g: (b, 0, g, 0, 0))
    return pl.pallas_call(
        functools.partial(_rw_scan_kernel, passes=passes, gp=gp),
        grid=(nseq, ngrp // gp),
        in_specs=[one, one, one, two, two, two] + ([] if s0_bd is None else [st]),
        out_specs=[two, st],
        out_shape=[jax.ShapeDtypeStruct((2, nseq * L, RW_W), F32),
                   jax.ShapeDtypeStruct((nseq, 2, ngrp, LANES, LANES), F32)],
        compiler_params=_cparams("arbitrary", "arbitrary"),
    )(r, v, kk, lw, kd, bd, *(() if s0_bd is None else (s0_bd,)))


def _rw_post_kernel(y_ref, bon_ref, g_ref, lnw_ref, lnb_ref, ones_ref, o_ref):
    y = y_ref[0] + y_ref[1]
    ones = ones_ref[...]
    mean = _dot_exact_r(y, ones) * (1.0 / RW_N)
    yc = y - mean
    var = _dot_exact_r(yc * yc, ones) * (1.0 / RW_N)
    yn = yc * lax.rsqrt(var + RW_LN_EPS) * lnw_ref[...] + lnb_ref[...]
    o_ref[...] = ((yn + bon_ref[...]) * g_ref[...]).astype(o_ref.dtype)


def rwkv_post(y2, bonus, g, ln_w, ln_b):
    n = bonus.shape[0]
    tm = 512
    row = pl.BlockSpec((tm, RW_W), lambda i: (i, 0))
    vec = pl.BlockSpec((1, RW_W), lambda i: (0, 0))
    return pl.pallas_call(
        _rw_post_kernel,
        grid=(n // tm,),
        in_specs=[pl.BlockSpec((2, tm, RW_W), lambda i: (0, i, 0)), row, row, vec, vec,
                  pl.BlockSpec((RW_W, RW_W), lambda i: (0, 0))],
        out_specs=row,
        out_shape=jax.ShapeDtypeStruct((n, RW_W), BF16),
        compiler_params=_cparams("arbitrary"),
    )(y2, bonus, g, ln_w.reshape(1, RW_W), ln_b.reshape(1, RW_W),
      jnp.asarray(_block_ones(RW_W, RW_N)).astype(BF16))


def _rw_states_to_blockdiag(s):
    b = s.shape[0]
    s = s.reshape(b, 2, 4, 2, RW_N, RW_N)
    z = jnp.zeros_like(s[:, :, :, 0])
    top = jnp.concatenate([s[:, :, :, 0], z], -1)
    bot = jnp.concatenate([z, s[:, :, :, 1]], -1)
    return jnp.concatenate([top, bot], -2)


def _rw_states_from_blockdiag(s):
    b = s.shape[0]
    return jnp.stack([s[:, :, :, :RW_N, :RW_N], s[:, :, :, RW_N:, RW_N:]], 3).reshape(b, 2, RW_H, RW_N, RW_N)


def _gd_blocks(ch, passes):
    n = SUPER
    c = GD_CHUNK
    nc = n // c
    idx = range(len(ch))
    msk = [x["masks"] for x in ch]
    cum = [_dot_exact_l(x["masks"]["incl_bf"], x["lw"]) for x in ch]
    tot = [_chunk_totals(cum[i], c, ch[i]["fwd"]) for i in idx]
    suf = [tot[i] - cum[i] for i in idx]
    cum_row = [cum[i].T for i in idx]
    gam = [jnp.exp(jnp.where(msk[i]["incl"], cum[i][:, 0:1] - cum_row[i], -jnp.inf)) for i in idx]
    kb = [x["k"] * x["beta"] for x in ch]
    m = [_dot(jnp.concatenate([kb[i], ch[i]["q"]], 0), ch[i]["k"], passes, _NT) for i in idx]
    a = [jnp.where(msk[i]["strict"], m[i][:n] * gam[i], 0.0) for i in idx]
    qk = [jnp.where(msk[i]["incl"], m[i][n:] * gam[i], 0.0) for i in idx]
    tinv = _tri_inv([-x for x in a], msk, passes)
    e_cum = [jnp.exp(cum[i]) for i in idx]
    uw = [_dot(tinv[i], jnp.concatenate([ch[i]["v"] * ch[i]["beta"], kb[i] * e_cum[i]], 1), passes) for i in idx]
    qq = [_dot(qk[i], uw[i], passes) for i in idx]
    u = [uw[i][:, :LANES] for i in idx]
    w = [uw[i][:, LANES:] for i in idx]
    o0 = [qq[i][:, :LANES] for i in idx]
    qhat = [ch[i]["q"] * e_cum[i] - qq[i][:, LANES:] for i in idx]
    kd = [ch[i]["k"] * jnp.exp(suf[i]) for i in idx]
    s = [x["s"] for x in ch]
    os_ = [[None] * nc for _ in ch]
    for step in range(nc):
        ci = [step if x["fwd"] else nc - 1 - step for x in ch]
        sl = [slice(ci[i] * c, (ci[i] + 1) * c) for i in idx]
        xx = [_dot(jnp.concatenate([w[i][sl[i]], qhat[i][sl[i]]], 0), s[i], passes) for i in idx]
        vn = [u[i][sl[i]] - xx[i][:c] for i in idx]
        for i in idx:
            os_[i][ci[i]] = o0[i][sl[i]] + xx[i][c:]
        upd = [_dot(kd[i][sl[i]], vn[i], passes, _TN) for i in idx]
        s = [s[i] * jnp.exp(tot[i][ci[i] * c:ci[i] * c + 1]) + upd[i] for i in idx]
    return [jnp.concatenate(o, 0) for o in os_], s


def _gd_kernel(pq_ref, pk_ref, pv_ref, pz_ref, pab_ref, wq_ref, wk_ref, wv_ref, alog_ref, dtb_ref, ng_ref,
               *rest, passes, hg):
    s0_ref, o_ref, s_ref, q_scr, k_scr, v_scr, gb_scr, o_scr = rest if len(rest) == 8 else (None,) + rest
    L = pq_ref.shape[0]
    h0 = pl.program_id(1) * hg
    l2n = lambda t: t * lax.rsqrt(jnp.sum(t * t, -1, keepdims=True) + 1e-6)
    pab = pab_ref[...].astype(F32)
    gb_scr[0] = -jnp.exp(alog_ref[...]) * _softplus(pab + dtb_ref[...])
    gb_scr[1] = _sigmoid(pab)
    for hh in range(hg):
        cols = slice(hh * LANES, (hh + 1) * LANES)
        q_scr[:, cols] = l2n(_silu(_conv3(pq_ref[:, cols].astype(F32), wq_ref[:, cols]))) * (GD_DK ** -0.5)
        k_scr[:, cols] = l2n(_silu(_conv3(pk_ref[:, cols].astype(F32), wk_ref[:, cols])))
        v_scr[:, cols] = _silu(_conv3(pv_ref[:, cols].astype(F32), wv_ref[:, cols]))
    nblk = L // SUPER
    lane = lax.broadcasted_iota(jnp.int32, (SUPER, LANES), 1)

    def pick(tile, which):
        return jnp.broadcast_to(jnp.sum(jnp.where(lane == which, tile, 0.0), -1, keepdims=True), tile.shape)

    def body(i, carry):
        chains, where = [], []
        for d in range(2):
            masks = _chunk_masks(SUPER, GD_CHUNK, d == 0)
            j = i if d == 0 else nblk - 1 - i
            rows = pl.ds(pl.multiple_of(j * SUPER, SUPER), SUPER)
            g_t, b_t = gb_scr[0, rows, :], gb_scr[1, rows, :]
            for hh in range(hg):
                cols = slice(hh * LANES, (hh + 1) * LANES)
                chains.append(dict(q=q_scr[rows, cols], k=k_scr[rows, cols], v=v_scr[rows, cols],
                                   lw=pick(g_t, d * GD_H + h0 + hh), beta=pick(b_t, 2 * GD_H + d * GD_H + h0 + hh),
                                   s=carry[d * hg + hh], masks=masks, fwd=d == 0))
                where.append((d, rows, cols))
        os_, ss = _gd_blocks(chains, passes)
        for (d, rows, cols), o in zip(where, os_):
            o_scr[d, rows, cols] = o
        return tuple(ss)

    zero = jnp.zeros((LANES, LANES), F32)
    s_fin = lax.fori_loop(0, nblk, body,
                          tuple(zero if s0_ref is None else s0_ref[d, hh] for d in range(2) for hh in range(hg)))
    for d in range(2):
        for hh in range(hg):
            s_ref[d, hh] = s_fin[d * hg + hh]
    for hh in range(hg):
        cols = slice(hh * LANES, (hh + 1) * LANES)
        o = o_scr[0, :, cols] + o_scr[1, :, cols]
        o = o * lax.rsqrt(jnp.mean(o * o, -1, keepdims=True) + NORM_EPS) * ng_ref[...]
        o_ref[:, cols] = (o * _silu(pz_ref[:, cols].astype(F32))).astype(o_ref.dtype)


def gdn_mix(p, row0, nseq, L, conv_w, a_log, dt_bias, norm_g, s0, passes, hg):
    b0 = row0 // L
    w = hg * LANES
    nhb = GD_H // hg
    col = lambda part: pl.BlockSpec((L, w), lambda b, h: (b0 + b, part * nhb + h))
    wcol = lambda part: pl.BlockSpec((3, w), lambda b, h: (0, part * nhb + h))
    vec = pl.BlockSpec((1, LANES), lambda b, h: (0, 0))
    st = pl.BlockSpec((None, 2, hg, LANES, LANES), lambda b, h: (b, 0, h, 0, 0))
    alog_row = jnp.zeros((1, LANES), F32).at[0, :2 * GD_H].set(a_log.reshape(-1))
    dtb_row = jnp.zeros((1, LANES), F32).at[0, :2 * GD_H].set(dt_bias.reshape(-1))
    return pl.pallas_call(
        functools.partial(_gd_kernel, passes=passes, hg=hg),
        grid=(nseq, nhb),
        in_specs=[col(0), col(1), col(2), col(3),
                  pl.BlockSpec((L, LANES), lambda b, h: (b0 + b, 4 * GD_H)),
                  wcol(0), wcol(1), wcol(2), vec, vec, vec] + ([] if s0 is None else [st]),
        out_specs=[pl.BlockSpec((L, w), lambda b, h: (b, h)), st],
        out_shape=[jax.ShapeDtypeStruct((nseq * L, GD_H * LANES), BF16),
                   jax.ShapeDtypeStruct((nseq, 2, GD_H, LANES, LANES), F32)],
        scratch_shapes=[pltpu.VMEM((L, w), F32), pltpu.VMEM((L, w), F32), pltpu.VMEM((L, w), F32),
                        pltpu.VMEM((2, L, LANES), F32), pltpu.VMEM((2, L, w), F32)],
        compiler_params=_cparams("arbitrary", "arbitrary"),
    )(p, p, p, p, p, conv_w, conv_w, conv_w, alog_row, dtb_row, norm_g.reshape(1, LANES),
      *(() if s0 is None else (s0,)))


def _router_kernel(x_ref, g_ref, sc_ref, sh_ref, wr_ref, h_ref, gate_ref, idx_ref, cnt_ref, cnt_scr):
    h = _norm_mod(x_ref[...], g_ref[...], sc_ref[...], sh_ref[...])
    h_ref[...] = h
    logits = _dot(h, wr_ref[...], 6)
    lane = lax.broadcasted_iota(jnp.int32, logits.shape, 1)
    logits = jnp.where(lane < N_EXP, logits, -jnp.inf)
    m1 = jnp.max(logits, -1, keepdims=True)
    i1 = jnp.min(jnp.where(logits == m1, lane, LANES), -1, keepdims=True)
    rest = jnp.where(lane == i1, -jnp.inf, logits)
    m2 = jnp.max(rest, -1, keepdims=True)
    i2 = jnp.min(jnp.where(rest == m2, lane, LANES), -1, keepdims=True)
    e2 = jnp.exp(m2 - m1)
    g1 = 1.0 / (1.0 + e2)
    g2 = e2 / (1.0 + e2)
    gate_ref[...] = jnp.where(lane == 0, g1, 0.0) + jnp.where(lane == 1, g2, 0.0)

    @pl.when(pl.program_id(0) == 0)
    def _():
        cnt_scr[...] = jnp.zeros_like(cnt_scr)

    tm = logits.shape[0]
    hot = jnp.where((lane == i1) | (lane == i2), 1.0, 0.0)
    row = lax.broadcasted_iota(jnp.int32, (tm, tm), 0)
    col = lax.broadcasted_iota(jnp.int32, (tm, tm), 1)
    before = _dg(_mask_bf16(col < row), hot.astype(BF16)) + cnt_scr[0:1]
    r1 = jnp.sum(jnp.where(lane == i1, before, 0.0), -1, keepdims=True).astype(jnp.int32)
    r2 = jnp.sum(jnp.where(lane == i2, before, 0.0), -1, keepdims=True).astype(jnp.int32)
    idx_ref[...] = (jnp.where(lane == 0, i1, 0) + jnp.where(lane == 1, i2, 0)
                    + jnp.where(lane == 2, r1, 0) + jnp.where(lane == 3, r2, 0))
    cnt_scr[...] = cnt_scr[...] + jnp.sum(hot, 0, keepdims=True)
    cnt_ref[...] = cnt_scr[...]


def moe_router(x, g, sc_t, sh_t, router):
    tm = 512
    per = TM // tm
    wr = jnp.zeros((D, LANES), F32).at[:, :N_EXP].set(router)
    return pl.pallas_call(
        _router_kernel,
        grid=(T_ALL // tm,),
        in_specs=[pl.BlockSpec((tm, D), lambda i: (i, 0)),
                  pl.BlockSpec((1, D), lambda i: (0, 0)),
                  pl.BlockSpec((None, 1, D), lambda i: (i // per, 0, 0)),
                  pl.BlockSpec((None, 1, D), lambda i: (i // per, 0, 0)),
                  pl.BlockSpec((D, LANES), lambda i: (0, 0))],
        out_specs=[pl.BlockSpec((tm, D), lambda i: (i, 0)), pl.BlockSpec((tm, LANES), lambda i: (i, 0)),
                   pl.BlockSpec((tm, LANES), lambda i: (i, 0)), pl.BlockSpec((8, LANES), lambda i: (0, 0))],
        out_shape=[jax.ShapeDtypeStruct((T_ALL, D), F32), jax.ShapeDtypeStruct((T_ALL, LANES), F32),
                   jax.ShapeDtypeStruct((T_ALL, LANES), jnp.int32), jax.ShapeDtypeStruct((8, LANES), F32)],
        scratch_shapes=[pltpu.VMEM((8, LANES), F32)],
        compiler_params=_cparams("arbitrary"),
    )(x, g.reshape(1, D), sc_t, sh_t, wr)


def moe_slot_positions(idx, cnt):
    e_flat = idx[:, 0:2].reshape(-1)
    rank = idx[:, 2:4].reshape(-1)
    onehot = (e_flat[:, None] == jnp.arange(N_EXP, dtype=jnp.int32)[None, :]).astype(jnp.int32)
    counts = cnt[0, :N_EXP].astype(jnp.int32)
    gsize = ((counts + MOE_TILE - 1) // MOE_TILE) * MOE_TILE
    gend = jnp.cumsum(gsize)
    pos = jnp.sum(onehot * (gend - gsize)[None, :], axis=1) + rank
    tile_start = jnp.arange(MOE_TILES, dtype=jnp.int32) * MOE_TILE
    tile_expert = jnp.minimum(jnp.sum((gend[None, :] <= tile_start[:, None]).astype(jnp.int32), axis=1), N_EXP - 1)
    return pos.astype(jnp.int32), tile_expert.astype(jnp.int32), (gend[-1:] // MOE_TILE).astype(jnp.int32)


def _row_copy(src, s, dst, d, sem):
    return pltpu.make_async_copy(src.at[pl.ds(s, 1)], dst.at[pl.ds(d, 1)], sem)


def _dispatch_kernel(pos_ref, h_ref, xs_in_ref, xs_ref, sem):
    del xs_in_ref
    tm = h_ref.shape[0]
    base = pl.program_id(0) * tm

    def issue(r, carry):
        for k in range(2):
            _row_copy(h_ref, r, xs_ref, pos_ref[(base + r) * 2 + k], sem).start()
        return carry

    lax.fori_loop(0, tm, issue, 0, unroll=8)
    for k in range(2):
        pltpu.make_async_copy(h_ref, xs_ref.at[pl.ds(0, tm)], sem).wait()


def moe_dispatch(h, pos):
    tm = 512
    return pl.pallas_call(
        _dispatch_kernel,
        grid_spec=pltpu.PrefetchScalarGridSpec(
            num_scalar_prefetch=1,
            grid=(T_ALL // tm,),
            in_specs=[pl.BlockSpec((tm, D), lambda i, pos: (i, 0)), pl.BlockSpec(memory_space=pl.ANY)],
            out_specs=pl.BlockSpec(memory_space=pl.ANY),
            scratch_shapes=[pltpu.SemaphoreType.DMA]),
        out_shape=jax.ShapeDtypeStruct((MOE_ROWS, D), F32),
        input_output_aliases={2: 0},
        compiler_params=_cparams("arbitrary"),
    )(pos, h, jnp.zeros((MOE_ROWS, D), F32))


def _expert_kernel(te_ref, nu_ref, xs_ref, wg_ref, wu_ref, wo_ref, o_ref, x_scr, acc):
    i = pl.program_id(0)
    f = pl.program_id(1)
    used = i < nu_ref[0]

    @pl.when(used & (f == 0))
    def _():
        x_scr[...] = xs_ref[...].astype(BF16)
        acc[...] = jnp.zeros_like(acc)

    @pl.when(used)
    def _():
        x = x_scr[...]
        act = _silu(_dg(x, wg_ref[...])) * _dg(x, wu_ref[...])
        acc[...] += _dg(act.astype(BF16), wo_ref[...])

    last = f == pl.num_programs(1) - 1

    @pl.when(used & last)
    def _():
        o_ref[...] = acc[...]

    @pl.when(jnp.logical_not(used) & last)
    def _():
        o_ref[...] = jnp.zeros_like(o_ref)


def moe_experts(xs, tile_expert, n_used, w_in_bf16, w_out_bf16):
    tn = 512
    nf = E_FF // tn
    live = lambda i, f, nu: jnp.where(i < nu[0], f, 0)
    return pl.pallas_call(
        _expert_kernel,
        grid_spec=pltpu.PrefetchScalarGridSpec(
            num_scalar_prefetch=2,
            grid=(MOE_TILES, nf),
            in_specs=[pl.BlockSpec((MOE_TILE, D), lambda i, f, te, nu: (i, 0)),
                      pl.BlockSpec((None, D, tn), lambda i, f, te, nu: (te[i], 0, live(i, f, nu))),
                      pl.BlockSpec((None, D, tn), lambda i, f, te, nu: (te[i], 0, live(i, f, nu) + nf)),
                      pl.BlockSpec((None, tn, D), lambda i, f, te, nu: (te[i], live(i, f, nu), 0))],
            out_specs=pl.BlockSpec((MOE_TILE, D), lambda i, f, te, nu: (i, 0)),
            scratch_shapes=[pltpu.VMEM((MOE_TILE, D), BF16), pltpu.VMEM((MOE_TILE, D), F32)]),
        out_shape=jax.ShapeDtypeStruct((MOE_ROWS, D), F32),
        compiler_params=_cparams("arbitrary", "arbitrary"),
    )(tile_expert, n_used, xs, w_in_bf16, w_in_bf16, w_out_bf16)


def _combine_kernel(pos_ref, ys_ref, gates_ref, x_ref, gate_ref, fg_ref, o_ref, y_scr, sem):
    tm = x_ref.shape[0]
    base = pl.program_id(0) * tm

    def issue(r, carry):
        for k in range(2):
            _row_copy(ys_ref, pos_ref[(base + r) * 2 + k], y_scr.at[k], r, sem).start()
        return carry

    lax.fori_loop(0, tm, issue, 0, unroll=8)
    for k in range(2):
        pltpu.make_async_copy(ys_ref.at[pl.ds(0, tm)], y_scr.at[k], sem).wait()
    gates = gates_ref[...]
    lane = lax.broadcasted_iota(jnp.int32, gates.shape, 1)
    g0 = jnp.sum(jnp.where(lane == 0, gates, 0.0), -1, keepdims=True)
    g1 = jnp.sum(jnp.where(lane == 1, gates, 0.0), -1, keepdims=True)
    xn = x_ref[...] + gate_ref[...] * (y_scr[0] * g0 + y_scr[1] * g1)
    y = xn * lax.rsqrt(jnp.mean(xn * xn, -1, keepdims=True) + NORM_EPS)
    o_ref[...] = y * fg_ref[...]


def moe_combine_final(ys, pos, gates, x, gate_t, final_g):
    tm = 256
    per = TM // tm
    return pl.pallas_call(
        _combine_kernel,
        grid_spec=pltpu.PrefetchScalarGridSpec(
            num_scalar_prefetch=1,
            grid=(T_ALL // tm,),
            in_specs=[pl.BlockSpec(memory_space=pl.ANY),
                      pl.BlockSpec((tm, LANES), lambda i, pos: (i, 0)),
                      pl.BlockSpec((tm, D), lambda i, pos: (i, 0)),
                      pl.BlockSpec((None, 1, D), lambda i, pos: (i // per, 0, 0)),
                      pl.BlockSpec((1, D), lambda i, pos: (0, 0))],
            out_specs=pl.BlockSpec((tm, D), lambda i, pos: (i, 0)),
            scratch_shapes=[pltpu.VMEM((2, tm, D), F32), pltpu.SemaphoreType.DMA]),
        out_shape=jax.ShapeDtypeStruct((T_ALL, D), F32),
        compiler_params=_cparams("arbitrary"),
    )(pos, ys, gates, x, gate_t, final_g.reshape(1, D))


def _tile_rows(mod_l, k):
    cols = mod_l[:, k * D:(k + 1) * D]
    ctx = jnp.broadcast_to(cols[0:1], (T_CTX // TM, D))
    den = jnp.repeat(cols[1:1 + B_DEN], L_DEN // TM, axis=0)
    return jnp.concatenate([ctx, den], 0)[:, None, :]


def _pad_cols(w, n):
    return jnp.pad(w, ((0, 0), (0, n - w.shape[1])))


def kernel(x_prompt, x_sample, state_rwkv, state_gdn, c, c_ctx, ada_w, ada_b, norm_mix_g, norm_ffn_g, final_norm_g, e_w_in, e_hy_conv_w, e_hy_conv_b, e_hf_w1, e_hf_b1, e_hf_freq1, e_hf_w2, e_hf_b2, e_hf_freq2, e_hf_w3, e_hy_bias, e_rw_mu, e_rw_w0, e_rw_w2, e_rw_a0, e_rw_a2, e_rw_g2, e_rw_kk, e_rw_ka, e_rw_rk, e_rw_ln_w, e_rw_ln_b, e_w_out, e_ffn_w_in, e_ffn_w_out, o_w_in, o_conv_w, o_A_log, o_dt_bias, o_norm_g, o_w_out, o_router, o_moe_w_in, o_moe_w_out):
    passes = 1
    cond16 = jnp.zeros((16, D), F32).at[0].set(c_ctx).at[1:1 + B_DEN].set(c)
    mod = modulation(cond16, ada_w, ada_b)
    x = assemble_tokens(x_prompt, x_sample)

    m0 = [_tile_rows(mod[0], k) for k in range(6)]
    p = norm_matmul(x, norm_mix_g[0], m0[1], m0[0], _pad_cols(e_w_in[0], P_EVEN_PAD).astype(BF16),
                    P_EVEN_PAD // 2, BF16)
    ys_hy, ys_rw, st_rw = [], [], None
    for row0, nseq, L in ((0, B_CTX, L_CTX), (T_CTX, B_DEN, L_DEN)):
        spec = hyena_filter_spectrum(L, e_hf_w1[0], e_hf_b1[0], e_hf_freq1[0], e_hf_w2[0], e_hf_b2[0],
                                     e_hf_freq2[0], e_hf_w3[0])
        ys_hy.append(hyena_mix(p, row0, nseq, L, e_hy_conv_w[0], e_hy_conv_b[0].reshape(1, -1), spec,
                               e_hy_bias[0], 1))
        r, v, kk, lw, kd, bd, bonus, g = rwkv_prep(p, row0, nseq, L, e_rw_mu[0], e_rw_w0[0], e_rw_w2[0],
                                                   e_rw_a0[0], e_rw_a2[0], e_rw_g2[0], e_rw_kk[0], e_rw_ka[0],
                                                   e_rw_rk[0])
        s0 = None if row0 == 0 else _rw_states_to_blockdiag(state_rwkv[:, 0])
        y2, s_new = rwkv_scan(r, v, kk, lw, kd, bd, s0, nseq, L, passes, 4)
        if row0 == 0:
            st_rw = _rw_states_from_blockdiag(s_new)
        ys_rw.append(rwkv_post(y2, bonus, g, e_rw_ln_w[0], e_rw_ln_b[0]))
    y_mix = jnp.concatenate([jnp.concatenate(ys_hy, 0), jnp.concatenate(ys_rw, 0)], 1)
    x = matmul_residual(y_mix, e_w_out[0].astype(BF16), x, m0[2], D)
    act = norm_swiglu(x, norm_ffn_g[0], m0[4], m0[3], e_ffn_w_in[0].astype(BF16), D_FF, D_FF // 2)
    x = matmul_residual(act, e_ffn_w_out[0].astype(BF16), x, m0[5], D // 2)

    m1 = [_tile_rows(mod[1], k) for k in range(6)]
    p = norm_matmul(x, norm_mix_g[1], m1[1], m1[0], _pad_cols(o_w_in[0], P_ODD_PAD).astype(BF16),
                    P_ODD_PAD // 3, BF16)
    os_, st_gd = [], None
    for row0, nseq, L in ((0, B_CTX, L_CTX), (T_CTX, B_DEN, L_DEN)):
        s0 = None if row0 == 0 else state_gdn[:, 0]
        o, s_new = gdn_mix(p, row0, nseq, L, o_conv_w[0], o_A_log[0], o_dt_bias[0], o_norm_g[0], s0, passes, 8)
        if row0 == 0:
            st_gd = s_new
        os_.append(o)
    x = matmul_residual(jnp.concatenate(os_, 0), o_w_out[0].astype(BF16), x, m1[2], D)
    h, gates, idx, cnt = moe_router(x, norm_ffn_g[1], m1[4], m1[3], o_router[0])
    pos, tile_expert, n_used = moe_slot_positions(idx, cnt)
    ys = moe_experts(moe_dispatch(h, pos), tile_expert, n_used, o_moe_w_in[0].astype(BF16),
                     o_moe_w_out[0].astype(BF16))
    y = moe_combine_final(ys, pos, gates, x, m1[5], final_norm_g)

    y_prompt = y[:T_CTX].reshape(B_CTX, L_CTX, D)
    y_sample = y[T_CTX:].reshape(B_DEN, L_DEN, D)
    return (y_prompt, y_sample, st_rw[:, None], st_gd[:, None])
```

```python
import functools
import math

import numpy as np
import jax
import jax.numpy as jnp
from jax import lax
from jax.experimental import pallas as pl
from jax.experimental.pallas import tpu as pltpu

F32 = jnp.float32
BF16 = jnp.bfloat16

D = 1024
B_CTX, L_CTX = 32, 256
B_DEN, L_DEN = 8, 1024
T_CTX = B_CTX * L_CTX
T_DEN = B_DEN * L_DEN
T_ALL = T_CTX + T_DEN
GRID_W = 64
NORM_EPS = 1e-6

HY_W = 512
HY_EMB = 33
HY_BANDS = 16
HY_FFN = 64
HY_TARGET, HY_FAST, HY_SLOW = 1e-2, 0.3, 1.5

RW_W = 512
RW_N = 64
RW_H = 8
RW_LORA = 224
RW_LN_EPS = 64e-5
P_EVEN = 3 * HY_W + 3 * RW_W + RW_LORA
P_EVEN_PAD = 3328

GD_H = 8
GD_DK = 128
GD_QKV = 3072
P_ODD = 4128
P_ODD_PAD = 4224

D_FF = 2816
N_EXP = 8
E_FF = 3584
MOE_TILE = 1024
MOE_ROWS = 2 * T_ALL + N_EXP * MOE_TILE
MOE_TILES = MOE_ROWS // MOE_TILE

LANES = 128
TM = 1024
RW_CHUNK = 32
GD_CHUNK = 64
SUPER = 128
VMEM_LIMIT = 56 * 1024 * 1024

_NN = (((1,), (0,)), ((), ()))
_NT = (((1,), (1,)), ((), ()))
_TN = (((0,), (0,)), ((), ()))


def _cparams(*sem):
    return pltpu.CompilerParams(dimension_semantics=sem, vmem_limit_bytes=VMEM_LIMIT)


def _dg(a, b, dims=_NN):
    return lax.dot_general(a, b, dims, preferred_element_type=F32)


def _split2(x):
    hi = x.astype(BF16)
    lo = (x - hi.astype(F32)).astype(BF16)
    return hi, lo


def _split3(x):
    x0 = x.astype(BF16)
    r1 = x - x0.astype(F32)
    x1 = r1.astype(BF16)
    x2 = (r1 - x1.astype(F32)).astype(BF16)
    return x0, x1, x2


def _dot(a, b, passes=1, dims=_NN):
    if passes == 1:
        return _dg(a.astype(BF16), b.astype(BF16), dims)
    if passes == 3:
        ah, al = _split2(a)
        bh, bl = _split2(b)
        return _dg(ah, bh, dims) + (_dg(ah, bl, dims) + _dg(al, bh, dims))
    a0, a1, a2 = _split3(a)
    b0, b1, b2 = _split3(b)
    small = _dg(a0, b2, dims) + _dg(a1, b1, dims) + _dg(a2, b0, dims)
    mid = _dg(a0, b1, dims) + _dg(a1, b0, dims)
    return _dg(a0, b0, dims) + (mid + small)


def _dot_exact_l(m, x, dims=_NN):
    x0, x1, x2 = _split3(x)
    return _dg(m, x0, dims) + (_dg(m, x1, dims) + _dg(m, x2, dims))


def _dot_exact_r(x, m, dims=_NN):
    x0, x1, x2 = _split3(x)
    return _dg(x0, m, dims) + (_dg(x1, m, dims) + _dg(x2, m, dims))


def _sigmoid(x):
    return 1.0 / (1.0 + jnp.exp(-x))


def _silu(x):
    return (0.5 * x) * (1.0 + jnp.tanh(0.5 * x))


def _softplus(x):
    return jnp.maximum(x, 0.0) + jnp.log(1.0 + jnp.exp(-jnp.abs(x)))


def _shift_rows(x):
    n = x.shape[0]
    row = lax.broadcasted_iota(jnp.int32, x.shape, 0)
    prev = jnp.where(row == 0, 0.0, pltpu.roll(x, 1, 0))
    nxt = jnp.where(row == n - 1, 0.0, pltpu.roll(x, n - 1, 0))
    return prev, nxt


def _conv3(x, w, b=None):
    prev, nxt = _shift_rows(x)
    y = prev * w[0:1] + x * w[1:2] + nxt * w[2:3]
    return y if b is None else y + b


def _chunk_masks(n, chunk, fwd):
    row = lax.broadcasted_iota(jnp.int32, (n, n), 0)
    col = lax.broadcasted_iota(jnp.int32, (n, n), 1)
    sh = int(math.log2(chunk))
    same = (row >> sh) == (col >> sh)
    before = (col < row) if fwd else (col > row)
    pair = (row >> 1) == (col >> 1)
    joins = [((row >> (lvl + 1)) == (col >> (lvl + 1))) & ((row >> lvl) != (col >> lvl))
             for lvl in range(1, sh)]
    incl = same & (before | (row == col))
    return dict(strict=same & before, incl=incl, eye=jnp.where(row == col, 1.0, 0.0), pair=pair, joins=joins,
                incl_bf=_mask_bf16(incl))


def _mask_bf16(m):
    return jnp.where(m, 1.0, 0.0).astype(BF16)


def _chunk_totals(cum, chunk, fwd):
    n = cum.shape[0]
    rows = [cum[(ci + 1) * chunk - 1:(ci + 1) * chunk] if fwd else cum[ci * chunk:ci * chunk + 1]
            for ci in range(n // chunk)]
    return jnp.concatenate([jnp.broadcast_to(r, (chunk, cum.shape[1])) for r in rows], 0)


def _tri_inv(xs, masks, passes):
    ts = [m["eye"] + jnp.where(m["pair"], x, 0.0) for x, m in zip(xs, masks)]
    for lvl in range(len(masks[0]["joins"])):
        ps = [_dot(jnp.where(m["joins"][lvl], x, 0.0), t, passes) for x, m, t in zip(xs, masks, ts)]
        ts = [t + _dot(t, p, passes) for t, p in zip(ts, ps)]
    return ts


@functools.lru_cache(maxsize=None)
def _pos_table():
    t = np.arange(L_DEN)
    row = (t // GRID_W).astype(np.float32)
    col = (t % GRID_W).astype(np.float32)
    q = D // 4
    omega = np.exp(-math.log(10000.0) * np.arange(q, dtype=np.float32) / q).astype(np.float32)
    enc = lambda pos: np.concatenate([np.sin(pos[:, None] * omega), np.cos(pos[:, None] * omega)], -1)
    return np.concatenate([enc(row), enc(col)], -1).astype(np.float32)


@functools.lru_cache(maxsize=None)
def _dft_tables(L):
    f = np.arange(L, dtype=np.int64)
    m = (f[:, None] * f[None, :]) % (2 * L)
    ang = np.pi * m.astype(np.float64) / L
    return np.cos(ang).astype(np.float32), np.sin(ang).astype(np.float32)


@functools.lru_cache(maxsize=None)
def _hyena_static(L):
    k = np.arange(L, dtype=np.float32)
    t = k / np.float32(L - 1)
    bands = np.linspace(1e-4, HY_BANDS - 1, HY_BANDS, dtype=np.float32)
    ang = (np.float32(2.0 * math.pi) * k / np.float32(L))[:, None] * bands[None, :]
    feats = np.concatenate([t[:, None], np.cos(ang), -np.sin(ang)], -1).astype(np.float32)
    feats_p = np.zeros((L, LANES), np.float32)
    feats_p[:, :HY_EMB] = feats
    deltas = np.abs(np.linspace(math.log(HY_TARGET) / HY_FAST, math.log(HY_TARGET) / HY_SLOW, HY_W,
                                dtype=np.float32))
    window = np.exp(-t[:, None] * deltas[None, :]).astype(np.float32)
    return feats_p, window


def _block_ones(n, blk):
    i = np.arange(n) // blk
    return (i[:, None] == i[None, :]).astype(np.float32)


def _mod_kernel(c_ref, w_ref, b_ref, o_ref):
    o_ref[...] = _dot(_silu(c_ref[...]), w_ref[...], 6) + b_ref[...]


def modulation(cond16, ada_w, ada_b):
    depth = ada_w.shape[0]
    tn = 1024
    return pl.pallas_call(
        _mod_kernel,
        grid=(depth, 6 * D // tn),
        in_specs=[pl.BlockSpec((16, D), lambda i, j: (0, 0)),
                  pl.BlockSpec((None, D, tn), lambda i, j: (i, 0, j)),
                  pl.BlockSpec((None, 1, tn), lambda i, j: (i, 0, j))],
        out_specs=pl.BlockSpec((None, 16, tn), lambda i, j: (i, 0, j)),
        out_shape=jax.ShapeDtypeStruct((depth, 16, 6 * D), F32),
        compiler_params=_cparams("arbitrary", "arbitrary"),
    )(cond16, ada_w, ada_b.reshape(depth, 1, 6 * D))


def _assemble_kernel(xp_ref, xs_ref, pos_ref, o_ref):
    i = pl.program_id(0)

    @pl.when(i < T_CTX // 256)
    def _():
        o_ref[...] = xp_ref[...]

    @pl.when(i >= T_CTX // 256)
    def _():
        o_ref[...] = xs_ref[...] + pos_ref[...]


def assemble_tokens(x_prompt, x_sample):
    nc = T_CTX // 256
    pos = jnp.asarray(_pos_table())
    return pl.pallas_call(
        _assemble_kernel,
        grid=(T_ALL // 256,),
        in_specs=[pl.BlockSpec((256, D), lambda i: (jnp.minimum(i, nc - 1), 0)),
                  pl.BlockSpec((256, D), lambda i: (jnp.maximum(i - nc, 0), 0)),
                  pl.BlockSpec((256, D), lambda i: (jnp.maximum(i - nc, 0) % (L_DEN // 256), 0))],
        out_specs=pl.BlockSpec((256, D), lambda i: (i, 0)),
        out_shape=jax.ShapeDtypeStruct((T_ALL, D), F32),
        compiler_params=_cparams("arbitrary"),
    )(x_prompt.reshape(T_CTX, D), x_sample.reshape(T_DEN, D), pos)


def _norm_mod(x, g, sc, sh):
    y = x * lax.rsqrt(jnp.mean(x * x, -1, keepdims=True) + NORM_EPS)
    return (y * g) * (1.0 + sc) + sh


def _norm_mm_kernel(x_ref, g_ref, sc_ref, sh_ref, w_ref, o_ref, h_scr):
    @pl.when(pl.program_id(1) == 0)
    def _():
        h_scr[...] = _norm_mod(x_ref[...], g_ref[...], sc_ref[...], sh_ref[...]).astype(BF16)

    o_ref[...] = _dg(h_scr[...], w_ref[...]).astype(o_ref.dtype)


def norm_matmul(x, g, sc_t, sh_t, w_bf16, tn, out_dtype=F32):
    n = w_bf16.shape[1]
    return pl.pallas_call(
        _norm_mm_kernel,
        grid=(T_ALL // TM, n // tn),
        in_specs=[pl.BlockSpec((TM, D), lambda i, j: (i, 0)),
                  pl.BlockSpec((1, D), lambda i, j: (0, 0)),
                  pl.BlockSpec((None, 1, D), lambda i, j: (i, 0, 0)),
                  pl.BlockSpec((None, 1, D), lambda i, j: (i, 0, 0)),
                  pl.BlockSpec((D, tn), lambda i, j: (0, j))],
        out_specs=pl.BlockSpec((TM, tn), lambda i, j: (i, j)),
        out_shape=jax.ShapeDtypeStruct((T_ALL, n), out_dtype),
        scratch_shapes=[pltpu.VMEM((TM, D), BF16)],
        compiler_params=_cparams("arbitrary", "arbitrary"),
    )(x, g.reshape(1, D), sc_t, sh_t, w_bf16)


def _norm_swiglu_kernel(x_ref, g_ref, sc_ref, sh_ref, wg_ref, wu_ref, o_ref, h_scr):
    @pl.when(pl.program_id(1) == 0)
    def _():
        h_scr[...] = _norm_mod(x_ref[...], g_ref[...], sc_ref[...], sh_ref[...]).astype(BF16)

    h = h_scr[...]
    o_ref[...] = (_silu(_dg(h, wg_ref[...])) * _dg(h, wu_ref[...])).astype(o_ref.dtype)


def norm_swiglu(x, g, sc_t, sh_t, w_in_bf16, dff, tn):
    nj = dff // tn
    return pl.pallas_call(
        _norm_swiglu_kernel,
        grid=(T_ALL // TM, nj),
        in_specs=[pl.BlockSpec((TM, D), lambda i, j: (i, 0)),
                  pl.BlockSpec((1, D), lambda i, j: (0, 0)),
                  pl.BlockSpec((None, 1, D), lambda i, j: (i, 0, 0)),
                  pl.BlockSpec((None, 1, D), lambda i, j: (i, 0, 0)),
                  pl.BlockSpec((D, tn), lambda i, j: (0, j)),
                  pl.BlockSpec((D, tn), lambda i, j: (0, j + nj))],
        out_specs=pl.BlockSpec((TM, tn), lambda i, j: (i, j)),
        out_shape=jax.ShapeDtypeStruct((T_ALL, dff), BF16),
        scratch_shapes=[pltpu.VMEM((TM, D), BF16)],
        compiler_params=_cparams("arbitrary", "arbitrary"),
    )(x, g.reshape(1, D), sc_t, sh_t, w_in_bf16, w_in_bf16)


def _mm_res_kernel(y_ref, w_ref, x_ref, gate_ref, o_ref):
    o_ref[...] = x_ref[...] + gate_ref[...] * _dg(y_ref[...], w_ref[...])


def matmul_residual(y_bf16, w_bf16, x, gate_t, tn):
    k = y_bf16.shape[1]
    return pl.pallas_call(
        _mm_res_kernel,
        grid=(T_ALL // TM, D // tn),
        in_specs=[pl.BlockSpec((TM, k), lambda i, j: (i, 0)),
                  pl.BlockSpec((k, tn), lambda i, j: (0, j)),
                  pl.BlockSpec((TM, tn), lambda i, j: (i, j)),
                  pl.BlockSpec((None, 1, tn), lambda i, j: (i, 0, j))],
        out_specs=pl.BlockSpec((TM, tn), lambda i, j: (i, j)),
        out_shape=jax.ShapeDtypeStruct((T_ALL, D), F32),
        compiler_params=_cparams("arbitrary", "arbitrary"),
    )(y_bf16, w_bf16, x, gate_t)


def _mm_res_split_kernel(*refs, widths):
    n = len(widths)
    ctx_refs, den_refs = refs[0:2 * n:2], refs[1:2 * n:2]
    w_ref, x_ref, gate_ref, o_ref = refs[2 * n:]
    i = pl.program_id(0)

    def run(y_refs):
        acc, k0 = None, 0
        for y_ref, kw in zip(y_refs, widths):
            part = _dg(y_ref[...], w_ref[k0:k0 + kw, :])
            acc = part if acc is None else acc + part
            k0 += kw
        o_ref[...] = x_ref[...] + gate_ref[...] * acc

    @pl.when(i < T_CTX // TM)
    def _():
        run(ctx_refs)

    @pl.when(i >= T_CTX // TM)
    def _():
        run(den_refs)


def matmul_residual_split(parts, w_bf16, x, gate_t):
    widths = tuple(pc.shape[1] for pc, _ in parts)
    nc = T_CTX // TM
    specs, args = [], []
    for (pc, pd), kw in zip(parts, widths):
        specs += [pl.BlockSpec((TM, kw), lambda i: (jnp.minimum(i, nc - 1), 0)),
                  pl.BlockSpec((TM, kw), lambda i: (jnp.maximum(i - nc, 0), 0))]
        args += [pc, pd]
    return pl.pallas_call(
        functools.partial(_mm_res_split_kernel, widths=widths),
        grid=(T_ALL // TM,),
        in_specs=specs + [pl.BlockSpec((sum(widths), D), lambda i: (0, 0)),
                          pl.BlockSpec((TM, D), lambda i: (i, 0)),
                          pl.BlockSpec((None, 1, D), lambda i: (i, 0, 0))],
        out_specs=pl.BlockSpec((TM, D), lambda i: (i, 0)),
        out_shape=jax.ShapeDtypeStruct((T_ALL, D), F32),
        compiler_params=_cparams("arbitrary"),
    )(*args, w_bf16, x, gate_t)


def _hy_filter_kernel(feat_ref, w1_ref, b1_ref, f1_ref, w2_ref, b2_ref, f2_ref, w3f_ref, w3b_ref, win_ref,
                      c_ref, s_ref, hr_ref, hi_ref, hn_ref):
    L = feat_ref.shape[0]
    h = jnp.sin(f1_ref[...] * (_dot(feat_ref[...], w1_ref[...], 6) + b1_ref[...]))
    h = jnp.sin(f2_ref[...] * (_dot(h, w2_ref[...], 6) + b2_ref[...]))
    win = win_ref[...]
    fw = _dot(h, w3f_ref[...], 6) * win
    bw = _dot(h, w3b_ref[...], 6) * win
    row = lax.broadcasted_iota(jnp.int32, fw.shape, 0)
    bw = jnp.where(row == 0, 0.0, bw)
    nrm = jnp.sum(jnp.abs(fw), 0, keepdims=True) + jnp.sum(jnp.abs(bw), 0, keepdims=True)
    ev = (fw + bw) / nrm
    od = (bw - fw) / nrm
    alt = (1 - 2 * (row & 1)).astype(F32)
    hr_ref[...] = _dot(c_ref[...], ev, 3)
    hi_ref[...] = _dot(s_ref[...], od, 3)
    hn_ref[...] = jnp.broadcast_to(jnp.sum(ev * alt, 0, keepdims=True), (8, ev.shape[1]))
    del L


def hyena_filter_spectrum(L, w1, b1, f1, w2, b2, f2, w3):
    feats, window = _hyena_static(L)
    cos_t, sin_t = _dft_tables(L)
    tc = 128
    ncb = HY_W // tc
    w1p = jnp.zeros((LANES, HY_FFN), F32).at[:HY_EMB].set(w1)
    const = lambda shape: pl.BlockSpec(shape, lambda o, c: (0,) * len(shape))
    return pl.pallas_call(
        _hy_filter_kernel,
        grid=(2, ncb),
        in_specs=[const((L, LANES)), const((LANES, HY_FFN)), const((1, HY_FFN)), const((1, HY_FFN)),
                  const((HY_FFN, HY_FFN)), const((1, HY_FFN)), const((1, HY_FFN)),
                  pl.BlockSpec((HY_FFN, tc), lambda o, c: (0, o * ncb + c)),
                  pl.BlockSpec((HY_FFN, tc), lambda o, c: (0, 2 * ncb + o * ncb + c)),
                  pl.BlockSpec((L, tc), lambda o, c: (0, c)),
                  const((L, L)), const((L, L))],
        out_specs=[pl.BlockSpec((None, L, tc), lambda o, c: (o, 0, c)),
                   pl.BlockSpec((None, L, tc), lambda o, c: (o, 0, c)),
                   pl.BlockSpec((None, 8, tc), lambda o, c: (o, 0, c))],
        out_shape=[jax.ShapeDtypeStruct((2, L, HY_W), F32), jax.ShapeDtypeStruct((2, L, HY_W), F32),
                   jax.ShapeDtypeStruct((2, 8, HY_W), F32)],
        compiler_params=_cparams("arbitrary", "arbitrary"),
    )(jnp.asarray(feats), w1p, b1.reshape(1, -1), f1.reshape(1, -1), w2, b2.reshape(1, -1), f2.reshape(1, -1),
      w3, w3, jnp.asarray(window), jnp.asarray(cos_t), jnp.asarray(sin_t))


def _hy_mix_kernel(pv_ref, p1_ref, p2_ref, wv_ref, w1_ref, w2_ref, bv_ref, b1_ref, b2_ref,
                   hr_ref, hi_ref, hn_ref, bias_ref, ch_ref, cl_ref, sh_ref, sl_ref, o_ref, *, passes):
    L = pv_ref.shape[0]
    z = _conv3(pv_ref[...].astype(F32), wv_ref[...], bv_ref[...])
    gates = (_conv3(p1_ref[...].astype(F32), w1_ref[...], b1_ref[...]),
             _conv3(p2_ref[...].astype(F32), w2_ref[...], b2_ref[...]))
    row = lax.broadcasted_iota(jnp.int32, z.shape, 0)
    alt = (1 - 2 * (row & 1)).astype(F32)
    ch, cl, sh, sl = ch_ref[...], cl_ref[...], sh_ref[...], sl_ref[...]

    def tdot(th, tl, x):
        if passes == 1:
            return _dg(th, x.astype(BF16))
        xh, xl = _split2(x)
        return _dg(th, xh) + (_dg(th, xl) + _dg(tl, xh))

    inv_l = 1.0 / L
    for o in range(2):
        hr, hi, hn = hr_ref[o], hi_ref[o], hn_ref[o][0:1]
        zc = tdot(ch, cl, z)
        zs = tdot(sh, sl, z)
        zn = jnp.sum(z * alt, 0, keepdims=True)
        yr = zc * hr + zs * hi
        yi = zc * hi - zs * hr
        wr = jnp.where(row == 0, 0.5 * inv_l, inv_l)
        conv = tdot(ch, cl, yr * wr) - tdot(sh, sl, yi * inv_l) + alt * (zn * hn * (0.5 * inv_l))
        z = gates[o] * (conv + z * bias_ref[o:o + 1])
    o_ref[...] = z.astype(o_ref.dtype)


def hyena_mix(p, row0, nseq, L, conv_w, conv_b, spec, bias, passes):
    hr, hi, hn = spec
    cos_t, sin_t = _dft_tables(L)
    ch, cl = _np_split2(cos_t)
    sh, sl = _np_split2(sin_t)
    tc = 512 if L <= 256 else 256
    ncb = HY_W // tc
    b0 = row0 // L
    pspec = lambda part: pl.BlockSpec((L, tc), lambda b, c: (b0 + b, part * ncb + c))
    wspec = lambda part: pl.BlockSpec((3, tc), lambda b, c: (0, part * ncb + c))
    bspec = lambda part: pl.BlockSpec((1, tc), lambda b, c: (0, part * ncb + c))
    hspec = lambda rows: pl.BlockSpec((2, rows, tc), lambda b, c: (0, 0, c))
    tab = pl.BlockSpec((L, L), lambda b, c: (0, 0))
    return pl.pallas_call(
        functools.partial(_hy_mix_kernel, passes=passes),
        grid=(nseq, ncb),
        in_specs=[pspec(0), pspec(1), pspec(2), wspec(0), wspec(1), wspec(2), bspec(0), bspec(1), bspec(2),
                  hspec(L), hspec(L), hspec(8), pl.BlockSpec((2, tc), lambda b, c: (0, c)), tab, tab, tab, tab],
        out_specs=pl.BlockSpec((L, tc), lambda b, c: (b, c)),
        out_shape=jax.ShapeDtypeStruct((nseq * L, HY_W), BF16),
        compiler_params=_cparams("arbitrary", "arbitrary"),
    )(p, p, p, conv_w, conv_w, conv_w, conv_b, conv_b, conv_b, hr, hi, hn, bias,
      jnp.asarray(ch), jnp.asarray(cl), jnp.asarray(sh), jnp.asarray(sl))


def _np_split2(x):
    hi = x.astype(jnp.bfloat16)
    lo = (x - hi.astype(np.float32)).astype(jnp.bfloat16)
    return hi, lo


def _rw_prep_kernel(pr_ref, pk_ref, pv_ref, pl_ref, mur_ref, muk_ref, muv_ref, mul_ref, w0_ref, a0_ref,
                    w2_ref, a2_ref, g2_ref, kkw_ref, kaw_ref, rkw_ref, ones_ref,
                    r_ref, v_ref, kk_ref, lw_ref, kd_ref, bd_ref, bon_ref, g_ref):
    def shift(p, mu):
        prev, nxt = _shift_rows(p)
        return p + (0.5 * (prev + nxt) - p) * mu

    r = shift(pr_ref[...].astype(F32), mur_ref[...])
    k = shift(pk_ref[...].astype(F32), muk_ref[...])
    v = shift(pv_ref[...].astype(F32), muv_ref[...])
    lo = shift(pl_ref[...].astype(F32), mul_ref[...])
    ones = ones_ref[...]
    g_ref[...] = _dot(_sigmoid(lo), g2_ref[...], 1)
    kkr = k * kkw_ref[...]
    kk = kkr / jnp.maximum(jnp.sqrt(_dot_exact_r(kkr * kkr, ones)), 1e-12)
    th = jnp.tanh(lo)
    bon = jnp.zeros_like(r)
    for d in range(2):
        w = -_softplus(-(w0_ref[d:d + 1] + _dot(th, w2_ref[d], 3))) - 0.5
        lw_ref[d] = -jnp.exp(w)
        a = _sigmoid(a0_ref[d:d + 1] + _dot(lo, a2_ref[d], 1))
        kd = k * (1.0 + (a - 1.0) * kaw_ref[...])
        kd_ref[d] = kd
        bd_ref[d] = kk * a
        bon = bon + _dot_exact_r(r * kd * rkw_ref[...], ones) * v
    r_ref[...] = r
    v_ref[...] = v
    kk_ref[...] = kk
    bon_ref[...] = bon


def rwkv_prep(p, row0, nseq, L, mu, w0, w2, a0, a2, g2, k_k, k_a, r_k):
    b0 = row0 // L
    cb = RW_W if L <= 256 else RW_W // 2
    ncb = RW_W // cb
    c0 = 3 * HY_W // cb
    wide = lambda part: pl.BlockSpec((L, cb), lambda b, c: (b0 + b, c0 + part * ncb + c))
    lora = pl.BlockSpec((L, 256), lambda b, c: (b0 + b, (3 * HY_W + 3 * RW_W) // 256))
    muw = lambda part: pl.BlockSpec((1, cb), lambda b, c: (0, part * ncb + c))
    vec = lambda rows: pl.BlockSpec((rows, cb), lambda b, c: (0, c))
    mu_p = jnp.zeros((1, 3 * RW_W + 256), F32).at[0, :3 * RW_W + RW_LORA].set(mu)
    w2f = jnp.zeros((2, 256, RW_W), F32).at[0, 0:32].set(w2[0]).at[1, 32:64].set(w2[1])
    a2f = jnp.zeros((2, 256, RW_W), F32).at[0, 64:96].set(a2[0]).at[1, 96:128].set(a2[1])
    g2f = jnp.zeros((256, RW_W), F32).at[128:224].set(g2)
    n = nseq * L
    one = jax.ShapeDtypeStruct((n, RW_W), F32)
    two = jax.ShapeDtypeStruct((2, n, RW_W), F32)
    ospec1 = pl.BlockSpec((L, cb), lambda b, c: (b, c))
    ospec2 = pl.BlockSpec((2, L, cb), lambda b, c: (0, b, c))
    return pl.pallas_call(
        _rw_prep_kernel,
        grid=(nseq, ncb),
        in_specs=[wide(0), wide(1), wide(2), lora, muw(0), muw(1), muw(2),
                  pl.BlockSpec((1, 256), lambda b, c: (0, 3 * RW_W // 256)),
                  vec(2), vec(2),
                  pl.BlockSpec((2, 256, cb), lambda b, c: (0, 0, c)),
                  pl.BlockSpec((2, 256, cb), lambda b, c: (0, 0, c)),
                  pl.BlockSpec((256, cb), lambda b, c: (0, c)),
                  vec(1), vec(1), vec(1),
                  pl.BlockSpec((cb, cb), lambda b, c: (0, 0))],
        out_specs=[ospec1, ospec1, ospec1, ospec2, ospec2, ospec2, ospec1, ospec1],
        out_shape=[one, one, one, two, two, two, one, one],
        compiler_params=_cparams("arbitrary", "arbitrary"),
    )(p, p, p, p, mu_p, mu_p, mu_p, mu_p, w0, a0, w2f, a2f, g2f, k_k.reshape(1, RW_W), k_a.reshape(1, RW_W),
      r_k.reshape(1, RW_W), jnp.asarray(_block_ones(cb, RW_N)).astype(BF16))


def _rw_blocks(ch, passes):
    n = SUPER
    c = RW_CHUNK
    nc = n // c
    idx = range(len(ch))
    cum = [_dot_exact_l(x["masks"]["incl_bf"], x["lw"]) for x in ch]
    tot = [_chunk_totals(cum[i], c, ch[i]["fwd"]) for i in idx]
    suf = [tot[i] - cum[i] for i in idx]
    e_neg =[jnp.exp(-cum[i]) for i in idx]
    at = [ch[i]["a"] * jnp.exp(cum[i] - ch[i]["lw"]) for i in idx]
    rt = [ch[i]["r"] * jnp.exp(cum[i]) for i in idx]
    bk = [jnp.concatenate([ch[i]["b"] * e_neg[i], ch[i]["k"] * e_neg[i]], 0) for i in idx]
    e_suf = [jnp.exp(suf[i]) for i in idx]
    bp = [ch[i]["b"] * e_suf[i] for i in idx]
    kp = [ch[i]["k"] * e_suf[i] for i in idx]
    lane = lax.broadcasted_iota(jnp.int32, (1, LANES), 1)
    heads = range(LANES // RW_N)
    sub = [(i, g) for i in idx for g in heads]
    mg = [(lane >> 6) == g for g in heads]
    at_g = [jnp.where(mg[g], at[i], 0.0) for i, g in sub]
    v_g = [jnp.where(mg[g], ch[i]["v"], 0.0) for i, g in sub]
    m = [_dot(jnp.concatenate([at_g[j], jnp.where(mg[g], rt[i], 0.0)], 0), bk[i], passes, _NT)
         for j, (i, g) in enumerate(sub)]
    smask = [ch[i]["masks"] for i, g in sub]
    ab = [jnp.where(smask[j]["strict"], m[j][:n, :n], 0.0) for j in range(len(sub))]
    ak = [jnp.where(smask[j]["strict"], m[j][:n, n:], 0.0) for j in range(len(sub))]
    rb = [jnp.where(smask[j]["incl"], m[j][n:, :n], 0.0) for j in range(len(sub))]
    rk = [jnp.where(smask[j]["incl"], m[j][n:, n:], 0.0) for j in range(len(sub))]
    tinv = _tri_inv(ab, smask, passes)
    akv = [_dot(ak[j], v_g[j], passes) for j in range(len(sub))]
    aw = [_dot(tinv[j], jnp.concatenate([at_g[j], akv[j]], 1), passes) for j in range(len(sub))]
    ry = [_dot(rb[j], aw[j], passes) for j in range(len(sub))]
    rkv = [_dot(rk[j], v_g[j], passes) for j in range(len(sub))]
    nh = len(heads)
    ahat = [sum(aw[i * nh + g][:, :LANES] for g in heads) for i in idx]
    w1 = [sum(aw[i * nh + g][:, LANES:] for g in heads) for i in idx]
    rhat = [rt[i] + sum(ry[i * nh + g][:, :LANES] for g in heads) for i in idx]
    y0 = [sum(ry[i * nh + g][:, LANES:] + rkv[i * nh + g] for g in heads) for i in idx]
    rowl = lax.broadcasted_iota(jnp.int32, (LANES, LANES), 0)
    coll = lax.broadcasted_iota(jnp.int32, (LANES, LANES), 1)
    diag_blocks = (rowl >> 6) == (coll >> 6)
    s = [x["s"] for x in ch]
    ys = [[None] * nc for _ in ch]
    for step in range(nc):
        ci = [step if x["fwd"] else nc - 1 - step for x in ch]
        sl = [slice(ci[i] * c, (ci[i] + 1) * c) for i in idx]
        xx = [_dot(jnp.concatenate([ahat[i][sl[i]], rhat[i][sl[i]]], 0), s[i], passes, _NT) for i in idx]
        u = [w1[i][sl[i]] + xx[i][:c] for i in idx]
        for i in idx:
            ys[i][ci[i]] = y0[i][sl[i]] + xx[i][c:]
        upd = [_dot(jnp.concatenate([u[i], ch[i]["v"][sl[i]]], 0),
                    jnp.concatenate([bp[i][sl[i]], kp[i][sl[i]]], 0), passes, _TN) for i in idx]
        s = [s[i] * jnp.exp(tot[i][ci[i] * c:ci[i] * c + 1]) + jnp.where(diag_blocks, upd[i], 0.0) for i in idx]
    return [jnp.concatenate(y, 0) for y in ys], s


def _rw_scan_kernel(r_ref, v_ref, kk_ref, lw_ref, kd_ref, bd_ref, *rest, passes, gp):
    s0_ref, y_ref, s_ref = rest if len(rest) == 3 else (None,) + rest
    L = r_ref.shape[0]
    nblk = L // SUPER

    def body(i, carry):
        chains, where = [], []
        for d in range(2):
            masks = _chunk_masks(SUPER, RW_CHUNK, d == 0)
            j = i if d == 0 else nblk - 1 - i
            rows = pl.ds(pl.multiple_of(j * SUPER, SUPER), SUPER)
            for g in range(gp):
                cols = slice(g * LANES, (g + 1) * LANES)
                chains.append(dict(r=r_ref[rows, cols], lw=lw_ref[d, rows, cols], k=kd_ref[d, rows, cols],
                                   v=v_ref[rows, cols], a=-kk_ref[rows, cols], b=bd_ref[d, rows, cols],
                                   s=carry[d * gp + g], masks=masks, fwd=d == 0))
                where.append((d, rows, cols))
        ys, ss = _rw_blocks(chains, passes)
        for (d, rows, cols), y in zip(where, ys):
            y_ref[d, rows, cols] = y
        return tuple(ss)

    zero = jnp.zeros((LANES, LANES), F32)
    s_fin = lax.fori_loop(0, nblk, body,
                          tuple(zero if s0_ref is None else s0_ref[d, g] for d in range(2) for g in range(gp)))
    for d in range(2):
        for g in range(gp):
            s_ref[d, g] = s_fin[d * gp + g]


def rwkv_scan(r, v, kk, lw, kd, bd, s0_bd, nseq, L, passes, gp):
    ngrp = RW_W // LANES
    w = gp * LANES
    one = pl.BlockSpec((L, w), lambda b, g: (b, g))
    two = pl.BlockSpec((2, L, w), lambda b, g: (0, b, g))
    st = pl.BlockSpec((None, 2, gp, LANES, LANES), lambda b, g: (b, 0, g, 0, 0))
    return pl.pallas_call(
        functools.partial(_rw_scan_kernel, passes=passes, gp=gp),
        grid=(nseq, ngrp // gp),
        in_specs=[one, one, one, two, two, two] + ([] if s0_bd is None else [st]),
        out_specs=[two, st],
        out_shape=[jax.ShapeDtypeStruct((2, nseq * L, RW_W), F32),
                   jax.ShapeDtypeStruct((nseq, 2, ngrp, LANES, LANES), F32)],
        compiler_params=_cparams("arbitrary", "arbitrary"),
    )(r, v, kk, lw, kd, bd, *(() if s0_bd is None else (s0_bd,)))


def _rw_post_kernel(y_ref, bon_ref, g_ref, lnw_ref, lnb_ref, ones_ref, o_ref):
    y = y_ref[0] + y_ref[1]
    ones = ones_ref[...]
    mean = _dot_exact_r(y, ones) * (1.0 / RW_N)
    yc = y - mean
    var = _dot_exact_r(yc * yc, ones) * (1.0 / RW_N)
    yn = yc * lax.rsqrt(var + RW_LN_EPS) * lnw_ref[...] + lnb_ref[...]
    o_ref[...] = ((yn + bon_ref[...]) * g_ref[...]).astype(o_ref.dtype)


def rwkv_post(y2, bonus, g, ln_w, ln_b):
    n = bonus.shape[0]
    tm = 512
    row = pl.BlockSpec((tm, RW_W), lambda i: (i, 0))
    vec = pl.BlockSpec((1, RW_W), lambda i: (0, 0))
    return pl.pallas_call(
        _rw_post_kernel,
        grid=(n // tm,),
        in_specs=[pl.BlockSpec((2, tm, RW_W), lambda i: (0, i, 0)), row, row, vec, vec,
                  pl.BlockSpec((RW_W, RW_W), lambda i: (0, 0))],
        out_specs=row,
        out_shape=jax.ShapeDtypeStruct((n, RW_W), BF16),
        compiler_params=_cparams("arbitrary"),
    )(y2, bonus, g, ln_w.reshape(1, RW_W), ln_b.reshape(1, RW_W),
      jnp.asarray(_block_ones(RW_W, RW_N)).astype(BF16))


def _rw_states_to_blockdiag(s):
    b = s.shape[0]
    s = s.reshape(b, 2, 4, 2, RW_N, RW_N)
    z = jnp.zeros_like(s[:, :, :, 0])
    top = jnp.concatenate([s[:, :, :, 0], z], -1)
    bot = jnp.concatenate([z, s[:, :, :, 1]], -1)
    return jnp.concatenate([top, bot], -2)


def _rw_states_from_blockdiag(s):
    b = s.shape[0]
    return jnp.stack([s[:, :, :, :RW_N, :RW_N], s[:, :, :, RW_N:, RW_N:]], 3).reshape(b, 2, RW_H, RW_N, RW_N)


def _gd_blocks(ch, passes):
    n = SUPER
    c = GD_CHUNK
    nc = n // c
    idx = range(len(ch))
    msk = [x["masks"] for x in ch]
    cum = [_dot_exact_l(x["masks"]["incl_bf"], x["lw"]) for x in ch]
    tot = [_chunk_totals(cum[i], c, ch[i]["fwd"]) for i in idx]
    suf = [tot[i] - cum[i] for i in idx]
    cum_row = [cum[i].T for i in idx]
    gam = [jnp.exp(jnp.where(msk[i]["incl"], cum[i][:, 0:1] - cum_row[i], -jnp.inf)) for i in idx]
    kb = [x["k"] * x["beta"] for x in ch]
    m = [_dot(jnp.concatenate([kb[i], ch[i]["q"]], 0), ch[i]["k"], passes, _NT) for i in idx]
    a = [jnp.where(msk[i]["strict"], m[i][:n] * gam[i], 0.0) for i in idx]
    qk = [jnp.where(msk[i]["incl"], m[i][n:] * gam[i], 0.0) for i in idx]
    tinv = _tri_inv([-x for x in a], msk, passes)
    e_cum = [jnp.exp(cum[i]) for i in idx]
    uw = [_dot(tinv[i], jnp.concatenate([ch[i]["v"] * ch[i]["beta"], kb[i] * e_cum[i]], 1), passes) for i in idx]
    qq = [_dot(qk[i], uw[i], passes) for i in idx]
    u = [uw[i][:, :LANES] for i in idx]
    w = [uw[i][:, LANES:] for i in idx]
    o0 = [qq[i][:, :LANES] for i in idx]
    qhat = [ch[i]["q"] * e_cum[i] - qq[i][:, LANES:] for i in idx]
    kd = [ch[i]["k"] * jnp.exp(suf[i]) for i in idx]
    s = [x["s"] for x in ch]
    os_ = [[None] * nc for _ in ch]
    for step in range(nc):
        ci = [step if x["fwd"] else nc - 1 - step for x in ch]
        sl = [slice(ci[i] * c, (ci[i] + 1) * c) for i in idx]
        xx = [_dot(jnp.concatenate([w[i][sl[i]], qhat[i][sl[i]]], 0), s[i], passes) for i in idx]
        vn = [u[i][sl[i]] - xx[i][:c] for i in idx]
        for i in idx:
            os_[i][ci[i]] = o0[i][sl[i]] + xx[i][c:]
        upd = [_dot(kd[i][sl[i]], vn[i], passes, _TN) for i in idx]
        s = [s[i] * jnp.exp(tot[i][ci[i] * c:ci[i] * c + 1]) + upd[i] for i in idx]
    return [jnp.concatenate(o, 0) for o in os_], s


def _gd_kernel(pq_ref, pk_ref, pv_ref, pz_ref, pab_ref, wq_ref, wk_ref, wv_ref, alog_ref, dtb_ref, ng_ref,
               *rest, passes, hg):
    s0_ref, o_ref, s_ref, q_scr, k_scr, v_scr, gb_scr, o_scr = rest if len(rest) == 8 else (None,) + rest
    L = pq_ref.shape[0]
    h0 = pl.program_id(1) * hg
    l2n = lambda t: t * lax.rsqrt(jnp.sum(t * t, -1, keepdims=True) + 1e-6)
    pab = pab_ref[...].astype(F32)
    gb_scr[0] = -jnp.exp(alog_ref[...]) * _softplus(pab + dtb_ref[...])
    gb_scr[1] = _sigmoid(pab)
    for hh in range(hg):
        cols = slice(hh * LANES, (hh + 1) * LANES)
        q_scr[:, cols] = l2n(_silu(_conv3(pq_ref[:, cols].astype(F32), wq_ref[:, cols]))) * (GD_DK ** -0.5)
        k_scr[:, cols] = l2n(_silu(_conv3(pk_ref[:, cols].astype(F32), wk_ref[:, cols])))
        v_scr[:, cols] = _silu(_conv3(pv_ref[:, cols].astype(F32), wv_ref[:, cols]))
    nblk = L // SUPER
    lane = lax.broadcasted_iota(jnp.int32, (SUPER, LANES), 1)

    def pick(tile, which):
        return jnp.broadcast_to(jnp.sum(jnp.where(lane == which, tile, 0.0), -1, keepdims=True), tile.shape)

    def body(i, carry):
        chains, where = [], []
        for d in range(2):
            masks = _chunk_masks(SUPER, GD_CHUNK, d == 0)
            j = i if d == 0 else nblk - 1 - i
            rows = pl.ds(pl.multiple_of(j * SUPER, SUPER), SUPER)
            g_t, b_t = gb_scr[0, rows, :], gb_scr[1, rows, :]
            for hh in range(hg):
                cols = slice(hh * LANES, (hh + 1) * LANES)
                chains.append(dict(q=q_scr[rows, cols], k=k_scr[rows, cols], v=v_scr[rows, cols],
                                   lw=pick(g_t, d * GD_H + h0 + hh), beta=pick(b_t, 2 * GD_H + d * GD_H + h0 + hh),
                                   s=carry[d * hg + hh], masks=masks, fwd=d == 0))
                where.append((d, rows, cols))
        os_, ss = _gd_blocks(chains, passes)
        for (d, rows, cols), o in zip(where, os_):
            o_scr[d, rows, cols] = o
        return tuple(ss)

    zero = jnp.zeros((LANES, LANES), F32)
    s_fin = lax.fori_loop(0, nblk, body,
                          tuple(zero if s0_ref is None else s0_ref[d, hh] for d in range(2) for hh in range(hg)))
    for d in range(2):
        for hh in range(hg):
            s_ref[d, hh] = s_fin[d * hg + hh]
    for hh in range(hg):
        cols = slice(hh * LANES, (hh + 1) * LANES)
        o = o_scr[0, :, cols] + o_scr[1, :, cols]
        o = o * lax.rsqrt(jnp.mean(o * o, -1, keepdims=True) + NORM_EPS) * ng_ref[...]
        o_ref[:, cols] = (o * _silu(pz_ref[:, cols].astype(F32))).astype(o_ref.dtype)


def gdn_mix(p, row0, nseq, L, conv_w, a_log, dt_bias, norm_g, s0, passes, hg):
    b0 = row0 // L
    w = hg * LANES
    nhb = GD_H // hg
    col = lambda part: pl.BlockSpec((L, w), lambda b, h: (b0 + b, part * nhb + h))
    wcol = lambda part: pl.BlockSpec((3, w), lambda b, h: (0, part * nhb + h))
    vec = pl.BlockSpec((1, LANES), lambda b, h: (0, 0))
    st = pl.BlockSpec((None, 2, hg, LANES, LANES), lambda b, h: (b, 0, h, 0, 0))
    alog_row = jnp.zeros((1, LANES), F32).at[0, :2 * GD_H].set(a_log.reshape(-1))
    dtb_row = jnp.zeros((1, LANES), F32).at[0, :2 * GD_H].set(dt_bias.reshape(-1))
    return pl.pallas_call(
        functools.partial(_gd_kernel, passes=passes, hg=hg),
        grid=(nseq, nhb),
        in_specs=[col(0), col(1), col(2), col(3),
                  pl.BlockSpec((L, LANES), lambda b, h: (b0 + b, 4 * GD_H)),
                  wcol(0), wcol(1), wcol(2), vec, vec, vec] + ([] if s0 is None else [st]),
        out_specs=[pl.BlockSpec((L, w), lambda b, h: (b, h)), st],
        out_shape=[jax.ShapeDtypeStruct((nseq * L, GD_H * LANES), BF16),
                   jax.ShapeDtypeStruct((nseq, 2, GD_H, LANES, LANES), F32)],
        scratch_shapes=[pltpu.VMEM((L, w), F32), pltpu.VMEM((L, w), F32), pltpu.VMEM((L, w), F32),
                        pltpu.VMEM((2, L, LANES), F32), pltpu.VMEM((2, L, w), F32)],
        compiler_params=_cparams("arbitrary", "arbitrary"),
    )(p, p, p, p, p, conv_w, conv_w, conv_w, alog_row, dtb_row, norm_g.reshape(1, LANES),
      *(() if s0 is None else (s0,)))


def _router_kernel(x_ref, g_ref, sc_ref, sh_ref, wr_ref, h_ref, gate_ref, idx_ref, cnt_ref, cnt_scr):
    h = _norm_mod(x_ref[...], g_ref[...], sc_ref[...], sh_ref[...])
    h_ref[...] = h
    logits = _dot(h, wr_ref[...], 6)
    lane = lax.broadcasted_iota(jnp.int32, logits.shape, 1)
    logits = jnp.where(lane < N_EXP, logits, -jnp.inf)
    m1 = jnp.max(logits, -1, keepdims=True)
    i1 = jnp.min(jnp.where(logits == m1, lane, LANES), -1, keepdims=True)
    rest = jnp.where(lane == i1, -jnp.inf, logits)
    m2 = jnp.max(rest, -1, keepdims=True)
    i2 = jnp.min(jnp.where(rest == m2, lane, LANES), -1, keepdims=True)
    e2 = jnp.exp(m2 - m1)
    g1 = 1.0 / (1.0 + e2)
    g2 = e2 / (1.0 + e2)
    gate_ref[...] = jnp.where(lane == 0, g1, 0.0) + jnp.where(lane == 1, g2, 0.0)

    @pl.when(pl.program_id(0) == 0)
    def _():
        cnt_scr[...] = jnp.zeros_like(cnt_scr)

    tm = logits.shape[0]
    hot = jnp.where((lane == i1) | (lane == i2), 1.0, 0.0)
    row = lax.broadcasted_iota(jnp.int32, (tm, tm), 0)
    col = lax.broadcasted_iota(jnp.int32, (tm, tm), 1)
    before = _dg(_mask_bf16(col < row), hot.astype(BF16)) + cnt_scr[0:1]
    r1 = jnp.sum(jnp.where(lane == i1, before, 0.0), -1, keepdims=True).astype(jnp.int32)
    r2 = jnp.sum(jnp.where(lane == i2, before, 0.0), -1, keepdims=True).astype(jnp.int32)
    idx_ref[...] = (jnp.where(lane == 0, i1, 0) + jnp.where(lane == 1, i2, 0)
                    + jnp.where(lane == 2, r1, 0) + jnp.where(lane == 3, r2, 0))
    cnt_scr[...] = cnt_scr[...] + jnp.sum(hot, 0, keepdims=True)
    cnt_ref[...] = cnt_scr[...]


def moe_router(x, g, sc_t, sh_t, router):
    tm = 512
    per = TM // tm
    wr = jnp.zeros((D, LANES), F32).at[:, :N_EXP].set(router)
    return pl.pallas_call(
        _router_kernel,
        grid=(T_ALL // tm,),
        in_specs=[pl.BlockSpec((tm, D), lambda i: (i, 0)),
                  pl.BlockSpec((1, D), lambda i: (0, 0)),
                  pl.BlockSpec((None, 1, D), lambda i: (i // per, 0, 0)),
                  pl.BlockSpec((None, 1, D), lambda i: (i // per, 0, 0)),
                  pl.BlockSpec((D, LANES), lambda i: (0, 0))],
        out_specs=[pl.BlockSpec((tm, D), lambda i: (i, 0)), pl.BlockSpec((tm, LANES), lambda i: (i, 0)),
                   pl.BlockSpec((tm, LANES), lambda i: (i, 0)), pl.BlockSpec((8, LANES), lambda i: (0, 0))],
        out_shape=[jax.ShapeDtypeStruct((T_ALL, D), F32), jax.ShapeDtypeStruct((T_ALL, LANES), F32),
                   jax.ShapeDtypeStruct((T_ALL, LANES), jnp.int32), jax.ShapeDtypeStruct((8, LANES), F32)],
        scratch_shapes=[pltpu.VMEM((8, LANES), F32)],
        compiler_params=_cparams("arbitrary"),
    )(x, g.reshape(1, D), sc_t, sh_t, wr)


def moe_slot_positions(idx, cnt):
    e_flat = idx[:, 0:2].reshape(-1)
    rank = idx[:, 2:4].reshape(-1)
    onehot = (e_flat[:, None] == jnp.arange(N_EXP, dtype=jnp.int32)[None, :]).astype(jnp.int32)
    counts = cnt[0, :N_EXP].astype(jnp.int32)
    gsize = ((counts + MOE_TILE - 1) // MOE_TILE) * MOE_TILE
    gend = jnp.cumsum(gsize)
    pos = jnp.sum(onehot * (gend - gsize)[None, :], axis=1) + rank
    tile_start = jnp.arange(MOE_TILES, dtype=jnp.int32) * MOE_TILE
    tile_expert = jnp.minimum(jnp.sum((gend[None, :] <= tile_start[:, None]).astype(jnp.int32), axis=1), N_EXP - 1)
    return pos.astype(jnp.int32), tile_expert.astype(jnp.int32), (gend[-1:] // MOE_TILE).astype(jnp.int32)


def _row_copy(src, s, dst, d, sem):
    return pltpu.make_async_copy(src.at[pl.ds(s, 1)], dst.at[pl.ds(d, 1)], sem)


def _dispatch_kernel(pos_ref, h_ref, xs_in_ref, xs_ref, sem):
    del xs_in_ref
    tm = h_ref.shape[0]
    base = pl.program_id(0) * tm

    def issue(r, carry):
        for k in range(2):
            _row_copy(h_ref, r, xs_ref, pos_ref[(base + r) * 2 + k], sem).start()
        return carry

    lax.fori_loop(0, tm, issue, 0, unroll=8)
    for k in range(2):
        pltpu.make_async_copy(h_ref, xs_ref.at[pl.ds(0, tm)], sem).wait()


def moe_dispatch(h, pos):
    tm = 512
    return pl.pallas_call(
        _dispatch_kernel,
        grid_spec=pltpu.PrefetchScalarGridSpec(
            num_scalar_prefetch=1,
            grid=(T_ALL // tm,),
            in_specs=[pl.BlockSpec((tm, D), lambda i, pos: (i, 0)), pl.BlockSpec(memory_space=pl.ANY)],
            out_specs=pl.BlockSpec(memory_space=pl.ANY),
            scratch_shapes=[pltpu.SemaphoreType.DMA]),
        out_shape=jax.ShapeDtypeStruct((MOE_ROWS, D), F32),
        input_output_aliases={2: 0},
        compiler_params=_cparams("arbitrary"),
    )(pos, h, jnp.zeros((MOE_ROWS, D), F32))


def _expert_kernel(te_ref, nu_ref, xs_ref, wg_ref, wu_ref, wo_ref, o_ref, x_scr, acc):
    i = pl.program_id(0)
    f = pl.program_id(1)
    used = i < nu_ref[0]

    @pl.when(used & (f == 0))
    def _():
        x_scr[...] = xs_ref[...].astype(BF16)
        acc[...] = jnp.zeros_like(acc)

    @pl.when(used)
    def _():
        x = x_scr[...]
        act = _silu(_dg(x, wg_ref[...])) * _dg(x, wu_ref[...])
        acc[...] += _dg(act.astype(BF16), wo_ref[...])

    last = f == pl.num_programs(1) - 1

    @pl.when(used & last)
    def _():
        o_ref[...] = acc[...]

    @pl.when(jnp.logical_not(used) & last)
    def _():
        o_ref[...] = jnp.zeros_like(o_ref)


def moe_experts(xs, tile_expert, n_used, w_in_bf16, w_out_bf16):
    tn = 512
    nf = E_FF // tn
    live = lambda i, f, nu: jnp.where(i < nu[0], f, 0)
    return pl.pallas_call(
        _expert_kernel,
        grid_spec=pltpu.PrefetchScalarGridSpec(
            num_scalar_prefetch=2,
            grid=(MOE_TILES, nf),
            in_specs=[pl.BlockSpec((MOE_TILE, D), lambda i, f, te, nu: (i, 0)),
                      pl.BlockSpec((None, D, tn), lambda i, f, te, nu: (te[i], 0, live(i, f, nu))),
                      pl.BlockSpec((None, D, tn), lambda i, f, te, nu: (te[i], 0, live(i, f, nu) + nf)),
                      pl.BlockSpec((None, tn, D), lambda i, f, te, nu: (te[i], live(i, f, nu), 0))],
            out_specs=pl.BlockSpec((MOE_TILE, D), lambda i, f, te, nu: (i, 0)),
            scratch_shapes=[pltpu.VMEM((MOE_TILE, D), BF16), pltpu.VMEM((MOE_TILE, D), F32)]),
        out_shape=jax.ShapeDtypeStruct((MOE_ROWS, D), F32),
        compiler_params=_cparams("arbitrary", "arbitrary"),
    )(tile_expert, n_used, xs, w_in_bf16, w_in_bf16, w_out_bf16)


def _combine_kernel(pos_ref, ys_ref, gates_ref, x_ref, gate_ref, fg_ref, oc_ref, od_ref, y_scr, sem):
    tm = x_ref.shape[0]
    i = pl.program_id(0)
    base = i * tm

    def issue(r, carry):
        for k in range(2):
            _row_copy(ys_ref, pos_ref[(base + r) * 2 + k], y_scr.at[k], r, sem).start()
        return carry

    lax.fori_loop(0, tm, issue, 0, unroll=8)
    for k in range(2):
        pltpu.make_async_copy(ys_ref.at[pl.ds(0, tm)], y_scr.at[k], sem).wait()
    gates = gates_ref[...]
    lane = lax.broadcasted_iota(jnp.int32, gates.shape, 1)
    g0 = jnp.sum(jnp.where(lane == 0, gates, 0.0), -1, keepdims=True)
    g1 = jnp.sum(jnp.where(lane == 1, gates, 0.0), -1, keepdims=True)
    xn = x_ref[...] + gate_ref[...] * (y_scr[0] * g0 + y_scr[1] * g1)
    y = xn * lax.rsqrt(jnp.mean(xn * xn, -1, keepdims=True) + NORM_EPS) * fg_ref[...]

    @pl.when(i < T_CTX // tm)
    def _():
        oc_ref[...] = y

    @pl.when(i >= T_CTX // tm)
    def _():
        od_ref[...] = y


def moe_combine_final(ys, pos, gates, x, gate_t, final_g):
    tm = 256
    per = TM // tm
    nc = T_CTX // tm
    return pl.pallas_call(
        _combine_kernel,
        grid_spec=pltpu.PrefetchScalarGridSpec(
            num_scalar_prefetch=1,
            grid=(T_ALL // tm,),
            in_specs=[pl.BlockSpec(memory_space=pl.ANY),
                      pl.BlockSpec((tm, LANES), lambda i, pos: (i, 0)),
                      pl.BlockSpec((tm, D), lambda i, pos: (i, 0)),
                      pl.BlockSpec((None, 1, D), lambda i, pos: (i // per, 0, 0)),
                      pl.BlockSpec((1, D), lambda i, pos: (0, 0))],
            out_specs=[pl.BlockSpec((tm, D), lambda i, pos: (jnp.minimum(i, nc - 1), 0)),
                       pl.BlockSpec((tm, D), lambda i, pos: (jnp.maximum(i - nc, 0), 0))],
            scratch_shapes=[pltpu.VMEM((2, tm, D), F32), pltpu.SemaphoreType.DMA]),
        out_shape=[jax.ShapeDtypeStruct((T_CTX, D), F32), jax.ShapeDtypeStruct((T_DEN, D), F32)],
        compiler_params=_cparams("arbitrary"),
    )(pos, ys, gates, x, gate_t, final_g.reshape(1, D))


def _tile_rows(mod_l, k):
    cols = mod_l[:, k * D:(k + 1) * D]
    ctx = jnp.broadcast_to(cols[0:1], (T_CTX // TM, D))
    den = jnp.repeat(cols[1:1 + B_DEN], L_DEN // TM, axis=0)
    return jnp.concatenate([ctx, den], 0)[:, None, :]


def _pad_cols(w, n):
    return jnp.pad(w, ((0, 0), (0, n - w.shape[1])))


def kernel(x_prompt, x_sample, state_rwkv, state_gdn, c, c_ctx, ada_w, ada_b, norm_mix_g, norm_ffn_g, final_norm_g, e_w_in, e_hy_conv_w, e_hy_conv_b, e_hf_w1, e_hf_b1, e_hf_freq1, e_hf_w2, e_hf_b2, e_hf_freq2, e_hf_w3, e_hy_bias, e_rw_mu, e_rw_w0, e_rw_w2, e_rw_a0, e_rw_a2, e_rw_g2, e_rw_kk, e_rw_ka, e_rw_rk, e_rw_ln_w, e_rw_ln_b, e_w_out, e_ffn_w_in, e_ffn_w_out, o_w_in, o_conv_w, o_A_log, o_dt_bias, o_norm_g, o_w_out, o_router, o_moe_w_in, o_moe_w_out):
    passes = 1
    cond16 = jnp.zeros((16, D), F32).at[0].set(c_ctx).at[1:1 + B_DEN].set(c)
    mod = modulation(cond16, ada_w, ada_b)
    x = assemble_tokens(x_prompt, x_sample)

    m0 = [_tile_rows(mod[0], k) for k in range(6)]
    p = norm_matmul(x, norm_mix_g[0], m0[1], m0[0], _pad_cols(e_w_in[0], P_EVEN_PAD).astype(BF16),
                    P_EVEN_PAD // 2, BF16)
    ys_hy, ys_rw, st_rw = [], [], None
    for row0, nseq, L in ((0, B_CTX, L_CTX), (T_CTX, B_DEN, L_DEN)):
        spec = hyena_filter_spectrum(L, e_hf_w1[0], e_hf_b1[0], e_hf_freq1[0], e_hf_w2[0], e_hf_b2[0],
                                     e_hf_freq2[0], e_hf_w3[0])
        ys_hy.append(hyena_mix(p, row0, nseq, L, e_hy_conv_w[0], e_hy_conv_b[0].reshape(1, -1), spec,
                               e_hy_bias[0], 1))
        r, v, kk, lw, kd, bd, bonus, g = rwkv_prep(p, row0, nseq, L, e_rw_mu[0], e_rw_w0[0], e_rw_w2[0],
                                                   e_rw_a0[0], e_rw_a2[0], e_rw_g2[0], e_rw_kk[0], e_rw_ka[0],
                                                   e_rw_rk[0])
        s0 = None if row0 == 0 else _rw_states_to_blockdiag(state_rwkv[:, 0])
        y2, s_new = rwkv_scan(r, v, kk, lw, kd, bd, s0, nseq, L, passes, 4)
        if row0 == 0:
            st_rw = _rw_states_from_blockdiag(s_new)
        ys_rw.append(rwkv_post(y2, bonus, g, e_rw_ln_w[0], e_rw_ln_b[0]))
    x = matmul_residual_split([ys_hy, ys_rw], e_w_out[0].astype(BF16), x, m0[2])
    act = norm_swiglu(x, norm_ffn_g[0], m0[4], m0[3], e_ffn_w_in[0].astype(BF16), D_FF, D_FF // 2)
    x = matmul_residual(act, e_ffn_w_out[0].astype(BF16), x, m0[5], D // 2)

    m1 = [_tile_rows(mod[1], k) for k in range(6)]
    p = norm_matmul(x, norm_mix_g[1], m1[1], m1[0], _pad_cols(o_w_in[0], P_ODD_PAD).astype(BF16),
                    P_ODD_PAD // 3, BF16)
    os_, st_gd = [], None
    for row0, nseq, L in ((0, B_CTX, L_CTX), (T_CTX, B_DEN, L_DEN)):
        s0 = None if row0 == 0 else state_gdn[:, 0]
        o, s_new = gdn_mix(p, row0, nseq, L, o_conv_w[0], o_A_log[0], o_dt_bias[0], o_norm_g[0], s0, passes, 8)
        if row0 == 0:
            st_gd = s_new
        os_.append(o)
    x = matmul_residual_split([os_], o_w_out[0].astype(BF16), x, m1[2])
    h, gates, idx, cnt = moe_router(x, norm_ffn_g[1], m1[4], m1[3], o_router[0])
    pos, tile_expert, n_used = moe_slot_positions(idx, cnt)
    ys = moe_experts(moe_dispatch(h, pos), tile_expert, n_used, o_moe_w_in[0].astype(BF16),
                     o_moe_w_out[0].astype(BF16))
    y_ctx, y_den = moe_combine_final(ys, pos, gates, x, m1[5], final_norm_g)

    y_prompt = y_ctx.reshape(B_CTX, L_CTX, D)
    y_sample = y_den.reshape(B_DEN, L_DEN, D)
    return (y_prompt, y_sample, st_rw[:, None], st_gd[:, None])
```

```python
import functools
import math

import numpy as np
import jax
import jax.numpy as jnp
from jax import lax
from jax.experimental import pallas as pl
from jax.experimental.pallas import tpu as pltpu

F32 = jnp.float32
BF16 = jnp.bfloat16

D = 1024
B_CTX, L_CTX = 32, 256
B_DEN, L_DEN = 8, 1024
T_CTX = B_CTX * L_CTX
T_DEN = B_DEN * L_DEN
T_ALL = T_CTX + T_DEN
GRID_W = 64
NORM_EPS = 1e-6

HY_W = 512
HY_EMB = 33
HY_BANDS = 16
HY_FFN = 64
HY_TARGET, HY_FAST, HY_SLOW = 1e-2, 0.3, 1.5

RW_W = 512
RW_N = 64
RW_H = 8
RW_LORA = 224
RW_LN_EPS = 64e-5
P_EVEN = 3 * HY_W + 3 * RW_W + RW_LORA
P_EVEN_PAD = 3328

GD_H = 8
GD_DK = 128
GD_QKV = 3072
P_ODD = 4128
P_ODD_PAD = 4224

D_FF = 2816
N_EXP = 8
E_FF = 3584
MOE_TILE = 1024
MOE_ROWS = 2 * T_ALL + N_EXP * MOE_TILE
MOE_TILES = MOE_ROWS // MOE_TILE

LANES = 128
TM = 1024
RW_CHUNK = 32
GD_CHUNK = 64
SUPER = 128
VMEM_LIMIT = 56 * 1024 * 1024

_NN = (((1,), (0,)), ((), ()))
_NT = (((1,), (1,)), ((), ()))
_TN = (((0,), (0,)), ((), ()))


def _cparams(*sem):
    return pltpu.CompilerParams(dimension_semantics=sem, vmem_limit_bytes=VMEM_LIMIT)


def _dg(a, b, dims=_NN):
    return lax.dot_general(a, b, dims, preferred_element_type=F32)


def _split2(x):
    hi = x.astype(BF16)
    lo = (x - hi.astype(F32)).astype(BF16)
    return hi, lo


def _split3(x):
    x0 = x.astype(BF16)
    r1 = x - x0.astype(F32)
    x1 = r1.astype(BF16)
    x2 = (r1 - x1.astype(F32)).astype(BF16)
    return x0, x1, x2


def _dot(a, b, passes=1, dims=_NN):
    if passes == 1:
        return _dg(a.astype(BF16), b.astype(BF16), dims)
    if passes == 3:
        ah, al = _split2(a)
        bh, bl = _split2(b)
        return _dg(ah, bh, dims) + (_dg(ah, bl, dims) + _dg(al, bh, dims))
    a0, a1, a2 = _split3(a)
    b0, b1, b2 = _split3(b)
    small = _dg(a0, b2, dims) + _dg(a1, b1, dims) + _dg(a2, b0, dims)
    mid = _dg(a0, b1, dims) + _dg(a1, b0, dims)
    return _dg(a0, b0, dims) + (mid + small)


def _dot_exact_l(m, x, dims=_NN):
    x0, x1, x2 = _split3(x)
    return _dg(m, x0, dims) + (_dg(m, x1, dims) + _dg(m, x2, dims))


def _dot_exact_r(x, m, dims=_NN):
    x0, x1, x2 = _split3(x)
    return _dg(x0, m, dims) + (_dg(x1, m, dims) + _dg(x2, m, dims))


def _sigmoid(x):
    return 1.0 / (1.0 + jnp.exp(-x))


def _silu(x):
    return (0.5 * x) * (1.0 + jnp.tanh(0.5 * x))


def _softplus(x):
    return jnp.maximum(x, 0.0) + jnp.log(1.0 + jnp.exp(-jnp.abs(x)))


def _shift_rows(x):
    n = x.shape[0]
    row = lax.broadcasted_iota(jnp.int32, x.shape, 0)
    prev = jnp.where(row == 0, 0.0, pltpu.roll(x, 1, 0))
    nxt = jnp.where(row == n - 1, 0.0, pltpu.roll(x, n - 1, 0))
    return prev, nxt


def _conv3(x, w, b=None):
    prev, nxt = _shift_rows(x)
    y = prev * w[0:1] + x * w[1:2] + nxt * w[2:3]
    return y if b is None else y + b


def _chunk_masks(n, chunk, fwd):
    row = lax.broadcasted_iota(jnp.int32, (n, n), 0)
    col = lax.broadcasted_iota(jnp.int32, (n, n), 1)
    sh = int(math.log2(chunk))
    same = (row >> sh) == (col >> sh)
    before = (col < row) if fwd else (col > row)
    pair = (row >> 1) == (col >> 1)
    joins = [((row >> (lvl + 1)) == (col >> (lvl + 1))) & ((row >> lvl) != (col >> lvl))
             for lvl in range(1, sh)]
    incl = same & (before | (row == col))
    return dict(strict=same & before, incl=incl, eye=jnp.where(row == col, 1.0, 0.0), pair=pair, joins=joins,
                incl_bf=_mask_bf16(incl))


def _mask_bf16(m):
    return jnp.where(m, 1.0, 0.0).astype(BF16)


def _chunk_totals(cum, chunk, fwd):
    n = cum.shape[0]
    rows = [cum[(ci + 1) * chunk - 1:(ci + 1) * chunk] if fwd else cum[ci * chunk:ci * chunk + 1]
            for ci in range(n // chunk)]
    return jnp.concatenate([jnp.broadcast_to(r, (chunk, cum.shape[1])) for r in rows], 0)


def _tri_inv(xs, masks, passes):
    ts = [m["eye"] + jnp.where(m["pair"], x, 0.0) for x, m in zip(xs, masks)]
    for lvl in range(len(masks[0]["joins"])):
        ps = [_dot(jnp.where(m["joins"][lvl], x, 0.0), t, passes) for x, m, t in zip(xs, masks, ts)]
        ts = [t + _dot(t, p, passes) for t, p in zip(ts, ps)]
    return ts


@functools.lru_cache(maxsize=None)
def _pos_table():
    t = np.arange(L_DEN)
    row = (t // GRID_W).astype(np.float32)
    col = (t % GRID_W).astype(np.float32)
    q = D // 4
    omega = np.exp(-math.log(10000.0) * np.arange(q, dtype=np.float32) / q).astype(np.float32)
    enc = lambda pos: np.concatenate([np.sin(pos[:, None] * omega), np.cos(pos[:, None] * omega)], -1)
    return np.concatenate([enc(row), enc(col)], -1).astype(np.float32)


@functools.lru_cache(maxsize=None)
def _dft_tables(L):
    f = np.arange(L, dtype=np.int64)
    m = (f[:, None] * f[None, :]) % (2 * L)
    ang = np.pi * m.astype(np.float64) / L
    return np.cos(ang).astype(np.float32), np.sin(ang).astype(np.float32)


@functools.lru_cache(maxsize=None)
def _hyena_static(L):
    k = np.arange(L, dtype=np.float32)
    t = k / np.float32(L - 1)
    bands = np.linspace(1e-4, HY_BANDS - 1, HY_BANDS, dtype=np.float32)
    ang = (np.float32(2.0 * math.pi) * k / np.float32(L))[:, None] * bands[None, :]
    feats = np.concatenate([t[:, None], np.cos(ang), -np.sin(ang)], -1).astype(np.float32)
    feats_p = np.zeros((L, LANES), np.float32)
    feats_p[:, :HY_EMB] = feats
    deltas = np.abs(np.linspace(math.log(HY_TARGET) / HY_FAST, math.log(HY_TARGET) / HY_SLOW, HY_W,
                                dtype=np.float32))
    window = np.exp(-t[:, None] * deltas[None, :]).astype(np.float32)
    return feats_p, window


def _block_ones(n, blk):
    i = np.arange(n) // blk
    return (i[:, None] == i[None, :]).astype(np.float32)


def _mod_kernel(c_ref, w_ref, b_ref, o_ref):
    o_ref[...] = _dot(_silu(c_ref[...]), w_ref[...], 6) + b_ref[...]


def modulation(cond16, ada_w, ada_b):
    depth = ada_w.shape[0]
    tn = 1024
    return pl.pallas_call(
        _mod_kernel,
        grid=(depth, 6 * D // tn),
        in_specs=[pl.BlockSpec((16, D), lambda i, j: (0, 0)),
                  pl.BlockSpec((None, D, tn), lambda i, j: (i, 0, j)),
                  pl.BlockSpec((None, 1, tn), lambda i, j: (i, 0, j))],
        out_specs=pl.BlockSpec((None, 16, tn), lambda i, j: (i, 0, j)),
        out_shape=jax.ShapeDtypeStruct((depth, 16, 6 * D), F32),
        compiler_params=_cparams("arbitrary", "arbitrary"),
    )(cond16, ada_w, ada_b.reshape(depth, 1, 6 * D))


def _assemble_kernel(xp_ref, xs_ref, pos_ref, o_ref):
    i = pl.program_id(0)

    @pl.when(i < T_CTX // 256)
    def _():
        o_ref[...] = xp_ref[...]

    @pl.when(i >= T_CTX // 256)
    def _():
        o_ref[...] = xs_ref[...] + pos_ref[...]


def assemble_tokens(x_prompt, x_sample):
    nc = T_CTX // 256
    pos = jnp.asarray(_pos_table())
    return pl.pallas_call(
        _assemble_kernel,
        grid=(T_ALL // 256,),
        in_specs=[pl.BlockSpec((256, D), lambda i: (jnp.minimum(i, nc - 1), 0)),
                  pl.BlockSpec((256, D), lambda i: (jnp.maximum(i - nc, 0), 0)),
                  pl.BlockSpec((256, D), lambda i: (jnp.maximum(i - nc, 0) % (L_DEN // 256), 0))],
        out_specs=pl.BlockSpec((256, D), lambda i: (i, 0)),
        out_shape=jax.ShapeDtypeStruct((T_ALL, D), F32),
        compiler_params=_cparams("arbitrary"),
    )(x_prompt.reshape(T_CTX, D), x_sample.reshape(T_DEN, D), pos)


def _norm_mod(x, g, sc, sh):
    y = x * lax.rsqrt(jnp.mean(x * x, -1, keepdims=True) + NORM_EPS)
    return (y * g) * (1.0 + sc) + sh


def _norm_mm_kernel(x_ref, g_ref, sc_ref, sh_ref, w_ref, o_ref, h_scr):
    @pl.when(pl.program_id(1) == 0)
    def _():
        h_scr[...] = _norm_mod(x_ref[...], g_ref[...], sc_ref[...], sh_ref[...]).astype(BF16)

    o_ref[...] = _dg(h_scr[...], w_ref[...]).astype(o_ref.dtype)


def norm_matmul(x, g, sc_t, sh_t, w_bf16, tn, out_dtype=F32):
    n = w_bf16.shape[1]
    return pl.pallas_call(
        _norm_mm_kernel,
        grid=(T_ALL // TM, n // tn),
        in_specs=[pl.BlockSpec((TM, D), lambda i, j: (i, 0)),
                  pl.BlockSpec((1, D), lambda i, j: (0, 0)),
                  pl.BlockSpec((None, 1, D), lambda i, j: (i, 0, 0)),
                  pl.BlockSpec((None, 1, D), lambda i, j: (i, 0, 0)),
                  pl.BlockSpec((D, tn), lambda i, j: (0, j))],
        out_specs=pl.BlockSpec((TM, tn), lambda i, j: (i, j)),
        out_shape=jax.ShapeDtypeStruct((T_ALL, n), out_dtype),
        scratch_shapes=[pltpu.VMEM((TM, D), BF16)],
        compiler_params=_cparams("arbitrary", "arbitrary"),
    )(x, g.reshape(1, D), sc_t, sh_t, w_bf16)


def _norm_swiglu_kernel(x_ref, g_ref, sc_ref, sh_ref, wg_ref, wu_ref, o_ref, h_scr):
    @pl.when(pl.program_id(1) == 0)
    def _():
        h_scr[...] = _norm_mod(x_ref[...], g_ref[...], sc_ref[...], sh_ref[...]).astype(BF16)

    h = h_scr[...]
    o_ref[...] = (_silu(_dg(h, wg_ref[...])) * _dg(h, wu_ref[...])).astype(o_ref.dtype)


def norm_swiglu(x, g, sc_t, sh_t, w_in_bf16, dff, tn):
    nj = dff // tn
    return pl.pallas_call(
        _norm_swiglu_kernel,
        grid=(T_ALL // TM, nj),
        in_specs=[pl.BlockSpec((TM, D), lambda i, j: (i, 0)),
                  pl.BlockSpec((1, D), lambda i, j: (0, 0)),
                  pl.BlockSpec((None, 1, D), lambda i, j: (i, 0, 0)),
                  pl.BlockSpec((None, 1, D), lambda i, j: (i, 0, 0)),
                  pl.BlockSpec((D, tn), lambda i, j: (0, j)),
                  pl.BlockSpec((D, tn), lambda i, j: (0, j + nj))],
        out_specs=pl.BlockSpec((TM, tn), lambda i, j: (i, j)),
        out_shape=jax.ShapeDtypeStruct((T_ALL, dff), BF16),
        scratch_shapes=[pltpu.VMEM((TM, D), BF16)],
        compiler_params=_cparams("arbitrary", "arbitrary"),
    )(x, g.reshape(1, D), sc_t, sh_t, w_in_bf16, w_in_bf16)


def _mm_res_kernel(y_ref, w_ref, x_ref, gate_ref, o_ref):
    o_ref[...] = x_ref[...] + gate_ref[...] * _dg(y_ref[...], w_ref[...])


def matmul_residual(y_bf16, w_bf16, x, gate_t, tn):
    k = y_bf16.shape[1]
    return pl.pallas_call(
        _mm_res_kernel,
        grid=(T_ALL // TM, D // tn),
        in_specs=[pl.BlockSpec((TM, k), lambda i, j: (i, 0)),
                  pl.BlockSpec((k, tn), lambda i, j: (0, j)),
                  pl.BlockSpec((TM, tn), lambda i, j: (i, j)),
                  pl.BlockSpec((None, 1, tn), lambda i, j: (i, 0, j))],
        out_specs=pl.BlockSpec((TM, tn), lambda i, j: (i, j)),
        out_shape=jax.ShapeDtypeStruct((T_ALL, D), F32),
        compiler_params=_cparams("arbitrary", "arbitrary"),
    )(y_bf16, w_bf16, x, gate_t)


def _mm_res_split_kernel(*refs, widths):
    n = len(widths)
    ctx_refs, den_refs = refs[0:2 * n:2], refs[1:2 * n:2]
    w_ref, x_ref, gate_ref, o_ref = refs[2 * n:]
    i = pl.program_id(0)

    def run(y_refs):
        acc, k0 = None, 0
        for y_ref, kw in zip(y_refs, widths):
            part = _dg(y_ref[...], w_ref[k0:k0 + kw, :])
            acc = part if acc is None else acc + part
            k0 += kw
        o_ref[...] = x_ref[...] + gate_ref[...] * acc

    @pl.when(i < T_CTX // TM)
    def _():
        run(ctx_refs)

    @pl.when(i >= T_CTX // TM)
    def _():
        run(den_refs)


def matmul_residual_split(parts, w_bf16, x, gate_t):
    widths = tuple(pc.shape[1] for pc, _ in parts)
    nc = T_CTX // TM
    specs, args = [], []
    for (pc, pd), kw in zip(parts, widths):
        specs += [pl.BlockSpec((TM, kw), lambda i: (jnp.minimum(i, nc - 1), 0)),
                  pl.BlockSpec((TM, kw), lambda i: (jnp.maximum(i - nc, 0), 0))]
        args += [pc, pd]
    return pl.pallas_call(
        functools.partial(_mm_res_split_kernel, widths=widths),
        grid=(T_ALL // TM,),
        in_specs=specs + [pl.BlockSpec((sum(widths), D), lambda i: (0, 0)),
                          pl.BlockSpec((TM, D), lambda i: (i, 0)),
                          pl.BlockSpec((None, 1, D), lambda i: (i, 0, 0))],
        out_specs=pl.BlockSpec((TM, D), lambda i: (i, 0)),
        out_shape=jax.ShapeDtypeStruct((T_ALL, D), F32),
        compiler_params=_cparams("arbitrary"),
    )(*args, w_bf16, x, gate_t)


def _hy_filter_kernel(feat_ref, w1_ref, b1_ref, f1_ref, w2_ref, b2_ref, f2_ref, w3f_ref, w3b_ref, win_ref,
                      c_ref, s_ref, hr_ref, hi_ref, hn_ref):
    L = feat_ref.shape[0]
    h = jnp.sin(f1_ref[...] * (_dot(feat_ref[...], w1_ref[...], 6) + b1_ref[...]))
    h = jnp.sin(f2_ref[...] * (_dot(h, w2_ref[...], 6) + b2_ref[...]))
    win = win_ref[...]
    fw = _dot(h, w3f_ref[...], 6) * win
    bw = _dot(h, w3b_ref[...], 6) * win
    row = lax.broadcasted_iota(jnp.int32, fw.shape, 0)
    bw = jnp.where(row == 0, 0.0, bw)
    nrm = jnp.sum(jnp.abs(fw), 0, keepdims=True) + jnp.sum(jnp.abs(bw), 0, keepdims=True)
    ev = (fw + bw) / nrm
    od = (bw - fw) / nrm
    alt = (1 - 2 * (row & 1)).astype(F32)
    hr_ref[...] = _dot(c_ref[...], ev, 3)
    hi_ref[...] = _dot(s_ref[...], od, 3)
    hn_ref[...] = jnp.broadcast_to(jnp.sum(ev * alt, 0, keepdims=True), (8, ev.shape[1]))
    del L


def hyena_filter_spectrum(L, w1, b1, f1, w2, b2, f2, w3):
    feats, window = _hyena_static(L)
    cos_t, sin_t = _dft_tables(L)
    tc = 256
    ncb = HY_W // tc
    w1p = jnp.zeros((LANES, HY_FFN), F32).at[:HY_EMB].set(w1)
    const = lambda shape: pl.BlockSpec(shape, lambda o, c: (0,) * len(shape))
    return pl.pallas_call(
        _hy_filter_kernel,
        grid=(2, ncb),
        in_specs=[const((L, LANES)), const((LANES, HY_FFN)), const((1, HY_FFN)), const((1, HY_FFN)),
                  const((HY_FFN, HY_FFN)), const((1, HY_FFN)), const((1, HY_FFN)),
                  pl.BlockSpec((HY_FFN, tc), lambda o, c: (0, o * ncb + c)),
                  pl.BlockSpec((HY_FFN, tc), lambda o, c: (0, 2 * ncb + o * ncb + c)),
                  pl.BlockSpec((L, tc), lambda o, c: (0, c)),
                  const((L, L)), const((L, L))],
        out_specs=[pl.BlockSpec((None, L, tc), lambda o, c: (o, 0, c)),
                   pl.BlockSpec((None, L, tc), lambda o, c: (o, 0, c)),
                   pl.BlockSpec((None, 8, tc), lambda o, c: (o, 0, c))],
        out_shape=[jax.ShapeDtypeStruct((2, L, HY_W), F32), jax.ShapeDtypeStruct((2, L, HY_W), F32),
                   jax.ShapeDtypeStruct((2, 8, HY_W), F32)],
        compiler_params=_cparams("arbitrary", "arbitrary"),
    )(jnp.asarray(feats), w1p, b1.reshape(1, -1), f1.reshape(1, -1), w2, b2.reshape(1, -1), f2.reshape(1, -1),
      w3, w3, jnp.asarray(window), jnp.asarray(cos_t), jnp.asarray(sin_t))


def _hy_mix_kernel(pv_ref, p1_ref, p2_ref, wv_ref, w1_ref, w2_ref, bv_ref, b1_ref, b2_ref,
                   hr_ref, hi_ref, hn_ref, bias_ref, ch_ref, cl_ref, sh_ref, sl_ref, o_ref, *, passes):
    L = pv_ref.shape[0]
    z = _conv3(pv_ref[...].astype(F32), wv_ref[...], bv_ref[...])
    gates = (_conv3(p1_ref[...].astype(F32), w1_ref[...], b1_ref[...]),
             _conv3(p2_ref[...].astype(F32), w2_ref[...], b2_ref[...]))
    row = lax.broadcasted_iota(jnp.int32, z.shape, 0)
    alt = (1 - 2 * (row & 1)).astype(F32)
    ch, cl, sh, sl = ch_ref[...], cl_ref[...], sh_ref[...], sl_ref[...]

    def tdot(th, tl, x):
        if passes == 1:
            return _dg(th, x.astype(BF16))
        xh, xl = _split2(x)
        return _dg(th, xh) + (_dg(th, xl) + _dg(tl, xh))

    inv_l = 1.0 / L
    for o in range(2):
        hr, hi, hn = hr_ref[o], hi_ref[o], hn_ref[o][0:1]
        zc = tdot(ch, cl, z)
        zs = tdot(sh, sl, z)
        zn = jnp.sum(z * alt, 0, keepdims=True)
        yr = zc * hr + zs * hi
        yi = zc * hi - zs * hr
        wr = jnp.where(row == 0, 0.5 * inv_l, inv_l)
        conv = tdot(ch, cl, yr * wr) - tdot(sh, sl, yi * inv_l) + alt * (zn * hn * (0.5 * inv_l))
        z = gates[o] * (conv + z * bias_ref[o:o + 1])
    o_ref[...] = z.astype(o_ref.dtype)


def hyena_mix(p, row0, nseq, L, conv_w, conv_b, spec, bias, passes):
    hr, hi, hn = spec
    cos_t, sin_t = _dft_tables(L)
    ch, cl = _np_split2(cos_t)
    sh, sl = _np_split2(sin_t)
    tc = 512 if L <= 256 else 256
    ncb = HY_W // tc
    b0 = row0 // L
    pspec = lambda part: pl.BlockSpec((L, tc), lambda b, c: (b0 + b, part * ncb + c))
    wspec = lambda part: pl.BlockSpec((3, tc), lambda b, c: (0, part * ncb + c))
    bspec = lambda part: pl.BlockSpec((1, tc), lambda b, c: (0, part * ncb + c))
    hspec = lambda rows: pl.BlockSpec((2, rows, tc), lambda b, c: (0, 0, c))
    tab = pl.BlockSpec((L, L), lambda b, c: (0, 0))
    return pl.pallas_call(
        functools.partial(_hy_mix_kernel, passes=passes),
        grid=(nseq, ncb),
        in_specs=[pspec(0), pspec(1), pspec(2), wspec(0), wspec(1), wspec(2), bspec(0), bspec(1), bspec(2),
                  hspec(L), hspec(L), hspec(8), pl.BlockSpec((2, tc), lambda b, c: (0, c)), tab, tab, tab, tab],
        out_specs=pl.BlockSpec((L, tc), lambda b, c: (b, c)),
        out_shape=jax.ShapeDtypeStruct((nseq * L, HY_W), BF16),
        compiler_params=_cparams("arbitrary", "arbitrary"),
    )(p, p, p, conv_w, conv_w, conv_w, conv_b, conv_b, conv_b, hr, hi, hn, bias,
      jnp.asarray(ch), jnp.asarray(cl), jnp.asarray(sh), jnp.asarray(sl))


def _np_split2(x):
    hi = x.astype(jnp.bfloat16)
    lo = (x - hi.astype(np.float32)).astype(jnp.bfloat16)
    return hi, lo


def _rw_prep_kernel(pr_ref, pk_ref, pv_ref, pl_ref, mur_ref, muk_ref, muv_ref, mul_ref, w0_ref, a0_ref,
                    w2_ref, a2_ref, g2_ref, kkw_ref, kaw_ref, rkw_ref, ones_ref,
                    r_ref, v_ref, kk_ref, lw_ref, kd_ref, bd_ref, bon_ref, g_ref):
    def shift(p, mu):
        prev, nxt = _shift_rows(p)
        return p + (0.5 * (prev + nxt) - p) * mu

    r = shift(pr_ref[...].astype(F32), mur_ref[...])
    k = shift(pk_ref[...].astype(F32), muk_ref[...])
    v = shift(pv_ref[...].astype(F32), muv_ref[...])
    lo = shift(pl_ref[...].astype(F32), mul_ref[...])
    ones = ones_ref[...]
    g_ref[...] = _dot(_sigmoid(lo), g2_ref[...], 1)
    kkr = k * kkw_ref[...]
    kk = kkr / jnp.maximum(jnp.sqrt(_dot_exact_r(kkr * kkr, ones)), 1e-12)
    th = jnp.tanh(lo)
    bon = jnp.zeros_like(r)
    for d in range(2):
        w = -_softplus(-(w0_ref[d:d + 1] + _dot(th, w2_ref[d], 3))) - 0.5
        lw_ref[d] = -jnp.exp(w)
        a = _sigmoid(a0_ref[d:d + 1] + _dot(lo, a2_ref[d], 1))
        kd = k * (1.0 + (a - 1.0) * kaw_ref[...])
        kd_ref[d] = kd
        bd_ref[d] = kk * a
        bon = bon + _dot_exact_r(r * kd * rkw_ref[...], ones) * v
    r_ref[...] = r
    v_ref[...] = v
    kk_ref[...] = kk
    bon_ref[...] = bon


def rwkv_prep(p, row0, nseq, L, mu, w0, w2, a0, a2, g2, k_k, k_a, r_k):
    b0 = row0 // L
    cb = RW_W if L <= 256 else RW_W // 2
    ncb = RW_W // cb
    c0 = 3 * HY_W // cb
    wide = lambda part: pl.BlockSpec((L, cb), lambda b, c: (b0 + b, c0 + part * ncb + c))
    lora = pl.BlockSpec((L, 256), lambda b, c: (b0 + b, (3 * HY_W + 3 * RW_W) // 256))
    muw = lambda part: pl.BlockSpec((1, cb), lambda b, c: (0, part * ncb + c))
    vec = lambda rows: pl.BlockSpec((rows, cb), lambda b, c: (0, c))
    mu_p = jnp.zeros((1, 3 * RW_W + 256), F32).at[0, :3 * RW_W + RW_LORA].set(mu)
    w2f = jnp.zeros((2, 256, RW_W), F32).at[0, 0:32].set(w2[0]).at[1, 32:64].set(w2[1])
    a2f = jnp.zeros((2, 256, RW_W), F32).at[0, 64:96].set(a2[0]).at[1, 96:128].set(a2[1])
    g2f = jnp.zeros((256, RW_W), F32).at[128:224].set(g2)
    n = nseq * L
    one = jax.ShapeDtypeStruct((n, RW_W), F32)
    two = jax.ShapeDtypeStruct((2, n, RW_W), F32)
    ospec1 = pl.BlockSpec((L, cb), lambda b, c: (b, c))
    ospec2 = pl.BlockSpec((2, L, cb), lambda b, c: (0, b, c))
    return pl.pallas_call(
        _rw_prep_kernel,
        grid=(nseq, ncb),
        in_specs=[wide(0), wide(1), wide(2), lora, muw(0), muw(1), muw(2),
                  pl.BlockSpec((1, 256), lambda b, c: (0, 3 * RW_W // 256)),
                  vec(2), vec(2),
                  pl.BlockSpec((2, 256, cb), lambda b, c: (0, 0, c)),
                  pl.BlockSpec((2, 256, cb), lambda b, c: (0, 0, c)),
                  pl.BlockSpec((256, cb), lambda b, c: (0, c)),
                  vec(1), vec(1), vec(1),
                  pl.BlockSpec((cb, cb), lambda b, c: (0, 0))],
        out_specs=[ospec1, ospec1, ospec1, ospec2, ospec2, ospec2, ospec1, ospec1],
        out_shape=[one, one, one, two, two, two, one, one],
        compiler_params=_cparams("arbitrary", "arbitrary"),
    )(p, p, p, p, mu_p, mu_p, mu_p, mu_p, w0, a0, w2f, a2f, g2f, k_k.reshape(1, RW_W), k_a.reshape(1, RW_W),
      r_k.reshape(1, RW_W), jnp.asarray(_block_ones(cb, RW_N)).astype(BF16))


def _rw_blocks(ch, passes):
    n = SUPER
    c = RW_CHUNK
    nc = n // c
    idx = range(len(ch))
    cum = [_dot_exact_l(x["masks"]["incl_bf"], x["lw"]) for x in ch]
    tot = [_chunk_totals(cum[i], c, ch[i]["fwd"]) for i in idx]
    suf = [tot[i] - cum[i] for i in idx]
    e_neg =[jnp.exp(-cum[i]) for i in idx]
    at = [ch[i]["a"] * jnp.exp(cum[i] - ch[i]["lw"]) for i in idx]
    rt = [ch[i]["r"] * jnp.exp(cum[i]) for i in idx]
    bk = [jnp.concatenate([ch[i]["b"] * e_neg[i], ch[i]["k"] * e_neg[i]], 0) for i in idx]
    e_suf = [jnp.exp(suf[i]) for i in idx]
    bp = [ch[i]["b"] * e_suf[i] for i in idx]
    kp = [ch[i]["k"] * e_suf[i] for i in idx]
    lane = lax.broadcasted_iota(jnp.int32, (1, LANES), 1)
    heads = range(LANES // RW_N)
    sub = [(i, g) for i in idx for g in heads]
    mg = [(lane >> 6) == g for g in heads]
    at_g = [jnp.where(mg[g], at[i], 0.0) for i, g in sub]
    v_g = [jnp.where(mg[g], ch[i]["v"], 0.0) for i, g in sub]
    m = [_dot(jnp.concatenate([at_g[j], jnp.where(mg[g], rt[i], 0.0)], 0), bk[i], passes, _NT)
         for j, (i, g) in enumerate(sub)]
    smask = [ch[i]["masks"] for i, g in sub]
    ab = [jnp.where(smask[j]["strict"], m[j][:n, :n], 0.0) for j in range(len(sub))]
    ak = [jnp.where(smask[j]["strict"], m[j][:n, n:], 0.0) for j in range(len(sub))]
    rb = [jnp.where(smask[j]["incl"], m[j][n:, :n], 0.0) for j in range(len(sub))]
    rk = [jnp.where(smask[j]["incl"], m[j][n:, n:], 0.0) for j in range(len(sub))]
    tinv = _tri_inv(ab, smask, passes)
    akv = [_dot(ak[j], v_g[j], passes) for j in range(len(sub))]
    aw = [_dot(tinv[j], jnp.concatenate([at_g[j], akv[j]], 1), passes) for j in range(len(sub))]
    ry = [_dot(rb[j], aw[j], passes) for j in range(len(sub))]
    rkv = [_dot(rk[j], v_g[j], passes) for j in range(len(sub))]
    nh = len(heads)
    ahat = [sum(aw[i * nh + g][:, :LANES] for g in heads) for i in idx]
    w1 = [sum(aw[i * nh + g][:, LANES:] for g in heads) for i in idx]
    rhat = [rt[i] + sum(ry[i * nh + g][:, :LANES] for g in heads) for i in idx]
    y0 = [sum(ry[i * nh + g][:, LANES:] + rkv[i * nh + g] for g in heads) for i in idx]
    rowl = lax.broadcasted_iota(jnp.int32, (LANES, LANES), 0)
    coll = lax.broadcasted_iota(jnp.int32, (LANES, LANES), 1)
    diag_blocks = (rowl >> 6) == (coll >> 6)
    s = [x["s"] for x in ch]
    ys = [[None] * nc for _ in ch]
    for step in range(nc):
        ci = [step if x["fwd"] else nc - 1 - step for x in ch]
        sl = [slice(ci[i] * c, (ci[i] + 1) * c) for i in idx]
        xx = [_dot(jnp.concatenate([ahat[i][sl[i]], rhat[i][sl[i]]], 0), s[i], passes, _NT) for i in idx]
        u = [w1[i][sl[i]] + xx[i][:c] for i in idx]
        for i in idx:
            ys[i][ci[i]] = y0[i][sl[i]] + xx[i][c:]
        upd = [_dot(jnp.concatenate([u[i], ch[i]["v"][sl[i]]], 0),
                    jnp.concatenate([bp[i][sl[i]], kp[i][sl[i]]], 0), passes, _TN) for i in idx]
        s = [s[i] * jnp.exp(tot[i][ci[i] * c:ci[i] * c + 1]) + jnp.where(diag_blocks, upd[i], 0.0) for i in idx]
    return [jnp.concatenate(y, 0) for y in ys], s


def _rw_scan_kernel(r_ref, v_ref, kk_ref, lw_ref, kd_ref, bd_ref, *rest, passes, gp):
    s0_ref, y_ref, s_ref = rest if len(rest) == 3 else (None,) + rest
    L = r_ref.shape[0]
    nblk = L // SUPER

    def body(i, carry):
        chains, where = [], []
        for d in range(2):
            masks = _chunk_masks(SUPER, RW_CHUNK, d == 0)
            j = i if d == 0 else nblk - 1 - i
            rows = pl.ds(pl.multiple_of(j * SUPER, SUPER), SUPER)
            for g in range(gp):
                cols = slice(g * LANES, (g + 1) * LANES)
                chains.append(dict(r=r_ref[rows, cols], lw=lw_ref[d, rows, cols], k=kd_ref[d, rows, cols],
                                   v=v_ref[rows, cols], a=-kk_ref[rows, cols], b=bd_ref[d, rows, cols],
                                   s=carry[d * gp + g], masks=masks, fwd=d == 0))
                where.append((d, rows, cols))
        ys, ss = _rw_blocks(chains, passes)
        for (d, rows, cols), y in zip(where, ys):
            y_ref[d, rows, cols] = y
        return tuple(ss)

    zero = jnp.zeros((LANES, LANES), F32)
    s_fin = lax.fori_loop(0, nblk, body,
                          tuple(zero if s0_ref is None else s0_ref[d, g] for d in range(2) for g in range(gp)))
    for d in range(2):
        for g in range(gp):
            s_ref[d, g] = s_fin[d * gp + g]


def rwkv_scan(r, v, kk, lw, kd, bd, s0_bd, nseq, L, passes, gp):
    ngrp = RW_W // LANES
    w = gp * LANES
    one = pl.BlockSpec((L, w), lambda b, g: (b, g))
    two = pl.BlockSpec((2, L, w), lambda b, g: (0, b, g))
    st = pl.BlockSpec((None, 2, gp, LANES, LANES), lambda b, g: (b, 0, g, 0, 0))
    return pl.pallas_call(
        functools.partial(_rw_scan_kernel, passes=passes, gp=gp),
        grid=(nseq, ngrp // gp),
        in_specs=[one, one, one, two, two, two] + ([] if s0_bd is None else [st]),
        out_specs=[two, st],
        out_shape=[jax.ShapeDtypeStruct((2, nseq * L, RW_W), F32),
                   jax.ShapeDtypeStruct((nseq, 2, ngrp, LANES, LANES), F32)],
        compiler_params=_cparams("arbitrary", "arbitrary"),
    )(r, v, kk, lw, kd, bd, *(() if s0_bd is None else (s0_bd,)))


def _rw_post_kernel(y_ref, bon_ref, g_ref, lnw_ref, lnb_ref, ones_ref, o_ref):
    y = y_ref[0] + y_ref[1]
    ones = ones_ref[...]
    mean = _dot_exact_r(y, ones) * (1.0 / RW_N)
    yc = y - mean
    var = _dot_exact_r(yc * yc, ones) * (1.0 / RW_N)
    yn = yc * lax.rsqrt(var + RW_LN_EPS) * lnw_ref[...] + lnb_ref[...]
    o_ref[...] = ((yn + bon_ref[...]) * g_ref[...]).astype(o_ref.dtype)


def rwkv_post(y2, bonus, g, ln_w, ln_b):
    n = bonus.shape[0]
    tm = 512
    row = pl.BlockSpec((tm, RW_W), lambda i: (i, 0))
    vec = pl.BlockSpec((1, RW_W), lambda i: (0, 0))
    return pl.pallas_call(
        _rw_post_kernel,
        grid=(n // tm,),
        in_specs=[pl.BlockSpec((2, tm, RW_W), lambda i: (0, i, 0)), row, row, vec, vec,
                  pl.BlockSpec((RW_W, RW_W), lambda i: (0, 0))],
        out_specs=row,
        out_shape=jax.ShapeDtypeStruct((n, RW_W), BF16),
        compiler_params=_cparams("arbitrary"),
    )(y2, bonus, g, ln_w.reshape(1, RW_W), ln_b.reshape(1, RW_W),
      jnp.asarray(_block_ones(RW_W, RW_N)).astype(BF16))


def _rw_states_to_blockdiag(s):
    b = s.shape[0]
    s = s.reshape(b, 2, 4, 2, RW_N, RW_N)
    z = jnp.zeros_like(s[:, :, :, 0])
    top = jnp.concatenate([s[:, :, :, 0], z], -1)
    bot = jnp.concatenate([z, s[:, :, :, 1]], -1)
    return jnp.concatenate([top, bot], -2)


def _rw_states_from_blockdiag(s):
    b = s.shape[0]
    return jnp.stack([s[:, :, :, :RW_N, :RW_N], s[:, :, :, RW_N:, RW_N:]], 3).reshape(b, 2, RW_H, RW_N, RW_N)


def _gd_blocks(ch, passes):
    n = SUPER
    c = GD_CHUNK
    nc = n // c
    idx = range(len(ch))
    msk = [x["masks"] for x in ch]
    cum = [_dot_exact_l(x["masks"]["incl_bf"], x["lw"]) for x in ch]
    tot = [_chunk_totals(cum[i], c, ch[i]["fwd"]) for i in idx]
    suf = [tot[i] - cum[i] for i in idx]
    cum_row = [cum[i].T for i in idx]
    gam = [jnp.exp(jnp.where(msk[i]["incl"], cum[i][:, 0:1] - cum_row[i], -jnp.inf)) for i in idx]
    kb = [x["k"] * x["beta"] for x in ch]
    m = [_dot(jnp.concatenate([kb[i], ch[i]["q"]], 0), ch[i]["k"], passes, _NT) for i in idx]
    a = [jnp.where(msk[i]["strict"], m[i][:n] * gam[i], 0.0) for i in idx]
    qk = [jnp.where(msk[i]["incl"], m[i][n:] * gam[i], 0.0) for i in idx]
    tinv = _tri_inv([-x for x in a], msk, passes)
    e_cum = [jnp.exp(cum[i]) for i in idx]
    uw = [_dot(tinv[i], jnp.concatenate([ch[i]["v"] * ch[i]["beta"], kb[i] * e_cum[i]], 1), passes) for i in idx]
    qq = [_dot(qk[i], uw[i], passes) for i in idx]
    u = [uw[i][:, :LANES] for i in idx]
    w = [uw[i][:, LANES:] for i in idx]
    o0 = [qq[i][:, :LANES] for i in idx]
    qhat = [ch[i]["q"] * e_cum[i] - qq[i][:, LANES:] for i in idx]
    kd = [ch[i]["k"] * jnp.exp(suf[i]) for i in idx]
    s = [x["s"] for x in ch]
    os_ = [[None] * nc for _ in ch]
    for step in range(nc):
        ci = [step if x["fwd"] else nc - 1 - step for x in ch]
        sl = [slice(ci[i] * c, (ci[i] + 1) * c) for i in idx]
        xx = [_dot(jnp.concatenate([w[i][sl[i]], qhat[i][sl[i]]], 0), s[i], passes) for i in idx]
        vn = [u[i][sl[i]] - xx[i][:c] for i in idx]
        for i in idx:
            os_[i][ci[i]] = o0[i][sl[i]] + xx[i][c:]
        upd = [_dot(kd[i][sl[i]], vn[i], passes, _TN) for i in idx]
        s = [s[i] * jnp.exp(tot[i][ci[i] * c:ci[i] * c + 1]) + upd[i] for i in idx]
    return [jnp.concatenate(o, 0) for o in os_], s


def _gd_kernel(pq_ref, pk_ref, pv_ref, pz_ref, pab_ref, wq_ref, wk_ref, wv_ref, alog_ref, dtb_ref, ng_ref,
               *rest, passes, hg):
    s0_ref, o_ref, s_ref, q_scr, k_scr, v_scr, gb_scr, o_scr = rest if len(rest) == 8 else (None,) + rest
    L = pq_ref.shape[0]
    h0 = pl.program_id(1) * hg
    l2n = lambda t: t * lax.rsqrt(jnp.sum(t * t, -1, keepdims=True) + 1e-6)
    pab = pab_ref[...].astype(F32)
    gb_scr[0] = -jnp.exp(alog_ref[...]) * _softplus(pab + dtb_ref[...])
    gb_scr[1] = _sigmoid(pab)
    for hh in range(hg):
        cols = slice(hh * LANES, (hh + 1) * LANES)
        q_scr[:, cols] = l2n(_silu(_conv3(pq_ref[:, cols].astype(F32), wq_ref[:, cols]))) * (GD_DK ** -0.5)
        k_scr[:, cols] = l2n(_silu(_conv3(pk_ref[:, cols].astype(F32), wk_ref[:, cols])))
        v_scr[:, cols] = _silu(_conv3(pv_ref[:, cols].astype(F32), wv_ref[:, cols]))
    nblk = L // SUPER
    lane = lax.broadcasted_iota(jnp.int32, (SUPER, LANES), 1)

    def pick(tile, which):
        return jnp.broadcast_to(jnp.sum(jnp.where(lane == which, tile, 0.0), -1, keepdims=True), tile.shape)

    def body(i, carry):
        chains, where = [], []
        for d in range(2):
            masks = _chunk_masks(SUPER, GD_CHUNK, d == 0)
            j = i if d == 0 else nblk - 1 - i
            rows = pl.ds(pl.multiple_of(j * SUPER, SUPER), SUPER)
            g_t, b_t = gb_scr[0, rows, :], gb_scr[1, rows, :]
            for hh in range(hg):
                cols = slice(hh * LANES, (hh + 1) * LANES)
                chains.append(dict(q=q_scr[rows, cols], k=k_scr[rows, cols], v=v_scr[rows, cols],
                                   lw=pick(g_t, d * GD_H + h0 + hh), beta=pick(b_t, 2 * GD_H + d * GD_H + h0 + hh),
                                   s=carry[d * hg + hh], masks=masks, fwd=d == 0))
                where.append((d, rows, cols))
        os_, ss = _gd_blocks(chains, passes)
        for (d, rows, cols), o in zip(where, os_):
            o_scr[d, rows, cols] = o
        return tuple(ss)

    zero = jnp.zeros((LANES, LANES), F32)
    s_fin = lax.fori_loop(0, nblk, body,
                          tuple(zero if s0_ref is None else s0_ref[d, hh] for d in range(2) for hh in range(hg)))
    for d in range(2):
        for hh in range(hg):
            s_ref[d, hh] = s_fin[d * hg + hh]
    for hh in range(hg):
        cols = slice(hh * LANES, (hh + 1) * LANES)
        o = o_scr[0, :, cols] + o_scr[1, :, cols]
        o = o * lax.rsqrt(jnp.mean(o * o, -1, keepdims=True) + NORM_EPS) * ng_ref[...]
        o_ref[:, cols] = (o * _silu(pz_ref[:, cols].astype(F32))).astype(o_ref.dtype)


def gdn_mix(p, row0, nseq, L, conv_w, a_log, dt_bias, norm_g, s0, passes, hg):
    b0 = row0 // L
    w = hg * LANES
    nhb = GD_H // hg
    col = lambda part: pl.BlockSpec((L, w), lambda b, h: (b0 + b, part * nhb + h))
    wcol = lambda part: pl.BlockSpec((3, w), lambda b, h: (0, part * nhb + h))
    vec = pl.BlockSpec((1, LANES), lambda b, h: (0, 0))
    st = pl.BlockSpec((None, 2, hg, LANES, LANES), lambda b, h: (b, 0, h, 0, 0))
    alog_row = jnp.zeros((1, LANES), F32).at[0, :2 * GD_H].set(a_log.reshape(-1))
    dtb_row = jnp.zeros((1, LANES), F32).at[0, :2 * GD_H].set(dt_bias.reshape(-1))
    return pl.pallas_call(
        functools.partial(_gd_kernel, passes=passes, hg=hg),
        grid=(nseq, nhb),
        in_specs=[col(0), col(1), col(2), col(3),
                  pl.BlockSpec((L, LANES), lambda b, h: (b0 + b, 4 * GD_H)),
                  wcol(0), wcol(1), wcol(2), vec, vec, vec] + ([] if s0 is None else [st]),
        out_specs=[pl.BlockSpec((L, w), lambda b, h: (b, h)), st],
        out_shape=[jax.ShapeDtypeStruct((nseq * L, GD_H * LANES), BF16),
                   jax.ShapeDtypeStruct((nseq, 2, GD_H, LANES, LANES), F32)],
        scratch_shapes=[pltpu.VMEM((L, w), F32), pltpu.VMEM((L, w), F32), pltpu.VMEM((L, w), F32),
                        pltpu.VMEM((2, L, LANES), F32), pltpu.VMEM((2, L, w), F32)],
        compiler_params=_cparams("arbitrary", "arbitrary"),
    )(p, p, p, p, p, conv_w, conv_w, conv_w, alog_row, dtb_row, norm_g.reshape(1, LANES),
      *(() if s0 is None else (s0,)))


def _router_kernel(x_ref, g_ref, sc_ref, sh_ref, wr_ref, h_ref, gate_ref, idx_ref, cnt_ref, cnt_scr):
    h = _norm_mod(x_ref[...], g_ref[...], sc_ref[...], sh_ref[...])
    _rows_to_tiles(h_ref, h)
    logits = _dot(h, wr_ref[...], 6)
    lane = lax.broadcasted_iota(jnp.int32, logits.shape, 1)
    logits = jnp.where(lane < N_EXP, logits, -jnp.inf)
    m1 = jnp.max(logits, -1, keepdims=True)
    i1 = jnp.min(jnp.where(logits == m1, lane, LANES), -1, keepdims=True)
    rest = jnp.where(lane == i1, -jnp.inf, logits)
    m2 = jnp.max(rest, -1, keepdims=True)
    i2 = jnp.min(jnp.where(rest == m2, lane, LANES), -1, keepdims=True)
    e2 = jnp.exp(m2 - m1)
    g1 = 1.0 / (1.0 + e2)
    g2 = e2 / (1.0 + e2)
    gate_ref[...] = jnp.where(lane == 0, g1, 0.0) + jnp.where(lane == 1, g2, 0.0)

    @pl.when(pl.program_id(0) == 0)
    def _():
        cnt_scr[...] = jnp.zeros_like(cnt_scr)

    tm = logits.shape[0]
    hot = jnp.where((lane == i1) | (lane == i2), 1.0, 0.0)
    row = lax.broadcasted_iota(jnp.int32, (tm, tm), 0)
    col = lax.broadcasted_iota(jnp.int32, (tm, tm), 1)
    before = _dg(_mask_bf16(col < row), hot.astype(BF16)) + cnt_scr[0:1]
    r1 = jnp.sum(jnp.where(lane == i1, before, 0.0), -1, keepdims=True).astype(jnp.int32)
    r2 = jnp.sum(jnp.where(lane == i2, before, 0.0), -1, keepdims=True).astype(jnp.int32)
    idx_ref[...] = (jnp.where(lane == 0, i1, 0) + jnp.where(lane == 1, i2, 0)
                    + jnp.where(lane == 2, r1, 0) + jnp.where(lane == 3, r2, 0))
    cnt_scr[...] = cnt_scr[...] + jnp.sum(hot, 0, keepdims=True)
    cnt_ref[...] = cnt_scr[...]


def moe_router(x, g, sc_t, sh_t, router):
    tm = 512
    per = TM // tm
    wr = jnp.zeros((D, LANES), F32).at[:, :N_EXP].set(router)
    return pl.pallas_call(
        _router_kernel,
        grid=(T_ALL // tm,),
        in_specs=[pl.BlockSpec((tm, D), lambda i: (i, 0)),
                  pl.BlockSpec((1, D), lambda i: (0, 0)),
                  pl.BlockSpec((None, 1, D), lambda i: (i // per, 0, 0)),
                  pl.BlockSpec((None, 1, D), lambda i: (i // per, 0, 0)),
                  pl.BlockSpec((D, LANES), lambda i: (0, 0))],
        out_specs=[pl.BlockSpec((tm * ROW_TILE, LANES), lambda i: (i, 0)), pl.BlockSpec((tm, LANES), lambda i: (i, 0)),
                   pl.BlockSpec((tm, LANES), lambda i: (i, 0)), pl.BlockSpec((8, LANES), lambda i: (0, 0))],
        out_shape=[jax.ShapeDtypeStruct((T_ALL * ROW_TILE, LANES), F32), jax.ShapeDtypeStruct((T_ALL, LANES), F32),
                   jax.ShapeDtypeStruct((T_ALL, LANES), jnp.int32), jax.ShapeDtypeStruct((8, LANES), F32)],
        scratch_shapes=[pltpu.VMEM((8, LANES), F32)],
        compiler_params=_cparams("arbitrary"),
    )(x, g.reshape(1, D), sc_t, sh_t, wr)


def moe_slot_positions(idx, cnt):
    e_flat = idx[:, 0:2].reshape(-1)
    rank = idx[:, 2:4].reshape(-1)
    onehot = (e_flat[:, None] == jnp.arange(N_EXP, dtype=jnp.int32)[None, :]).astype(jnp.int32)
    counts = cnt[0, :N_EXP].astype(jnp.int32)
    gsize = ((counts + MOE_TILE - 1) // MOE_TILE) * MOE_TILE
    gend = jnp.cumsum(gsize)
    pos = jnp.sum(onehot * (gend - gsize)[None, :], axis=1) + rank
    tile_start = jnp.arange(MOE_TILES, dtype=jnp.int32) * MOE_TILE
    tile_expert = jnp.minimum(jnp.sum((gend[None, :] <= tile_start[:, None]).astype(jnp.int32), axis=1), N_EXP - 1)
    return pos.astype(jnp.int32), tile_expert.astype(jnp.int32), (gend[-1:] // MOE_TILE).astype(jnp.int32)


ROW_TILE = D // LANES


def _rows_to_tiles(ref, x):
    n = x.shape[0]
    for j in range(ROW_TILE):
        ref[pl.ds(j, n, stride=ROW_TILE), :] = x[:, j * LANES:(j + 1) * LANES]


def _tiles_to_rows(ref, n):
    return jnp.concatenate([ref[pl.ds(j, n, stride=ROW_TILE), :] for j in range(ROW_TILE)], axis=1)


def _row_copy(src, s, dst, d, sem):
    return pltpu.make_async_copy(src.at[pl.ds(pl.multiple_of(s * ROW_TILE, ROW_TILE), ROW_TILE)],
                                 dst.at[pl.ds(pl.multiple_of(d * ROW_TILE, ROW_TILE), ROW_TILE)], sem)


def _dispatch_kernel(pos_ref, h_ref, xs_in_ref, xs_ref, sem):
    del xs_in_ref
    tm = h_ref.shape[0] // ROW_TILE
    base = pl.program_id(0) * tm

    def issue(r, carry):
        for k in range(2):
            _row_copy(h_ref, r, xs_ref, pos_ref[(base + r) * 2 + k], sem).start()
        return carry

    lax.fori_loop(0, tm, issue, 0, unroll=8)
    for k in range(2):
        pltpu.make_async_copy(h_ref, xs_ref.at[pl.ds(0, tm * ROW_TILE)], sem).wait()


def moe_dispatch(h, pos):
    tm = 512
    return pl.pallas_call(
        _dispatch_kernel,
        grid_spec=pltpu.PrefetchScalarGridSpec(
            num_scalar_prefetch=1,
            grid=(T_ALL // tm,),
            in_specs=[pl.BlockSpec((tm * ROW_TILE, LANES), lambda i, pos: (i, 0)),
                      pl.BlockSpec(memory_space=pl.ANY)],
            out_specs=pl.BlockSpec(memory_space=pl.ANY),
            scratch_shapes=[pltpu.SemaphoreType.DMA]),
        out_shape=jax.ShapeDtypeStruct((MOE_ROWS * ROW_TILE, LANES), F32),
        input_output_aliases={2: 0},
        compiler_params=_cparams("arbitrary"),
    )(pos, h, jnp.zeros((MOE_ROWS * ROW_TILE, LANES), F32))


def _expert_kernel(te_ref, nu_ref, xs_ref, wg_ref, wu_ref, wo_ref, o_ref, x_scr, acc):
    i = pl.program_id(0)
    f = pl.program_id(1)
    used = i < nu_ref[0]

    @pl.when(used & (f == 0))
    def _():
        x_scr[...] = _tiles_to_rows(xs_ref, MOE_TILE).astype(BF16)
        acc[...] = jnp.zeros_like(acc)

    @pl.when(used)
    def _():
        x = x_scr[...]
        act = _silu(_dg(x, wg_ref[...])) * _dg(x, wu_ref[...])
        acc[...] += _dg(act.astype(BF16), wo_ref[...])

    last = f == pl.num_programs(1) - 1

    @pl.when(used & last)
    def _():
        _rows_to_tiles(o_ref, acc[...])

    @pl.when(jnp.logical_not(used) & last)
    def _():
        o_ref[...] = jnp.zeros_like(o_ref)


def moe_experts(xs, tile_expert, n_used, w_in_bf16, w_out_bf16):
    tn = 512
    nf = E_FF // tn
    live = lambda i, f, nu: jnp.where(i < nu[0], f, 0)
    return pl.pallas_call(
        _expert_kernel,
        grid_spec=pltpu.PrefetchScalarGridSpec(
            num_scalar_prefetch=2,
            grid=(MOE_TILES, nf),
            in_specs=[pl.BlockSpec((MOE_TILE * ROW_TILE, LANES), lambda i, f, te, nu: (i, 0)),
                      pl.BlockSpec((None, D, tn), lambda i, f, te, nu: (te[i], 0, live(i, f, nu))),
                      pl.BlockSpec((None, D, tn), lambda i, f, te, nu: (te[i], 0, live(i, f, nu) + nf)),
                      pl.BlockSpec((None, tn, D), lambda i, f, te, nu: (te[i], live(i, f, nu), 0))],
            out_specs=pl.BlockSpec((MOE_TILE * ROW_TILE, LANES), lambda i, f, te, nu: (i, 0)),
            scratch_shapes=[pltpu.VMEM((MOE_TILE, D), BF16), pltpu.VMEM((MOE_TILE, D), F32)]),
        out_shape=jax.ShapeDtypeStruct((MOE_ROWS * ROW_TILE, LANES), F32),
        compiler_params=_cparams("arbitrary", "arbitrary"),
    )(tile_expert, n_used, xs, w_in_bf16, w_in_bf16, w_out_bf16)


def _combine_kernel(pos_ref, ys_ref, gates_ref, x_ref, gate_ref, fg_ref, oc_ref, od_ref, y_scr, sem):
    tm = x_ref.shape[0]
    i = pl.program_id(0)
    base = i * tm

    def issue(r, carry):
        for k in range(2):
            _row_copy(ys_ref, pos_ref[(base + r) * 2 + k], y_scr.at[k], r, sem).start()
        return carry

    lax.fori_loop(0, tm, issue, 0, unroll=8)
    for k in range(2):
        pltpu.make_async_copy(ys_ref.at[pl.ds(0, tm * ROW_TILE)], y_scr.at[k], sem).wait()
    gates = gates_ref[...]
    lane = lax.broadcasted_iota(jnp.int32, gates.shape, 1)
    g0 = jnp.sum(jnp.where(lane == 0, gates, 0.0), -1, keepdims=True)
    g1 = jnp.sum(jnp.where(lane == 1, gates, 0.0), -1, keepdims=True)
    xn = x_ref[...] + gate_ref[...] * (_tiles_to_rows(y_scr.at[0], tm) * g0 + _tiles_to_rows(y_scr.at[1], tm) * g1)
    y = xn * lax.rsqrt(jnp.mean(xn * xn, -1, keepdims=True) + NORM_EPS) * fg_ref[...]

    @pl.when(i < T_CTX // tm)
    def _():
        oc_ref[...] = y

    @pl.when(i >= T_CTX // tm)
    def _():
        od_ref[...] = y


def moe_combine_final(ys, pos, gates, x, gate_t, final_g):
    tm = 256
    per = TM // tm
    nc = T_CTX // tm
    return pl.pallas_call(
        _combine_kernel,
        grid_spec=pltpu.PrefetchScalarGridSpec(
            num_scalar_prefetch=1,
            grid=(T_ALL // tm,),
            in_specs=[pl.BlockSpec(memory_space=pl.ANY),
                      pl.BlockSpec((tm, LANES), lambda i, pos: (i, 0)),
                      pl.BlockSpec((tm, D), lambda i, pos: (i, 0)),
                      pl.BlockSpec((None, 1, D), lambda i, pos: (i // per, 0, 0)),
                      pl.BlockSpec((1, D), lambda i, pos: (0, 0))],
            out_specs=[pl.BlockSpec((tm, D), lambda i, pos: (jnp.minimum(i, nc - 1), 0)),
                       pl.BlockSpec((tm, D), lambda i, pos: (jnp.maximum(i - nc, 0), 0))],
            scratch_shapes=[pltpu.VMEM((2, tm * ROW_TILE, LANES), F32), pltpu.SemaphoreType.DMA]),
        out_shape=[jax.ShapeDtypeStruct((T_CTX, D), F32), jax.ShapeDtypeStruct((T_DEN, D), F32)],
        compiler_params=_cparams("arbitrary"),
    )(pos, ys, gates, x, gate_t, final_g.reshape(1, D))


def _tile_rows(mod_l, k):
    cols = mod_l[:, k * D:(k + 1) * D]
    ctx = jnp.broadcast_to(cols[0:1], (T_CTX // TM, D))
    den = jnp.repeat(cols[1:1 + B_DEN], L_DEN // TM, axis=0)
    return jnp.concatenate([ctx, den], 0)[:, None, :]


def _pad_cols(w, n):
    return jnp.pad(w, ((0, 0), (0, n - w.shape[1])))


def kernel(x_prompt, x_sample, state_rwkv, state_gdn, c, c_ctx, ada_w, ada_b, norm_mix_g, norm_ffn_g, final_norm_g, e_w_in, e_hy_conv_w, e_hy_conv_b, e_hf_w1, e_hf_b1, e_hf_freq1, e_hf_w2, e_hf_b2, e_hf_freq2, e_hf_w3, e_hy_bias, e_rw_mu, e_rw_w0, e_rw_w2, e_rw_a0, e_rw_a2, e_rw_g2, e_rw_kk, e_rw_ka, e_rw_rk, e_rw_ln_w, e_rw_ln_b, e_w_out, e_ffn_w_in, e_ffn_w_out, o_w_in, o_conv_w, o_A_log, o_dt_bias, o_norm_g, o_w_out, o_router, o_moe_w_in, o_moe_w_out):
    passes = 1
    cond16 = jnp.zeros((16, D), F32).at[0].set(c_ctx).at[1:1 + B_DEN].set(c)
    mod = modulation(cond16, ada_w, ada_b)
    x = assemble_tokens(x_prompt, x_sample)

    m0 = [_tile_rows(mod[0], k) for k in range(6)]
    p = norm_matmul(x, norm_mix_g[0], m0[1], m0[0], _pad_cols(e_w_in[0], P_EVEN_PAD).astype(BF16),
                    P_EVEN_PAD // 2, BF16)
    ys_hy, ys_rw, st_rw = [], [], None
    for row0, nseq, L in ((0, B_CTX, L_CTX), (T_CTX, B_DEN, L_DEN)):
        spec = hyena_filter_spectrum(L, e_hf_w1[0], e_hf_b1[0], e_hf_freq1[0], e_hf_w2[0], e_hf_b2[0],
                                     e_hf_freq2[0], e_hf_w3[0])
        ys_hy.append(hyena_mix(p, row0, nseq, L, e_hy_conv_w[0], e_hy_conv_b[0].reshape(1, -1), spec,
                               e_hy_bias[0], 1))
        r, v, kk, lw, kd, bd, bonus, g = rwkv_prep(p, row0, nseq, L, e_rw_mu[0], e_rw_w0[0], e_rw_w2[0],
                                                   e_rw_a0[0], e_rw_a2[0], e_rw_g2[0], e_rw_kk[0], e_rw_ka[0],
                                                   e_rw_rk[0])
        s0 = None if row0 == 0 else _rw_states_to_blockdiag(state_rwkv[:, 0])
        y2, s_new = rwkv_scan(r, v, kk, lw, kd, bd, s0, nseq, L, passes, 4)
        if row0 == 0:
            st_rw = _rw_states_from_blockdiag(s_new)
        ys_rw.append(rwkv_post(y2, bonus, g, e_rw_ln_w[0], e_rw_ln_b[0]))
    x = matmul_residual_split([ys_hy, ys_rw], e_w_out[0].astype(BF16), x, m0[2])
    act = norm_swiglu(x, norm_ffn_g[0], m0[4], m0[3], e_ffn_w_in[0].astype(BF16), D_FF, D_FF // 2)
    x = matmul_residual(act, e_ffn_w_out[0].astype(BF16), x, m0[5], D // 2)

    m1 = [_tile_rows(mod[1], k) for k in range(6)]
    p = norm_matmul(x, norm_mix_g[1], m1[1], m1[0], _pad_cols(o_w_in[0], P_ODD_PAD).astype(BF16),
                    P_ODD_PAD // 3, BF16)
    os_, st_gd = [], None
    for row0, nseq, L in ((0, B_CTX, L_CTX), (T_CTX, B_DEN, L_DEN)):
        s0 = None if row0 == 0 else state_gdn[:, 0]
        o, s_new = gdn_mix(p, row0, nseq, L, o_conv_w[0], o_A_log[0], o_dt_bias[0], o_norm_g[0], s0, passes, 8)
        if row0 == 0:
            st_gd = s_new
        os_.append(o)
    x = matmul_residual_split([os_], o_w_out[0].astype(BF16), x, m1[2])
    h, gates, idx, cnt = moe_router(x, norm_ffn_g[1], m1[4], m1[3], o_router[0])
    pos, tile_expert, n_used = moe_slot_positions(idx, cnt)
    ys = moe_experts(moe_dispatch(h, pos), tile_expert, n_used, o_moe_w_in[0].astype(BF16),
                     o_moe_w_out[0].astype(BF16))
    y_ctx, y_den = moe_combine_final(ys, pos, gates, x, m1[5], final_norm_g)

    y_prompt = y_ctx.reshape(B_CTX, L_CTX, D)
    y_sample = y_den.reshape(B_DEN, L_DEN, D)
    return (y_prompt, y_sample, st_rw[:, None], st_gd[:, None])
```

```python
import functools
import math

import numpy as np
import jax
import jax.numpy as jnp
from jax import lax
from jax.experimental import pallas as pl
from jax.experimental.pallas import tpu as pltpu

F32 = jnp.float32
BF16 = jnp.bfloat16

D = 1024
B_CTX, L_CTX = 32, 256
B_DEN, L_DEN = 8, 1024
T_CTX = B_CTX * L_CTX
T_DEN = B_DEN * L_DEN
T_ALL = T_CTX + T_DEN
GRID_W = 64
NORM_EPS = 1e-6

HY_W = 512
HY_EMB = 33
HY_BANDS = 16
HY_FFN = 64
HY_TARGET, HY_FAST, HY_SLOW = 1e-2, 0.3, 1.5

RW_W = 512
RW_N = 64
RW_H = 8
RW_LORA = 224
RW_LN_EPS = 64e-5
P_EVEN = 3 * HY_W + 3 * RW_W + RW_LORA
P_EVEN_PAD = 3328

GD_H = 8
GD_DK = 128
GD_QKV = 3072
P_ODD = 4128
P_ODD_PAD = 4224

D_FF = 2816
N_EXP = 8
E_FF = 3584
MOE_TILE = 1024
MOE_ROWS = 2 * T_ALL + N_EXP * MOE_TILE
MOE_TILES = MOE_ROWS // MOE_TILE

LANES = 128
TM = 1024
RW_CHUNK = 32
GD_CHUNK = 64
SUPER = 128
VMEM_LIMIT = 56 * 1024 * 1024

_NN = (((1,), (0,)), ((), ()))
_NT = (((1,), (1,)), ((), ()))
_TN = (((0,), (0,)), ((), ()))


def _cparams(*sem):
    return pltpu.CompilerParams(dimension_semantics=sem, vmem_limit_bytes=VMEM_LIMIT)


def _dg(a, b, dims=_NN):
    return lax.dot_general(a, b, dims, preferred_element_type=F32)


def _split2(x):
    hi = x.astype(BF16)
    lo = (x - hi.astype(F32)).astype(BF16)
    return hi, lo


def _split3(x):
    x0 = x.astype(BF16)
    r1 = x - x0.astype(F32)
    x1 = r1.astype(BF16)
    x2 = (r1 - x1.astype(F32)).astype(BF16)
    return x0, x1, x2


def _dot(a, b, passes=1, dims=_NN):
    if passes == 1:
        return _dg(a.astype(BF16), b.astype(BF16), dims)
    if passes == 3:
        ah, al = _split2(a)
        bh, bl = _split2(b)
        return _dg(ah, bh, dims) + (_dg(ah, bl, dims) + _dg(al, bh, dims))
    a0, a1, a2 = _split3(a)
    b0, b1, b2 = _split3(b)
    small = _dg(a0, b2, dims) + _dg(a1, b1, dims) + _dg(a2, b0, dims)
    mid = _dg(a0, b1, dims) + _dg(a1, b0, dims)
    return _dg(a0, b0, dims) + (mid + small)


def _dot_exact_l(m, x, dims=_NN):
    x0, x1, x2 = _split3(x)
    return _dg(m, x0, dims) + (_dg(m, x1, dims) + _dg(m, x2, dims))


def _dot_exact_r(x, m, dims=_NN):
    x0, x1, x2 = _split3(x)
    return _dg(x0, m, dims) + (_dg(x1, m, dims) + _dg(x2, m, dims))


def _sigmoid(x):
    return 1.0 / (1.0 + jnp.exp(-x))


def _silu(x):
    return (0.5 * x) * (1.0 + jnp.tanh(0.5 * x))


def _softplus(x):
    return jnp.maximum(x, 0.0) + jnp.log(1.0 + jnp.exp(-jnp.abs(x)))


def _shift_rows(x):
    n = x.shape[0]
    row = lax.broadcasted_iota(jnp.int32, x.shape, 0)
    prev = jnp.where(row == 0, 0.0, pltpu.roll(x, 1, 0))
    nxt = jnp.where(row == n - 1, 0.0, pltpu.roll(x, n - 1, 0))
    return prev, nxt


def _conv3(x, w, b=None):
    prev, nxt = _shift_rows(x)
    y = prev * w[0:1] + x * w[1:2] + nxt * w[2:3]
    return y if b is None else y + b


def _chunk_masks(n, chunk, fwd):
    row = lax.broadcasted_iota(jnp.int32, (n, n), 0)
    col = lax.broadcasted_iota(jnp.int32, (n, n), 1)
    sh = int(math.log2(chunk))
    same = (row >> sh) == (col >> sh)
    before = (col < row) if fwd else (col > row)
    pair = (row >> 1) == (col >> 1)
    joins = [((row >> (lvl + 1)) == (col >> (lvl + 1))) & ((row >> lvl) != (col >> lvl))
             for lvl in range(1, sh)]
    incl = same & (before | (row == col))
    return dict(strict=same & before, incl=incl, eye=jnp.where(row == col, 1.0, 0.0), pair=pair, joins=joins,
                incl_bf=_mask_bf16(incl))


def _mask_bf16(m):
    return jnp.where(m, 1.0, 0.0).astype(BF16)


def _chunk_totals(cum, chunk, fwd):
    n = cum.shape[0]
    rows = [cum[(ci + 1) * chunk - 1:(ci + 1) * chunk] if fwd else cum[ci * chunk:ci * chunk + 1]
            for ci in range(n // chunk)]
    return jnp.concatenate([jnp.broadcast_to(r, (chunk, cum.shape[1])) for r in rows], 0)


def _tri_inv(xs, masks, passes):
    ts = [m["eye"] + jnp.where(m["pair"], x, 0.0) for x, m in zip(xs, masks)]
    for lvl in range(len(masks[0]["joins"])):
        ps = [_dot(jnp.where(m["joins"][lvl], x, 0.0), t, passes) for x, m, t in zip(xs, masks, ts)]
        ts = [t + _dot(t, p, passes) for t, p in zip(ts, ps)]
    return ts


@functools.lru_cache(maxsize=None)
def _pos_table():
    t = np.arange(L_DEN)
    row = (t // GRID_W).astype(np.float32)
    col = (t % GRID_W).astype(np.float32)
    q = D // 4
    omega = np.exp(-math.log(10000.0) * np.arange(q, dtype=np.float32) / q).astype(np.float32)
    enc = lambda pos: np.concatenate([np.sin(pos[:, None] * omega), np.cos(pos[:, None] * omega)], -1)
    return np.concatenate([enc(row), enc(col)], -1).astype(np.float32)


@functools.lru_cache(maxsize=None)
def _dft_tables(L):
    f = np.arange(L, dtype=np.int64)
    m = (f[:, None] * f[None, :]) % (2 * L)
    ang = np.pi * m.astype(np.float64) / L
    return np.cos(ang).astype(np.float32), np.sin(ang).astype(np.float32)


@functools.lru_cache(maxsize=None)
def _hyena_static(L):
    k = np.arange(L, dtype=np.float32)
    t = k / np.float32(L - 1)
    bands = np.linspace(1e-4, HY_BANDS - 1, HY_BANDS, dtype=np.float32)
    ang = (np.float32(2.0 * math.pi) * k / np.float32(L))[:, None] * bands[None, :]
    feats = np.concatenate([t[:, None], np.cos(ang), -np.sin(ang)], -1).astype(np.float32)
    feats_p = np.zeros((L, LANES), np.float32)
    feats_p[:, :HY_EMB] = feats
    deltas = np.abs(np.linspace(math.log(HY_TARGET) / HY_FAST, math.log(HY_TARGET) / HY_SLOW, HY_W,
                                dtype=np.float32))
    window = np.exp(-t[:, None] * deltas[None, :]).astype(np.float32)
    return feats_p, window


def _block_ones(n, blk):
    i = np.arange(n) // blk
    return (i[:, None] == i[None, :]).astype(np.float32)


def _mod_kernel(c_ref, w_ref, b_ref, o_ref):
    o_ref[...] = _dot(_silu(c_ref[...]), w_ref[...], 6) + b_ref[...]


def modulation(cond16, ada_w, ada_b):
    depth = ada_w.shape[0]
    tn = 1024
    return pl.pallas_call(
        _mod_kernel,
        grid=(depth, 6 * D // tn),
        in_specs=[pl.BlockSpec((16, D), lambda i, j: (0, 0)),
                  pl.BlockSpec((None, D, tn), lambda i, j: (i, 0, j)),
                  pl.BlockSpec((None, 1, tn), lambda i, j: (i, 0, j))],
        out_specs=pl.BlockSpec((None, 16, tn), lambda i, j: (i, 0, j)),
        out_shape=jax.ShapeDtypeStruct((depth, 16, 6 * D), F32),
        compiler_params=_cparams("arbitrary", "arbitrary"),
    )(cond16, ada_w, ada_b.reshape(depth, 1, 6 * D))


def _assemble_kernel(xp_ref, xs_ref, pos_ref, o_ref):
    i = pl.program_id(0)

    @pl.when(i < T_CTX // 256)
    def _():
        o_ref[...] = xp_ref[...]

    @pl.when(i >= T_CTX // 256)
    def _():
        o_ref[...] = xs_ref[...] + pos_ref[...]


def assemble_tokens(x_prompt, x_sample):
    nc = T_CTX // 256
    pos = jnp.asarray(_pos_table())
    return pl.pallas_call(
        _assemble_kernel,
        grid=(T_ALL // 256,),
        in_specs=[pl.BlockSpec((256, D), lambda i: (jnp.minimum(i, nc - 1), 0)),
                  pl.BlockSpec((256, D), lambda i: (jnp.maximum(i - nc, 0), 0)),
                  pl.BlockSpec((256, D), lambda i: (jnp.maximum(i - nc, 0) % (L_DEN // 256), 0))],
        out_specs=pl.BlockSpec((256, D), lambda i: (i, 0)),
        out_shape=jax.ShapeDtypeStruct((T_ALL, D), F32),
        compiler_params=_cparams("arbitrary"),
    )(x_prompt.reshape(T_CTX, D), x_sample.reshape(T_DEN, D), pos)


def _norm_mod(x, g, sc, sh):
    y = x * lax.rsqrt(jnp.mean(x * x, -1, keepdims=True) + NORM_EPS)
    return (y * g) * (1.0 + sc) + sh


def _norm_mm_kernel(x_ref, g_ref, sc_ref, sh_ref, w_ref, o_ref, h_scr):
    @pl.when(pl.program_id(1) == 0)
    def _():
        h_scr[...] = _norm_mod(x_ref[...], g_ref[...], sc_ref[...], sh_ref[...]).astype(BF16)

    o_ref[...] = _dg(h_scr[...], w_ref[...]).astype(o_ref.dtype)


def norm_matmul(x, g, sc_t, sh_t, w_bf16, tn, out_dtype=F32):
    n = w_bf16.shape[1]
    return pl.pallas_call(
        _norm_mm_kernel,
        grid=(T_ALL // TM, n // tn),
        in_specs=[pl.BlockSpec((TM, D), lambda i, j: (i, 0)),
                  pl.BlockSpec((1, D), lambda i, j: (0, 0)),
                  pl.BlockSpec((None, 1, D), lambda i, j: (i, 0, 0)),
                  pl.BlockSpec((None, 1, D), lambda i, j: (i, 0, 0)),
                  pl.BlockSpec((D, tn), lambda i, j: (0, j))],
        out_specs=pl.BlockSpec((TM, tn), lambda i, j: (i, j)),
        out_shape=jax.ShapeDtypeStruct((T_ALL, n), out_dtype),
        scratch_shapes=[pltpu.VMEM((TM, D), BF16)],
        compiler_params=_cparams("arbitrary", "arbitrary"),
    )(x, g.reshape(1, D), sc_t, sh_t, w_bf16)


def _norm_swiglu_kernel(x_ref, g_ref, sc_ref, sh_ref, wg_ref, wu_ref, o_ref, h_scr):
    @pl.when(pl.program_id(1) == 0)
    def _():
        h_scr[...] = _norm_mod(x_ref[...], g_ref[...], sc_ref[...], sh_ref[...]).astype(BF16)

    h = h_scr[...]
    o_ref[...] = (_silu(_dg(h, wg_ref[...])) * _dg(h, wu_ref[...])).astype(o_ref.dtype)


def norm_swiglu(x, g, sc_t, sh_t, w_in_bf16, dff, tn):
    nj = dff // tn
    return pl.pallas_call(
        _norm_swiglu_kernel,
        grid=(T_ALL // TM, nj),
        in_specs=[pl.BlockSpec((TM, D), lambda i, j: (i, 0)),
                  pl.BlockSpec((1, D), lambda i, j: (0, 0)),
                  pl.BlockSpec((None, 1, D), lambda i, j: (i, 0, 0)),
                  pl.BlockSpec((None, 1, D), lambda i, j: (i, 0, 0)),
                  pl.BlockSpec((D, tn), lambda i, j: (0, j)),
                  pl.BlockSpec((D, tn), lambda i, j: (0, j + nj))],
        out_specs=pl.BlockSpec((TM, tn), lambda i, j: (i, j)),
        out_shape=jax.ShapeDtypeStruct((T_ALL, dff), BF16),
        scratch_shapes=[pltpu.VMEM((TM, D), BF16)],
        compiler_params=_cparams("arbitrary", "arbitrary"),
    )(x, g.reshape(1, D), sc_t, sh_t, w_in_bf16, w_in_bf16)


def _mm_res_kernel(y_ref, w_ref, x_ref, gate_ref, o_ref):
    o_ref[...] = x_ref[...] + gate_ref[...] * _dg(y_ref[...], w_ref[...])


def matmul_residual(y_bf16, w_bf16, x, gate_t, tn):
    k = y_bf16.shape[1]
    return pl.pallas_call(
        _mm_res_kernel,
        grid=(T_ALL // TM, D // tn),
        in_specs=[pl.BlockSpec((TM, k), lambda i, j: (i, 0)),
                  pl.BlockSpec((k, tn), lambda i, j: (0, j)),
                  pl.BlockSpec((TM, tn), lambda i, j: (i, j)),
                  pl.BlockSpec((None, 1, tn), lambda i, j: (i, 0, j))],
        out_specs=pl.BlockSpec((TM, tn), lambda i, j: (i, j)),
        out_shape=jax.ShapeDtypeStruct((T_ALL, D), F32),
        compiler_params=_cparams("arbitrary", "arbitrary"),
    )(y_bf16, w_bf16, x, gate_t)


def _mm_res_split_kernel(*refs, widths):
    n = len(widths)
    ctx_refs, den_refs = refs[0:2 * n:2], refs[1:2 * n:2]
    w_ref, x_ref, gate_ref, o_ref = refs[2 * n:]
    i = pl.program_id(0)

    def run(y_refs):
        acc, k0 = None, 0
        for y_ref, kw in zip(y_refs, widths):
            part = _dg(y_ref[...], w_ref[k0:k0 + kw, :])
            acc = part if acc is None else acc + part
            k0 += kw
        o_ref[...] = x_ref[...] + gate_ref[...] * acc

    @pl.when(i < T_CTX // TM)
    def _():
        run(ctx_refs)

    @pl.when(i >= T_CTX // TM)
    def _():
        run(den_refs)


def matmul_residual_split(parts, w_bf16, x, gate_t):
    widths = tuple(pc.shape[1] for pc, _ in parts)
    nc = T_CTX // TM
    specs, args = [], []
    for (pc, pd), kw in zip(parts, widths):
        specs += [pl.BlockSpec((TM, kw), lambda i: (jnp.minimum(i, nc - 1), 0)),
                  pl.BlockSpec((TM, kw), lambda i: (jnp.maximum(i - nc, 0), 0))]
        args += [pc, pd]
    return pl.pallas_call(
        functools.partial(_mm_res_split_kernel, widths=widths),
        grid=(T_ALL // TM,),
        in_specs=specs + [pl.BlockSpec((sum(widths), D), lambda i: (0, 0)),
                          pl.BlockSpec((TM, D), lambda i: (i, 0)),
                          pl.BlockSpec((None, 1, D), lambda i: (i, 0, 0))],
        out_specs=pl.BlockSpec((TM, D), lambda i: (i, 0)),
        out_shape=jax.ShapeDtypeStruct((T_ALL, D), F32),
        compiler_params=_cparams("arbitrary"),
    )(*args, w_bf16, x, gate_t)


def _hy_filter_kernel(feat_ref, w1_ref, b1_ref, f1_ref, w2_ref, b2_ref, f2_ref, w3f_ref, w3b_ref, win_ref,
                      c_ref, s_ref, hr_ref, hi_ref, hn_ref):
    L = feat_ref.shape[0]
    h = jnp.sin(f1_ref[...] * (_dot(feat_ref[...], w1_ref[...], 6) + b1_ref[...]))
    h = jnp.sin(f2_ref[...] * (_dot(h, w2_ref[...], 6) + b2_ref[...]))
    win = win_ref[...]
    fw = _dot(h, w3f_ref[...], 6) * win
    bw = _dot(h, w3b_ref[...], 6) * win
    row = lax.broadcasted_iota(jnp.int32, fw.shape, 0)
    bw = jnp.where(row == 0, 0.0, bw)
    nrm = jnp.sum(jnp.abs(fw), 0, keepdims=True) + jnp.sum(jnp.abs(bw), 0, keepdims=True)
    ev = (fw + bw) / nrm
    od = (bw - fw) / nrm
    alt = (1 - 2 * (row & 1)).astype(F32)
    hr_ref[...] = _dot(c_ref[...], ev, 3)
    hi_ref[...] = _dot(s_ref[...], od, 3)
    hn_ref[...] = jnp.broadcast_to(jnp.sum(ev * alt, 0, keepdims=True), (8, ev.shape[1]))
    del L


def hyena_filter_spectrum(L, w1, b1, f1, w2, b2, f2, w3):
    feats, window = _hyena_static(L)
    cos_t, sin_t = _dft_tables(L)
    tc = 256
    ncb = HY_W // tc
    w1p = jnp.zeros((LANES, HY_FFN), F32).at[:HY_EMB].set(w1)
    const = lambda shape: pl.BlockSpec(shape, lambda o, c: (0,) * len(shape))
    return pl.pallas_call(
        _hy_filter_kernel,
        grid=(2, ncb),
        in_specs=[const((L, LANES)), const((LANES, HY_FFN)), const((1, HY_FFN)), const((1, HY_FFN)),
                  const((HY_FFN, HY_FFN)), const((1, HY_FFN)), const((1, HY_FFN)),
                  pl.BlockSpec((HY_FFN, tc), lambda o, c: (0, o * ncb + c)),
                  pl.BlockSpec((HY_FFN, tc), lambda o, c: (0, 2 * ncb + o * ncb + c)),
                  pl.BlockSpec((L, tc), lambda o, c: (0, c)),
                  const((L, L)), const((L, L))],
        out_specs=[pl.BlockSpec((None, L, tc), lambda o, c: (o, 0, c)),
                   pl.BlockSpec((None, L, tc), lambda o, c: (o, 0, c)),
                   pl.BlockSpec((None, 8, tc), lambda o, c: (o, 0, c))],
        out_shape=[jax.ShapeDtypeStruct((2, L, HY_W), F32), jax.ShapeDtypeStruct((2, L, HY_W), F32),
                   jax.ShapeDtypeStruct((2, 8, HY_W), F32)],
        compiler_params=_cparams("arbitrary", "arbitrary"),
    )(jnp.asarray(feats), w1p, b1.reshape(1, -1), f1.reshape(1, -1), w2, b2.reshape(1, -1), f2.reshape(1, -1),
      w3, w3, jnp.asarray(window), jnp.asarray(cos_t), jnp.asarray(sin_t))


def _hy_mix_kernel(pv_ref, p1_ref, p2_ref, wv_ref, w1_ref, w2_ref, bv_ref, b1_ref, b2_ref,
                   hr_ref, hi_ref, hn_ref, bias_ref, ch_ref, cl_ref, sh_ref, sl_ref, o_ref, *, passes):
    L = pv_ref.shape[0]
    z = _conv3(pv_ref[...].astype(F32), wv_ref[...], bv_ref[...])
    gates = (_conv3(p1_ref[...].astype(F32), w1_ref[...], b1_ref[...]),
             _conv3(p2_ref[...].astype(F32), w2_ref[...], b2_ref[...]))
    row = lax.broadcasted_iota(jnp.int32, z.shape, 0)
    alt = (1 - 2 * (row & 1)).astype(F32)
    ch, cl, sh, sl = ch_ref[...], cl_ref[...], sh_ref[...], sl_ref[...]

    def tdot(th, tl, x):
        if passes == 1:
            return _dg(th, x.astype(BF16))
        xh, xl = _split2(x)
        return _dg(th, xh) + (_dg(th, xl) + _dg(tl, xh))

    inv_l = 1.0 / L
    for o in range(2):
        hr, hi, hn = hr_ref[o], hi_ref[o], hn_ref[o][0:1]
        zc = tdot(ch, cl, z)
        zs = tdot(sh, sl, z)
        zn = jnp.sum(z * alt, 0, keepdims=True)
        yr = zc * hr + zs * hi
        yi = zc * hi - zs * hr
        wr = jnp.where(row == 0, 0.5 * inv_l, inv_l)
        conv = tdot(ch, cl, yr * wr) - tdot(sh, sl, yi * inv_l) + alt * (zn * hn * (0.5 * inv_l))
        z = gates[o] * (conv + z * bias_ref[o:o + 1])
    o_ref[...] = z.astype(o_ref.dtype)


def hyena_mix(p, row0, nseq, L, conv_w, conv_b, spec, bias, passes):
    hr, hi, hn = spec
    cos_t, sin_t = _dft_tables(L)
    ch, cl = _np_split2(cos_t)
    sh, sl = _np_split2(sin_t)
    tc = 512 if L <= 256 else 256
    ncb = HY_W // tc
    b0 = row0 // L
    pspec = lambda part: pl.BlockSpec((L, tc), lambda b, c: (b0 + b, part * ncb + c))
    wspec = lambda part: pl.BlockSpec((3, tc), lambda b, c: (0, part * ncb + c))
    bspec = lambda part: pl.BlockSpec((1, tc), lambda b, c: (0, part * ncb + c))
    hspec = lambda rows: pl.BlockSpec((2, rows, tc), lambda b, c: (0, 0, c))
    tab = pl.BlockSpec((L, L), lambda b, c: (0, 0))
    return pl.pallas_call(
        functools.partial(_hy_mix_kernel, passes=passes),
        grid=(nseq, ncb),
        in_specs=[pspec(0), pspec(1), pspec(2), wspec(0), wspec(1), wspec(2), bspec(0), bspec(1), bspec(2),
                  hspec(L), hspec(L), hspec(8), pl.BlockSpec((2, tc), lambda b, c: (0, c)), tab, tab, tab, tab],
        out_specs=pl.BlockSpec((L, tc), lambda b, c: (b, c)),
        out_shape=jax.ShapeDtypeStruct((nseq * L, HY_W), BF16),
        compiler_params=_cparams("arbitrary", "arbitrary"),
    )(p, p, p, conv_w, conv_w, conv_w, conv_b, conv_b, conv_b, hr, hi, hn, bias,
      jnp.asarray(ch), jnp.asarray(cl), jnp.asarray(sh), jnp.asarray(sl))


def _np_split2(x):
    hi = x.astype(jnp.bfloat16)
    lo = (x - hi.astype(np.float32)).astype(jnp.bfloat16)
    return hi, lo


def _rw_prep_kernel(pr_ref, pk_ref, pv_ref, pl_ref, mur_ref, muk_ref, muv_ref, mul_ref, w0_ref, a0_ref,
                    w2_ref, a2_ref, g2_ref, kkw_ref, kaw_ref, rkw_ref, ones_ref,
                    r_ref, v_ref, kk_ref, lw_ref, kd_ref, bd_ref, bon_ref, g_ref):
    def shift(p, mu):
        prev, nxt = _shift_rows(p)
        return p + (0.5 * (prev + nxt) - p) * mu

    r = shift(pr_ref[...].astype(F32), mur_ref[...])
    k = shift(pk_ref[...].astype(F32), muk_ref[...])
    v = shift(pv_ref[...].astype(F32), muv_ref[...])
    lo = shift(pl_ref[...].astype(F32), mul_ref[...])
    ones = ones_ref[...]
    g_ref[...] = _dot(_sigmoid(lo), g2_ref[...], 1)
    kkr = k * kkw_ref[...]
    kk = kkr / jnp.maximum(jnp.sqrt(_dot_exact_r(kkr * kkr, ones)), 1e-12)
    th = jnp.tanh(lo)
    bon = jnp.zeros_like(r)
    for d in range(2):
        w = -_softplus(-(w0_ref[d:d + 1] + _dot(th, w2_ref[d], 3))) - 0.5
        lw_ref[d] = -jnp.exp(w)
        a = _sigmoid(a0_ref[d:d + 1] + _dot(lo, a2_ref[d], 1))
        kd = k * (1.0 + (a - 1.0) * kaw_ref[...])
        kd_ref[d] = kd
        bd_ref[d] = kk * a
        bon = bon + _dot_exact_r(r * kd * rkw_ref[...], ones) * v
    r_ref[...] = r
    v_ref[...] = v
    kk_ref[...] = kk
    bon_ref[...] = bon


def rwkv_prep(p, row0, nseq, L, mu, w0, w2, a0, a2, g2, k_k, k_a, r_k):
    b0 = row0 // L
    cb = RW_W if L <= 256 else RW_W // 2
    ncb = RW_W // cb
    c0 = 3 * HY_W // cb
    wide = lambda part: pl.BlockSpec((L, cb), lambda b, c: (b0 + b, c0 + part * ncb + c))
    lora = pl.BlockSpec((L, 256), lambda b, c: (b0 + b, (3 * HY_W + 3 * RW_W) // 256))
    muw = lambda part: pl.BlockSpec((1, cb), lambda b, c: (0, part * ncb + c))
    vec = lambda rows: pl.BlockSpec((rows, cb), lambda b, c: (0, c))
    mu_p = jnp.zeros((1, 3 * RW_W + 256), F32).at[0, :3 * RW_W + RW_LORA].set(mu)
    w2f = jnp.zeros((2, 256, RW_W), F32).at[0, 0:32].set(w2[0]).at[1, 32:64].set(w2[1])
    a2f = jnp.zeros((2, 256, RW_W), F32).at[0, 64:96].set(a2[0]).at[1, 96:128].set(a2[1])
    g2f = jnp.zeros((256, RW_W), F32).at[128:224].set(g2)
    n = nseq * L
    one = jax.ShapeDtypeStruct((n, RW_W), F32)
    two = jax.ShapeDtypeStruct((2, n, RW_W), F32)
    ospec1 = pl.BlockSpec((L, cb), lambda b, c: (b, c))
    ospec2 = pl.BlockSpec((2, L, cb), lambda b, c: (0, b, c))
    return pl.pallas_call(
        _rw_prep_kernel,
        grid=(nseq, ncb),
        in_specs=[wide(0), wide(1), wide(2), lora, muw(0), muw(1), muw(2),
                  pl.BlockSpec((1, 256), lambda b, c: (0, 3 * RW_W // 256)),
                  vec(2), vec(2),
                  pl.BlockSpec((2, 256, cb), lambda b, c: (0, 0, c)),
                  pl.BlockSpec((2, 256, cb), lambda b, c: (0, 0, c)),
                  pl.BlockSpec((256, cb), lambda b, c: (0, c)),
                  vec(1), vec(1), vec(1),
                  pl.BlockSpec((cb, cb), lambda b, c: (0, 0))],
        out_specs=[ospec1, ospec1, ospec1, ospec2, ospec2, ospec2, ospec1, ospec1],
        out_shape=[one, one, one, two, two, two, one, one],
        compiler_params=_cparams("arbitrary", "arbitrary"),
    )(p, p, p, p, mu_p, mu_p, mu_p, mu_p, w0, a0, w2f, a2f, g2f, k_k.reshape(1, RW_W), k_a.reshape(1, RW_W),
      r_k.reshape(1, RW_W), jnp.asarray(_block_ones(cb, RW_N)).astype(BF16))


def _rw_blocks(ch, passes):
    n = SUPER
    c = RW_CHUNK
    nc = n // c
    idx = range(len(ch))
    cum = [_dot_exact_l(x["masks"]["incl_bf"], x["lw"]) for x in ch]
    tot = [_chunk_totals(cum[i], c, ch[i]["fwd"]) for i in idx]
    suf = [tot[i] - cum[i] for i in idx]
    e_neg =[jnp.exp(-cum[i]) for i in idx]
    at = [ch[i]["a"] * jnp.exp(cum[i] - ch[i]["lw"]) for i in idx]
    rt = [ch[i]["r"] * jnp.exp(cum[i]) for i in idx]
    bk = [jnp.concatenate([ch[i]["b"] * e_neg[i], ch[i]["k"] * e_neg[i]], 0) for i in idx]
    e_suf = [jnp.exp(suf[i]) for i in idx]
    bp = [ch[i]["b"] * e_suf[i] for i in idx]
    kp = [ch[i]["k"] * e_suf[i] for i in idx]
    lane = lax.broadcasted_iota(jnp.int32, (1, LANES), 1)
    heads = range(LANES // RW_N)
    sub = [(i, g) for i in idx for g in heads]
    mg = [(lane >> 6) == g for g in heads]
    at_g = [jnp.where(mg[g], at[i], 0.0) for i, g in sub]
    v_g = [jnp.where(mg[g], ch[i]["v"], 0.0) for i, g in sub]
    m = [_dot(jnp.concatenate([at_g[j], jnp.where(mg[g], rt[i], 0.0)], 0), bk[i], passes, _NT)
         for j, (i, g) in enumerate(sub)]
    smask = [ch[i]["masks"] for i, g in sub]
    ab = [jnp.where(smask[j]["strict"], m[j][:n, :n], 0.0) for j in range(len(sub))]
    ak = [jnp.where(smask[j]["strict"], m[j][:n, n:], 0.0) for j in range(len(sub))]
    rb = [jnp.where(smask[j]["incl"], m[j][n:, :n], 0.0) for j in range(len(sub))]
    rk = [jnp.where(smask[j]["incl"], m[j][n:, n:], 0.0) for j in range(len(sub))]
    tinv = _tri_inv(ab, smask, passes)
    akv = [_dot(ak[j], v_g[j], passes) for j in range(len(sub))]
    aw = [_dot(tinv[j], jnp.concatenate([at_g[j], akv[j]], 1), passes) for j in range(len(sub))]
    ry = [_dot(rb[j], aw[j], passes) for j in range(len(sub))]
    rkv = [_dot(rk[j], v_g[j], passes) for j in range(len(sub))]
    nh = len(heads)
    ahat = [sum(aw[i * nh + g][:, :LANES] for g in heads) for i in idx]
    w1 = [sum(aw[i * nh + g][:, LANES:] for g in heads) for i in idx]
    rhat = [rt[i] + sum(ry[i * nh + g][:, :LANES] for g in heads) for i in idx]
    y0 = [sum(ry[i * nh + g][:, LANES:] + rkv[i * nh + g] for g in heads) for i in idx]
    rowl = lax.broadcasted_iota(jnp.int32, (LANES, LANES), 0)
    coll = lax.broadcasted_iota(jnp.int32, (LANES, LANES), 1)
    diag_blocks = (rowl >> 6) == (coll >> 6)
    s = [x["s"] for x in ch]
    ys = [[None] * nc for _ in ch]
    for step in range(nc):
        ci = [step if x["fwd"] else nc - 1 - step for x in ch]
        sl = [slice(ci[i] * c, (ci[i] + 1) * c) for i in idx]
        xx = [_dot(jnp.concatenate([ahat[i][sl[i]], rhat[i][sl[i]]], 0), s[i], passes, _NT) for i in idx]
        u = [w1[i][sl[i]] + xx[i][:c] for i in idx]
        for i in idx:
            ys[i][ci[i]] = y0[i][sl[i]] + xx[i][c:]
        upd = [_dot(jnp.concatenate([u[i], ch[i]["v"][sl[i]]], 0),
                    jnp.concatenate([bp[i][sl[i]], kp[i][sl[i]]], 0), passes, _TN) for i in idx]
        s = [s[i] * jnp.exp(tot[i][ci[i] * c:ci[i] * c + 1]) + jnp.where(diag_blocks, upd[i], 0.0) for i in idx]
    return [jnp.concatenate(y, 0) for y in ys], s


def _rw_scan_kernel(r_ref, v_ref, kk_ref, lw_ref, kd_ref, bd_ref, *rest, passes, gp):
    s0_ref, y_ref, s_ref = rest if len(rest) == 3 else (None,) + rest
    L = r_ref.shape[0]
    nblk = L // SUPER

    def body(i, carry):
        chains, where = [], []
        for d in range(2):
            masks = _chunk_masks(SUPER, RW_CHUNK, d == 0)
            j = i if d == 0 else nblk - 1 - i
            rows = pl.ds(pl.multiple_of(j * SUPER, SUPER), SUPER)
            for g in range(gp):
                cols = slice(g * LANES, (g + 1) * LANES)
                chains.append(dict(r=r_ref[rows, cols], lw=lw_ref[d, rows, cols], k=kd_ref[d, rows, cols],
                                   v=v_ref[rows, cols], a=-kk_ref[rows, cols], b=bd_ref[d, rows, cols],
                                   s=carry[d * gp + g], masks=masks, fwd=d == 0))
                where.append((d, rows, cols))
        ys, ss = _rw_blocks(chains, passes)
        for (d, rows, cols), y in zip(where, ys):
            y_ref[d, rows, cols] = y
        return tuple(ss)

    zero = jnp.zeros((LANES, LANES), F32)
    s_fin = lax.fori_loop(0, nblk, body,
                          tuple(zero if s0_ref is None else s0_ref[d, g] for d in range(2) for g in range(gp)))
    for d in range(2):
        for g in range(gp):
            s_ref[d, g] = s_fin[d * gp + g]


def rwkv_scan(r, v, kk, lw, kd, bd, s0_bd, nseq, L, passes, gp):
    ngrp = RW_W // LANES
    w = gp * LANES
    one = pl.BlockSpec((L, w), lambda b, g: (b, g))
    two = pl.BlockSpec((2, L, w), lambda b, g: (0, b, g))
    st = pl.BlockSpec((None, 2, gp, LANES, LANES), lambda b, g: (b, 0, g, 0, 0))
    return pl.pallas_call(
        functools.partial(_rw_scan_kernel, passes=passes, gp=gp),
        grid=(nseq, ngrp // gp),
        in_specs=[one, one, one, two, two, two] + ([] if s0_bd is None else [st]),
        out_specs=[two, st],
        out_shape=[jax.ShapeDtypeStruct((2, nseq * L, RW_W), F32),
                   jax.ShapeDtypeStruct((nseq, 2, ngrp, LANES, LANES), F32)],
        compiler_params=_cparams("arbitrary", "arbitrary"),
    )(r, v, kk, lw, kd, bd, *(() if s0_bd is None else (s0_bd,)))


def _rw_post_kernel(y_ref, bon_ref, g_ref, lnw_ref, lnb_ref, ones_ref, o_ref):
    y = y_ref[0] + y_ref[1]
    ones = ones_ref[...]
    mean = _dot_exact_r(y, ones) * (1.0 / RW_N)
    yc = y - mean
    var = _dot_exact_r(yc * yc, ones) * (1.0 / RW_N)
    yn = yc * lax.rsqrt(var + RW_LN_EPS) * lnw_ref[...] + lnb_ref[...]
    o_ref[...] = ((yn + bon_ref[...]) * g_ref[...]).astype(o_ref.dtype)


def rwkv_post(y2, bonus, g, ln_w, ln_b):
    n = bonus.shape[0]
    tm = 512
    row = pl.BlockSpec((tm, RW_W), lambda i: (i, 0))
    vec = pl.BlockSpec((1, RW_W), lambda i: (0, 0))
    return pl.pallas_call(
        _rw_post_kernel,
        grid=(n // tm,),
        in_specs=[pl.BlockSpec((2, tm, RW_W), lambda i: (0, i, 0)), row, row, vec, vec,
                  pl.BlockSpec((RW_W, RW_W), lambda i: (0, 0))],
        out_specs=row,
        out_shape=jax.ShapeDtypeStruct((n, RW_W), BF16),
        compiler_params=_cparams("arbitrary"),
    )(y2, bonus, g, ln_w.reshape(1, RW_W), ln_b.reshape(1, RW_W),
      jnp.asarray(_block_ones(RW_W, RW_N)).astype(BF16))


def _rw_states_to_blockdiag(s):
    b = s.shape[0]
    s = s.reshape(b, 2, 4, 2, RW_N, RW_N)
    z = jnp.zeros_like(s[:, :, :, 0])
    top = jnp.concatenate([s[:, :, :, 0], z], -1)
    bot = jnp.concatenate([z, s[:, :, :, 1]], -1)
    return jnp.concatenate([top, bot], -2)


def _rw_states_from_blockdiag(s):
    b = s.shape[0]
    return jnp.stack([s[:, :, :, :RW_N, :RW_N], s[:, :, :, RW_N:, RW_N:]], 3).reshape(b, 2, RW_H, RW_N, RW_N)


def _gd_blocks(ch, passes):
    n = SUPER
    c = GD_CHUNK
    nc = n // c
    idx = range(len(ch))
    msk = [x["masks"] for x in ch]
    cum = [_dot_exact_l(x["masks"]["incl_bf"], x["lw"]) for x in ch]
    tot = [_chunk_totals(cum[i], c, ch[i]["fwd"]) for i in idx]
    suf = [tot[i] - cum[i] for i in idx]
    cum_row = [cum[i].T for i in idx]
    gam = [jnp.exp(jnp.where(msk[i]["incl"], cum[i][:, 0:1] - cum_row[i], -jnp.inf)) for i in idx]
    kb = [x["k"] * x["beta"] for x in ch]
    m = [_dot(jnp.concatenate([kb[i], ch[i]["q"]], 0), ch[i]["k"], passes, _NT) for i in idx]
    a = [jnp.where(msk[i]["strict"], m[i][:n] * gam[i], 0.0) for i in idx]
    qk = [jnp.where(msk[i]["incl"], m[i][n:] * gam[i], 0.0) for i in idx]
    tinv = _tri_inv([-x for x in a], msk, passes)
    e_cum = [jnp.exp(cum[i]) for i in idx]
    uw = [_dot(tinv[i], jnp.concatenate([ch[i]["v"] * ch[i]["beta"], kb[i] * e_cum[i]], 1), passes) for i in idx]
    qq = [_dot(qk[i], uw[i], passes) for i in idx]
    u = [uw[i][:, :LANES] for i in idx]
    w = [uw[i][:, LANES:] for i in idx]
    o0 = [qq[i][:, :LANES] for i in idx]
    qhat = [ch[i]["q"] * e_cum[i] - qq[i][:, LANES:] for i in idx]
    kd = [ch[i]["k"] * jnp.exp(suf[i]) for i in idx]
    s = [x["s"] for x in ch]
    os_ = [[None] * nc for _ in ch]
    for step in range(nc):
        ci = [step if x["fwd"] else nc - 1 - step for x in ch]
        sl = [slice(ci[i] * c, (ci[i] + 1) * c) for i in idx]
        xx = [_dot(jnp.concatenate([w[i][sl[i]], qhat[i][sl[i]]], 0), s[i], passes) for i in idx]
        vn = [u[i][sl[i]] - xx[i][:c] for i in idx]
        for i in idx:
            os_[i][ci[i]] = o0[i][sl[i]] + xx[i][c:]
        upd = [_dot(kd[i][sl[i]], vn[i], passes, _TN) for i in idx]
        s = [s[i] * jnp.exp(tot[i][ci[i] * c:ci[i] * c + 1]) + upd[i] for i in idx]
    return [jnp.concatenate(o, 0) for o in os_], s


def _gd_kernel(pq_ref, pk_ref, pv_ref, pz_ref, pab_ref, wq_ref, wk_ref, wv_ref, alog_ref, dtb_ref, ng_ref,
               *rest, passes, hg):
    s0_ref, o_ref, s_ref, q_scr, k_scr, v_scr, gb_scr, o_scr = rest if len(rest) == 8 else (None,) + rest
    L = pq_ref.shape[0]
    h0 = pl.program_id(1) * hg
    l2n = lambda t: t * lax.rsqrt(jnp.sum(t * t, -1, keepdims=True) + 1e-6)
    pab = pab_ref[...].astype(F32)
    gb_scr[0] = -jnp.exp(alog_ref[...]) * _softplus(pab + dtb_ref[...])
    gb_scr[1] = _sigmoid(pab)
    for hh in range(hg):
        cols = slice(hh * LANES, (hh + 1) * LANES)
        q_scr[:, cols] = l2n(_silu(_conv3(pq_ref[:, cols].astype(F32), wq_ref[:, cols]))) * (GD_DK ** -0.5)
        k_scr[:, cols] = l2n(_silu(_conv3(pk_ref[:, cols].astype(F32), wk_ref[:, cols])))
        v_scr[:, cols] = _silu(_conv3(pv_ref[:, cols].astype(F32), wv_ref[:, cols]))
    nblk = L // SUPER
    lane = lax.broadcasted_iota(jnp.int32, (SUPER, LANES), 1)

    def pick(tile, which):
        return jnp.broadcast_to(jnp.sum(jnp.where(lane == which, tile, 0.0), -1, keepdims=True), tile.shape)

    def body(i, carry):
        chains, where = [], []
        for d in range(2):
            masks = _chunk_masks(SUPER, GD_CHUNK, d == 0)
            j = i if d == 0 else nblk - 1 - i
            rows = pl.ds(pl.multiple_of(j * SUPER, SUPER), SUPER)
            g_t, b_t = gb_scr[0, rows, :], gb_scr[1, rows, :]
            for hh in range(hg):
                cols = slice(hh * LANES, (hh + 1) * LANES)
                chains.append(dict(q=q_scr[rows, cols], k=k_scr[rows, cols], v=v_scr[rows, cols],
                                   lw=pick(g_t, d * GD_H + h0 + hh), beta=pick(b_t, 2 * GD_H + d * GD_H + h0 + hh),
                                   s=carry[d * hg + hh], masks=masks, fwd=d == 0))
                where.append((d, rows, cols))
        os_, ss = _gd_blocks(chains, passes)
        for (d, rows, cols), o in zip(where, os_):
            o_scr[d, rows, cols] = o
        return tuple(ss)

    zero = jnp.zeros((LANES, LANES), F32)
    s_fin = lax.fori_loop(0, nblk, body,
                          tuple(zero if s0_ref is None else s0_ref[d, hh] for d in range(2) for hh in range(hg)))
    for d in range(2):
        for hh in range(hg):
            s_ref[d, hh] = s_fin[d * hg + hh]
    for hh in range(hg):
        cols = slice(hh * LANES, (hh + 1) * LANES)
        o = o_scr[0, :, cols] + o_scr[1, :, cols]
        o = o * lax.rsqrt(jnp.mean(o * o, -1, keepdims=True) + NORM_EPS) * ng_ref[...]
        o_ref[:, cols] = (o * _silu(pz_ref[:, cols].astype(F32))).astype(o_ref.dtype)


def gdn_mix(p, row0, nseq, L, conv_w, a_log, dt_bias, norm_g, s0, passes, hg):
    b0 = row0 // L
    w = hg * LANES
    nhb = GD_H // hg
    col = lambda part: pl.BlockSpec((L, w), lambda b, h: (b0 + b, part * nhb + h))
    wcol = lambda part: pl.BlockSpec((3, w), lambda b, h: (0, part * nhb + h))
    vec = pl.BlockSpec((1, LANES), lambda b, h: (0, 0))
    st = pl.BlockSpec((None, 2, hg, LANES, LANES), lambda b, h: (b, 0, h, 0, 0))
    alog_row = jnp.zeros((1, LANES), F32).at[0, :2 * GD_H].set(a_log.reshape(-1))
    dtb_row = jnp.zeros((1, LANES), F32).at[0, :2 * GD_H].set(dt_bias.reshape(-1))
    return pl.pallas_call(
        functools.partial(_gd_kernel, passes=passes, hg=hg),
        grid=(nseq, nhb),
        in_specs=[col(0), col(1), col(2), col(3),
                  pl.BlockSpec((L, LANES), lambda b, h: (b0 + b, 4 * GD_H)),
                  wcol(0), wcol(1), wcol(2), vec, vec, vec] + ([] if s0 is None else [st]),
        out_specs=[pl.BlockSpec((L, w), lambda b, h: (b, h)), st],
        out_shape=[jax.ShapeDtypeStruct((nseq * L, GD_H * LANES), BF16),
                   jax.ShapeDtypeStruct((nseq, 2, GD_H, LANES, LANES), F32)],
        scratch_shapes=[pltpu.VMEM((L, w), F32), pltpu.VMEM((L, w), F32), pltpu.VMEM((L, w), F32),
                        pltpu.VMEM((2, L, LANES), F32), pltpu.VMEM((2, L, w), F32)],
        compiler_params=_cparams("arbitrary", "arbitrary"),
    )(p, p, p, p, p, conv_w, conv_w, conv_w, alog_row, dtb_row, norm_g.reshape(1, LANES),
      *(() if s0 is None else (s0,)))


def _router_kernel(x_ref, g_ref, sc_ref, sh_ref, wr_ref, h_ref, gate_ref, idx_ref, cnt_ref, cnt_scr):
    h = _norm_mod(x_ref[...], g_ref[...], sc_ref[...], sh_ref[...])
    h_ref[...] = h
    logits = _dot(h, wr_ref[...], 6)
    lane = lax.broadcasted_iota(jnp.int32, logits.shape, 1)
    logits = jnp.where(lane < N_EXP, logits, -jnp.inf)
    m1 = jnp.max(logits, -1, keepdims=True)
    i1 = jnp.min(jnp.where(logits == m1, lane, LANES), -1, keepdims=True)
    rest = jnp.where(lane == i1, -jnp.inf, logits)
    m2 = jnp.max(rest, -1, keepdims=True)
    i2 = jnp.min(jnp.where(rest == m2, lane, LANES), -1, keepdims=True)
    e2 = jnp.exp(m2 - m1)
    g1 = 1.0 / (1.0 + e2)
    g2 = e2 / (1.0 + e2)
    gate_ref[...] = jnp.where(lane == 0, g1, 0.0) + jnp.where(lane == 1, g2, 0.0)

    @pl.when(pl.program_id(0) == 0)
    def _():
        cnt_scr[...] = jnp.zeros_like(cnt_scr)

    tm = logits.shape[0]
    hot = jnp.where((lane == i1) | (lane == i2), 1.0, 0.0)
    row = lax.broadcasted_iota(jnp.int32, (tm, tm), 0)
    col = lax.broadcasted_iota(jnp.int32, (tm, tm), 1)
    before = _dg(_mask_bf16(col < row), hot.astype(BF16)) + cnt_scr[0:1]
    r1 = jnp.sum(jnp.where(lane == i1, before, 0.0), -1, keepdims=True).astype(jnp.int32)
    r2 = jnp.sum(jnp.where(lane == i2, before, 0.0), -1, keepdims=True).astype(jnp.int32)
    idx_ref[...] = (jnp.where(lane == 0, i1, 0) + jnp.where(lane == 1, i2, 0)
                    + jnp.where(lane == 2, r1, 0) + jnp.where(lane == 3, r2, 0))
    cnt_scr[...] = cnt_scr[...] + jnp.sum(hot, 0, keepdims=True)
    cnt_ref[...] = cnt_scr[...]


def moe_router(x, g, sc_t, sh_t, router):
    tm = 512
    per = TM // tm
    wr = jnp.zeros((D, LANES), F32).at[:, :N_EXP].set(router)
    return pl.pallas_call(
        _router_kernel,
        grid=(T_ALL // tm,),
        in_specs=[pl.BlockSpec((tm, D), lambda i: (i, 0)),
                  pl.BlockSpec((1, D), lambda i: (0, 0)),
                  pl.BlockSpec((None, 1, D), lambda i: (i // per, 0, 0)),
                  pl.BlockSpec((None, 1, D), lambda i: (i // per, 0, 0)),
                  pl.BlockSpec((D, LANES), lambda i: (0, 0))],
        out_specs=[pl.BlockSpec((tm, D), lambda i: (i, 0)), pl.BlockSpec((tm, LANES), lambda i: (i, 0)),
                   pl.BlockSpec((tm, LANES), lambda i: (i, 0)), pl.BlockSpec((8, LANES), lambda i: (0, 0))],
        out_shape=[jax.ShapeDtypeStruct((T_ALL, D), F32), jax.ShapeDtypeStruct((T_ALL, LANES), F32),
                   jax.ShapeDtypeStruct((T_ALL, LANES), jnp.int32), jax.ShapeDtypeStruct((8, LANES), F32)],
        scratch_shapes=[pltpu.VMEM((8, LANES), F32)],
        compiler_params=_cparams("arbitrary"),
    )(x, g.reshape(1, D), sc_t, sh_t, wr)


def moe_slot_positions(idx, cnt):
    e_flat = idx[:, 0:2].reshape(-1)
    rank = idx[:, 2:4].reshape(-1)
    onehot = (e_flat[:, None] == jnp.arange(N_EXP, dtype=jnp.int32)[None, :]).astype(jnp.int32)
    counts = cnt[0, :N_EXP].astype(jnp.int32)
    gsize = ((counts + MOE_TILE - 1) // MOE_TILE) * MOE_TILE
    gend = jnp.cumsum(gsize)
    pos = jnp.sum(onehot * (gend - gsize)[None, :], axis=1) + rank
    tile_start = jnp.arange(MOE_TILES, dtype=jnp.int32) * MOE_TILE
    tile_expert = jnp.minimum(jnp.sum((gend[None, :] <= tile_start[:, None]).astype(jnp.int32), axis=1), N_EXP - 1)
    return pos.astype(jnp.int32), tile_expert.astype(jnp.int32), (gend[-1:] // MOE_TILE).astype(jnp.int32)


def _row_copy(src, s, dst, d, sem):
    return pltpu.make_async_copy(src.at[pl.ds(s, 1)], dst.at[pl.ds(d, 1)], sem)


def _dispatch_kernel(pos_ref, h_ref, xs_in_ref, xs_ref, sem):
    del xs_in_ref
    tm = h_ref.shape[0]
    base = pl.program_id(0) * tm

    def issue(r, carry):
        for k in range(2):
            _row_copy(h_ref, r, xs_ref, pos_ref[(base + r) * 2 + k], sem).start()
        return carry

    lax.fori_loop(0, tm, issue, 0, unroll=8)
    for k in range(2):
        pltpu.make_async_copy(h_ref, xs_ref.at[pl.ds(0, tm)], sem).wait()


def moe_dispatch(h, pos):
    tm = 512
    return pl.pallas_call(
        _dispatch_kernel,
        grid_spec=pltpu.PrefetchScalarGridSpec(
            num_scalar_prefetch=1,
            grid=(T_ALL // tm,),
            in_specs=[pl.BlockSpec((tm, D), lambda i, pos: (i, 0)), pl.BlockSpec(memory_space=pl.ANY)],
            out_specs=pl.BlockSpec(memory_space=pl.ANY),
            scratch_shapes=[pltpu.SemaphoreType.DMA]),
        out_shape=jax.ShapeDtypeStruct((MOE_ROWS, D), F32),
        input_output_aliases={2: 0},
        compiler_params=_cparams("arbitrary"),
    )(pos, h, jnp.zeros((MOE_ROWS, D), F32))


def _expert_kernel(te_ref, nu_ref, xs_ref, wg_ref, wu_ref, wo_ref, o_ref, x_scr, acc):
    i = pl.program_id(0)
    f = pl.program_id(1)
    used = i < nu_ref[0]

    @pl.when(used & (f == 0))
    def _():
        x_scr[...] = xs_ref[...].astype(BF16)
        acc[...] = jnp.zeros_like(acc)

    @pl.when(used)
    def _():
        x = x_scr[...]
        act = _silu(_dg(x, wg_ref[...])) * _dg(x, wu_ref[...])
        acc[...] += _dg(act.astype(BF16), wo_ref[...])

    last = f == pl.num_programs(1) - 1

    @pl.when(used & last)
    def _():
        o_ref[...] = acc[...]

    @pl.when(jnp.logical_not(used) & last)
    def _():
        o_ref[...] = jnp.zeros_like(o_ref)


def moe_experts(xs, tile_expert, n_used, w_in_bf16, w_out_bf16):
    tn = 512
    nf = E_FF // tn
    live = lambda i, f, nu: jnp.where(i < nu[0], f, 0)
    return pl.pallas_call(
        _expert_kernel,
        grid_spec=pltpu.PrefetchScalarGridSpec(
            num_scalar_prefetch=2,
            grid=(MOE_TILES, nf),
            in_specs=[pl.BlockSpec((MOE_TILE, D), lambda i, f, te, nu: (i, 0)),
                      pl.BlockSpec((None, D, tn), lambda i, f, te, nu: (te[i], 0, live(i, f, nu))),
                      pl.BlockSpec((None, D, tn), lambda i, f, te, nu: (te[i], 0, live(i, f, nu) + nf)),
                      pl.BlockSpec((None, tn, D), lambda i, f, te, nu: (te[i], live(i, f, nu), 0))],
            out_specs=pl.BlockSpec((MOE_TILE, D), lambda i, f, te, nu: (i, 0)),
            scratch_shapes=[pltpu.VMEM((MOE_TILE, D), BF16), pltpu.VMEM((MOE_TILE, D), F32)]),
        out_shape=jax.ShapeDtypeStruct((MOE_ROWS, D), F32),
        compiler_params=_cparams("arbitrary", "arbitrary"),
    )(tile_expert, n_used, xs, w_in_bf16, w_in_bf16, w_out_bf16)


def _combine_kernel(pos_ref, ys_ref, gates_ref, x_ref, gate_ref, fg_ref, oc_ref, od_ref, y_scr, sem):
    tm = x_ref.shape[0]
    i = pl.program_id(0)
    base = i * tm

    def issue(r, carry):
        for k in range(2):
            _row_copy(ys_ref, pos_ref[(base + r) * 2 + k], y_scr.at[k], r, sem).start()
        return carry

    lax.fori_loop(0, tm, issue, 0, unroll=8)
    for k in range(2):
        pltpu.make_async_copy(ys_ref.at[pl.ds(0, tm)], y_scr.at[k], sem).wait()
    gates = gates_ref[...]
    lane = lax.broadcasted_iota(jnp.int32, gates.shape, 1)
    g0 = jnp.sum(jnp.where(lane == 0, gates, 0.0), -1, keepdims=True)
    g1 = jnp.sum(jnp.where(lane == 1, gates, 0.0), -1, keepdims=True)
    xn = x_ref[...] + gate_ref[...] * (y_scr[0] * g0 + y_scr[1] * g1)
    y = xn * lax.rsqrt(jnp.mean(xn * xn, -1, keepdims=True) + NORM_EPS) * fg_ref[...]

    @pl.when(i < T_CTX // tm)
    def _():
        oc_ref[...] = y

    @pl.when(i >= T_CTX // tm)
    def _():
        od_ref[...] = y


def moe_combine_final(ys, pos, gates, x, gate_t, final_g):
    tm = 256
    per = TM // tm
    nc = T_CTX // tm
    return pl.pallas_call(
        _combine_kernel,
        grid_spec=pltpu.PrefetchScalarGridSpec(
            num_scalar_prefetch=1,
            grid=(T_ALL // tm,),
            in_specs=[pl.BlockSpec(memory_space=pl.ANY),
                      pl.BlockSpec((tm, LANES), lambda i, pos: (i, 0)),
                      pl.BlockSpec((tm, D), lambda i, pos: (i, 0)),
                      pl.BlockSpec((None, 1, D), lambda i, pos: (i // per, 0, 0)),
                      pl.BlockSpec((1, D), lambda i, pos: (0, 0))],
            out_specs=[pl.BlockSpec((tm, D), lambda i, pos: (jnp.minimum(i, nc - 1), 0)),
                       pl.BlockSpec((tm, D), lambda i, pos: (jnp.maximum(i - nc, 0), 0))],
            scratch_shapes=[pltpu.VMEM((2, tm, D), F32), pltpu.SemaphoreType.DMA]),
        out_shape=[jax.ShapeDtypeStruct((T_CTX, D), F32), jax.ShapeDtypeStruct((T_DEN, D), F32)],
        compiler_params=_cparams("arbitrary"),
    )(pos, ys, gates, x, gate_t, final_g.reshape(1, D))


def _tile_rows(mod_l, k):
    cols = mod_l[:, k * D:(k + 1) * D]
    ctx = jnp.broadcast_to(cols[0:1], (T_CTX // TM, D))
    den = jnp.repeat(cols[1:1 + B_DEN], L_DEN // TM, axis=0)
    return jnp.concatenate([ctx, den], 0)[:, None, :]


def _pad_cols(w, n):
    return jnp.pad(w, ((0, 0), (0, n - w.shape[1])))


def kernel(x_prompt, x_sample, state_rwkv, state_gdn, c, c_ctx, ada_w, ada_b, norm_mix_g, norm_ffn_g, final_norm_g, e_w_in, e_hy_conv_w, e_hy_conv_b, e_hf_w1, e_hf_b1, e_hf_freq1, e_hf_w2, e_hf_b2, e_hf_freq2, e_hf_w3, e_hy_bias, e_rw_mu, e_rw_w0, e_rw_w2, e_rw_a0, e_rw_a2, e_rw_g2, e_rw_kk, e_rw_ka, e_rw_rk, e_rw_ln_w, e_rw_ln_b, e_w_out, e_ffn_w_in, e_ffn_w_out, o_w_in, o_conv_w, o_A_log, o_dt_bias, o_norm_g, o_w_out, o_router, o_moe_w_in, o_moe_w_out):
    passes = 1
    cond16 = jnp.zeros((16, D), F32).at[0].set(c_ctx).at[1:1 + B_DEN].set(c)
    mod = modulation(cond16, ada_w, ada_b)
    x = assemble_tokens(x_prompt, x_sample)

    m0 = [_tile_rows(mod[0], k) for k in range(6)]
    p = norm_matmul(x, norm_mix_g[0], m0[1], m0[0], _pad_cols(e_w_in[0], P_EVEN_PAD).astype(BF16),
                    P_EVEN_PAD // 2, BF16)
    ys_hy, ys_rw, st_rw = [], [], None
    for row0, nseq, L in ((0, B_CTX, L_CTX), (T_CTX, B_DEN, L_DEN)):
        spec = hyena_filter_spectrum(L, e_hf_w1[0], e_hf_b1[0], e_hf_freq1[0], e_hf_w2[0], e_hf_b2[0],
                                     e_hf_freq2[0], e_hf_w3[0])
        ys_hy.append(hyena_mix(p, row0, nseq, L, e_hy_conv_w[0], e_hy_conv_b[0].reshape(1, -1), spec,
                               e_hy_bias[0], 1))
        r, v, kk, lw, kd, bd, bonus, g = rwkv_prep(p, row0, nseq, L, e_rw_mu[0], e_rw_w0[0], e_rw_w2[0],
                                                   e_rw_a0[0], e_rw_a2[0], e_rw_g2[0], e_rw_kk[0], e_rw_ka[0],
                                                   e_rw_rk[0])
        s0 = None if row0 == 0 else _rw_states_to_blockdiag(state_rwkv[:, 0])
        y2, s_new = rwkv_scan(r, v, kk, lw, kd, bd, s0, nseq, L, passes, 4)
        if row0 == 0:
            st_rw = _rw_states_from_blockdiag(s_new)
        ys_rw.append(rwkv_post(y2, bonus, g, e_rw_ln_w[0], e_rw_ln_b[0]))
    x = matmul_residual_split([ys_hy, ys_rw], e_w_out[0].astype(BF16), x, m0[2])
    act = norm_swiglu(x, norm_ffn_g[0], m0[4], m0[3], e_ffn_w_in[0].astype(BF16), D_FF, D_FF // 2)
    x = matmul_residual(act, e_ffn_w_out[0].astype(BF16), x, m0[5], D // 2)

    m1 = [_tile_rows(mod[1], k) for k in range(6)]
    p = norm_matmul(x, norm_mix_g[1], m1[1], m1[0], _pad_cols(o_w_in[0], P_ODD_PAD).astype(BF16),
                    P_ODD_PAD // 3, BF16)
    os_, st_gd = [], None
    for row0, nseq, L in ((0, B_CTX, L_CTX), (T_CTX, B_DEN, L_DEN)):
        s0 = None if row0 == 0 else state_gdn[:, 0]
        o, s_new = gdn_mix(p, row0, nseq, L, o_conv_w[0], o_A_log[0], o_dt_bias[0], o_norm_g[0], s0, passes, 8)
        if row0 == 0:
            st_gd = s_new
        os_.append(o)
    x = matmul_residual_split([os_], o_w_out[0].astype(BF16), x, m1[2])
    h, gates, idx, cnt = moe_router(x, norm_ffn_g[1], m1[4], m1[3], o_router[0])
    pos, tile_expert, n_used = moe_slot_positions(idx, cnt)
    ys = moe_experts(moe_dispatch(h, pos), tile_expert, n_used, o_moe_w_in[0].astype(BF16),
                     o_moe_w_out[0].astype(BF16))
    y_ctx, y_den = moe_combine_final(ys, pos, gates, x, m1[5], final_norm_g)

    y_prompt = y_ctx.reshape(B_CTX, L_CTX, D)
    y_sample = y_den.reshape(B_DEN, L_DEN, D)
    return (y_prompt, y_sample, st_rw[:, None], st_gd[:, None])
```
